```python
import math
import jax
import jax.numpy as jnp
from jax import lax
import numpy as np

D_MODEL = 1024
BATCH = 8
SEQ = 4096
DEPTH = 4

GRID_W = 64
CTX_LEN = 256
EPS = 1e-6
F32 = jnp.float32
N_EVEN = (DEPTH + 1) // 2
N_ODD = DEPTH // 2

SSD_HEADS = 16
SSD_HEAD_DIM = 64
SSD_INNER = SSD_HEADS * SSD_HEAD_DIM
SSD_GROUPS = 2
SSD_R = SSD_HEADS // SSD_GROUPS
SSD_STATE = 128
SSD_BC = SSD_GROUPS * SSD_STATE
SSD_CHUNK = 128
DT_MIN = 1e-3
DT_MAX = 1e-1
MLSTM_HEADS = 4
MLSTM_QK_DIM = 128
MLSTM_V_DIM = 256
MLSTM_QK = MLSTM_HEADS * MLSTM_QK_DIM
MLSTM_INNER = MLSTM_HEADS * MLSTM_V_DIM
MLSTM_CHUNK = 128
CONV_K = 3
CONV_CH = SSD_INNER + 2 * SSD_BC + 2 * MLSTM_QK
CONV_SPLITS = (SSD_INNER, SSD_BC, SSD_BC, MLSTM_QK, MLSTM_QK)
EVEN_SPLITS = (CONV_CH, SSD_INNER, MLSTM_INNER, MLSTM_INNER, 2 * SSD_HEADS, 2 * MLSTM_HEADS, 2 * MLSTM_HEADS)
EVEN_PROJ = sum(EVEN_SPLITS)
EVEN_MIX = SSD_INNER + MLSTM_INNER
ATT_HEAD_DIM = 64
SWA_HEADS = 8
SWA_KV_HEADS = 2
DENSE_HEADS = 8
DENSE_KV_HEADS = 2
WINDOW = 128
BLOCK = 128
ROPE_THETA = 10000.0
ROPE_AXIS_DIM = ATT_HEAD_DIM // 2
ODD_SPLITS = (SWA_HEADS * ATT_HEAD_DIM, SWA_KV_HEADS * ATT_HEAD_DIM, SWA_KV_HEADS * ATT_HEAD_DIM,
              DENSE_HEADS * ATT_HEAD_DIM, DENSE_KV_HEADS * ATT_HEAD_DIM, DENSE_KV_HEADS * ATT_HEAD_DIM)
ODD_PROJ = sum(ODD_SPLITS)
ODD_MIX = (SWA_HEADS + DENSE_HEADS) * ATT_HEAD_DIM
N_GROUPS = 4
EXPERTS_PER_GROUP = 4
EXPERT_FF = 256
TOP_K = 2

kernel_name = 'hybrid_ssd_mlstm_swa_dense_hmoe_prefix_dit'


def _split(t, sizes):
    cuts = [int(i) for i in np.cumsum(sizes)[:-1]]
    return jnp.split(t, cuts, axis=-1)


def rmsnorm(x, w):
    xf = x.astype(F32)
    y = xf * lax.rsqrt(jnp.mean(xf * xf, axis=-1, keepdims=True) + EPS)
    return (y * w.astype(F32)).astype(x.dtype)


def group_rmsnorm(x, w, groups):
    shp = x.shape
    xg = x.astype(F32).reshape(shp[:-1] + (groups, shp[-1] // groups))
    xg = xg * lax.rsqrt(jnp.mean(xg * xg, axis=-1, keepdims=True) + EPS)
    return xg.reshape(shp) * w.astype(F32)


def adaln(cond, w, b):
    m = cond @ w + b
    return [t[:, None, :] for t in jnp.split(m, 6, axis=-1)]


def modulate(h, shift, scale):
    return h * (1 + scale) + shift


def rope_tables(S):
    rows = S // GRID_W
    row = jnp.repeat(jnp.arange(rows), GRID_W).astype(F32)
    col = (jnp.arange(rows * GRID_W) % GRID_W).astype(F32)
    inv = ROPE_THETA ** (-jnp.arange(0, ROPE_AXIS_DIM, 2, dtype=F32) / ROPE_AXIS_DIM)
    ang = jnp.concatenate([row[:, None] * inv, col[:, None] * inv], axis=-1)
    return jnp.cos(ang), jnp.sin(ang)


def apply_rope(x, cos, sin):
    xp = x.astype(F32).reshape(x.shape[:-1] + (-1, 2))
    x1, x2 = xp[..., 0], xp[..., 1]
    cs, sn = cos[:, None, :], sin[:, None, :]
    out = jnp.stack([x1 * cs - x2 * sn, x1 * sn + x2 * cs], axis=-1)
    return out.reshape(x.shape).astype(x.dtype)


def dwconv(x, w, b):
    C = x.shape[-1]
    y = lax.conv_general_dilated(x, w.astype(x.dtype)[:, None, :], window_strides=(1,),
                                 padding=[(CONV_K // 2, CONV_K // 2)],
                                 dimension_numbers=('NWC', 'WIO', 'NWC'), feature_group_count=C)
    return y + b


def _flip(t):
    return jnp.flip(t, axis=1)


def ssd_scan(xs, dA, Bm, Cm, h0):
    b, L, g, r, p = xs.shape
    Q = SSD_CHUNK
    c = L // Q
    xs = xs.reshape(b, c, Q, g, r, p)
    Bm = Bm.reshape(b, c, Q, g, SSD_STATE)
    Cm = Cm.reshape(b, c, Q, g, SSD_STATE)
    a_cum = jnp.cumsum(jnp.moveaxis(dA.reshape(b, c, Q, g, r), 2, -1), axis=-1)
    causal = jnp.tril(jnp.ones((Q, Q), dtype=bool))
    seg = a_cum[..., :, None] - a_cum[..., None, :]
    Lmat = jnp.exp(jnp.where(causal, seg, -jnp.inf))
    CB = jnp.einsum('bclgn,bcsgn->bcgls', Cm, Bm)
    y_diag = jnp.einsum('bcgls,bcgrls,bcsgrp->bclgrp', CB, Lmat, xs)
    decay_to_end = jnp.exp(a_cum[..., -1:] - a_cum)
    chunk_states = jnp.einsum('bclgn,bcgrl,bclgrp->bcgrpn', Bm, decay_to_end, xs)
    chunk_decay = jnp.exp(a_cum[..., -1])

    def step(h, inp):
        s, d = inp
        return d[..., None, None] * h + s, h

    h_last, h_in = lax.scan(step, h0, (jnp.moveaxis(chunk_states, 1, 0), jnp.moveaxis(chunk_decay, 1, 0)))
    h_in = jnp.moveaxis(h_in, 0, 1)
    y_off = jnp.einsum('bclgn,bcgrpn,bcgrl->bclgrp', Cm, h_in, jnp.exp(a_cum))
    return (y_diag + y_off).reshape(b, L, g, r, p), h_last


def ssd_bidir(xs, Bm, Cm, dt_raw, dt_bias, a_log, d_skip, h0_f, h0_b):
    b, L, _ = xs.shape
    x = xs.reshape(b, L, SSD_GROUPS, SSD_R, SSD_HEAD_DIM)
    Bm = Bm.reshape(b, L, SSD_GROUPS, SSD_STATE)
    Cm = Cm.reshape(b, L, SSD_GROUPS, SSD_STATE)
    dt = jax.nn.softplus(dt_raw.reshape(b, L, 2, SSD_HEADS) + dt_bias)
    dA = (dt * (-jnp.exp(a_log))).reshape(b, L, 2, SSD_GROUPS, SSD_R)
    dtg = dt.reshape(b, L, 2, SSD_GROUPS, SSD_R)
    y_f, hf = ssd_scan(x * dtg[:, :, 0, :, :, None], dA[:, :, 0], Bm, Cm, h0_f)
    y_b, hb = ssd_scan(_flip(x * dtg[:, :, 1, :, :, None]), _flip(dA[:, :, 1]), _flip(Bm), _flip(Cm), h0_b)
    y = y_f + _flip(y_b) + x * d_skip.reshape(SSD_GROUPS, SSD_R)[..., None]
    return y.reshape(b, L, SSD_INNER), hf, hb


def mlstm_scan(q, k, v, ig, lf, state):
    b, L, H, dk = q.shape
    dv = v.shape[-1]
    Q = MLSTM_CHUNK
    c = L // Q
    q = q.reshape(b, c, Q, H, dk)
    k = k.reshape(b, c, Q, H, dk)
    v = v.reshape(b, c, Q, H, dv)
    ig = jnp.moveaxis(ig.reshape(b, c, Q, H), 2, -1)
    bcum = jnp.cumsum(jnp.moveaxis(lf.reshape(b, c, Q, H), 2, -1), axis=-1)
    causal = jnp.tril(jnp.ones((Q, Q), dtype=bool))
    Dlog = jnp.where(causal, bcum[..., :, None] - bcum[..., None, :] + ig[..., None, :], -jnp.inf)
    w_end = bcum[..., -1:] - bcum + ig
    m_loc = jnp.max(w_end, axis=-1)
    e_end = jnp.exp(w_end - m_loc[..., None])
    C_loc = jnp.einsum('bchs,bcshk,bcshv->bchkv', e_end, k, v)
    n_loc = jnp.einsum('bchs,bcshk->bchk', e_end, k)
    f_chunk = bcum[..., -1]

    def step(carry, inp):
        C, n, m = carry
        Cl, nl, ml, fc = inp
        m_new = jnp.maximum(fc + m, ml)
        a = jnp.exp(fc + m - m_new)
        bb = jnp.exp(ml - m_new)
        C_new = a[..., None, None] * C + bb[..., None, None] * Cl
        n_new = a[..., None] * n + bb[..., None] * nl
        return (C_new, n_new, m_new), (C, n, m)

    final, (C_in, n_in, m_in) = lax.scan(
        step, state,
        (jnp.moveaxis(C_loc, 1, 0), jnp.moveaxis(n_loc, 1, 0), jnp.moveaxis(m_loc, 1, 0), jnp.moveaxis(f_chunk, 1, 0)))
    C_in = jnp.moveaxis(C_in, 0, 1)
    n_in = jnp.moveaxis(n_in, 0, 1)
    m_in = jnp.moveaxis(m_in, 0, 1)
    m_inter = bcum + m_in[..., None]
    m_t = jnp.maximum(m_inter, jnp.max(Dlog, axis=-1))
    P = jnp.exp(Dlog - m_t[..., None]) * jnp.einsum('bclhk,bcshk->bchls', q, k)
    w_inter = jnp.exp(m_inter - m_t)
    num = jnp.einsum('bchls,bcshv->bclhv', P, v) + jnp.einsum('bchl,bclhk,bchkv->bclhv', w_inter, q, C_in)
    den = jnp.sum(P, axis=-1) + w_inter * jnp.einsum('bclhk,bchk->bchl', q, n_in)
    den = jnp.maximum(jnp.abs(den), jnp.exp(-m_t))
    h = num / jnp.moveaxis(den, -1, 2)[..., None]
    return h.reshape(b, L, H, dv), final


def mlstm_bidir(q, k, v, ig_raw, fg_raw, ig_bias, fg_bias, st_f, st_b):
    b, L = q.shape[:2]
    ig = ig_raw.reshape(b, L, 2, MLSTM_HEADS) + ig_bias
    lf = jax.nn.log_sigmoid(fg_raw.reshape(b, L, 2, MLSTM_HEADS) + fg_bias)
    h_f, sf = mlstm_scan(q, k, v, ig[:, :, 0], lf[:, :, 0], st_f)
    h_b, sb = mlstm_scan(_flip(q), _flip(k), _flip(v), _flip(ig[:, :, 1]), _flip(lf[:, :, 1]), st_b)
    return h_f + _flip(h_b), sf, sb


def zero_states(b):
    h0 = jnp.zeros((b, SSD_GROUPS, SSD_R, SSD_HEAD_DIM, SSD_STATE), F32)
    st = (jnp.zeros((b, MLSTM_HEADS, MLSTM_QK_DIM, MLSTM_V_DIM), F32),
          jnp.zeros((b, MLSTM_HEADS, MLSTM_QK_DIM), F32),
          jnp.zeros((b, MLSTM_HEADS), F32))
    return (h0, h0, st, st)


def recurrent_heads(u, conv_w, conv_b, dt_bias, a_log, d_skip, ssd_norm_w, ig_bias, fg_bias, mlstm_norm_w, init):
    b, L, _ = u.shape
    conv_in, z, v, o, dt_raw, ig_raw, fg_raw = _split(u, EVEN_SPLITS)
    conv_out = jax.nn.silu(dwconv(conv_in, conv_w, conv_b)).astype(F32)
    xs, Bm, Cm, q, k = _split(conv_out, CONV_SPLITS)
    h0_f, h0_b, st_f, st_b = init
    y_ssd, hf, hb = ssd_bidir(xs, Bm, Cm, dt_raw.astype(F32), dt_bias, a_log, d_skip, h0_f, h0_b)
    y_ssd = group_rmsnorm(y_ssd * jax.nn.silu(z.astype(F32)), ssd_norm_w, SSD_GROUPS)
    hm, sf, sb = mlstm_bidir(q.reshape(b, L, MLSTM_HEADS, MLSTM_QK_DIM),
                             k.reshape(b, L, MLSTM_HEADS, MLSTM_QK_DIM) * (MLSTM_QK_DIM ** -0.5),
                             v.astype(F32).reshape(b, L, MLSTM_HEADS, MLSTM_V_DIM),
                             ig_raw.astype(F32), fg_raw.astype(F32), ig_bias, fg_bias, st_f, st_b)
    hm = group_rmsnorm(hm.reshape(b, L, MLSTM_INNER), mlstm_norm_w, MLSTM_HEADS) * jax.nn.sigmoid(o.astype(F32))
    return jnp.concatenate([y_ssd, hm], axis=-1), (hf, hb, sf, sb)


def recurrent_mixer(a_ctx, a_lat, w_in, conv_w, conv_b, dt_bias, a_log, d_skip, ssd_norm_w, ig_bias, fg_bias,
                    mlstm_norm_w, w_out):
    y_ctx, finals = recurrent_heads(a_ctx @ w_in, conv_w, conv_b, dt_bias, a_log, d_skip, ssd_norm_w, ig_bias,
                                    fg_bias, mlstm_norm_w, zero_states(a_ctx.shape[0]))
    y_lat, _ = recurrent_heads(a_lat @ w_in, conv_w, conv_b, dt_bias, a_log, d_skip, ssd_norm_w, ig_bias,
                               fg_bias, mlstm_norm_w, finals)
    return y_ctx.astype(a_ctx.dtype) @ w_out, y_lat.astype(a_lat.dtype) @ w_out


def full_attention(q, k, v, sink):
    b, T, H, dh = q.shape
    KV = k.shape[2]
    G = H // KV
    s = jnp.einsum('bqkgd,btkd->bkgqt', q.reshape(b, T, KV, G, dh), k).astype(F32) * (dh ** -0.5)
    if sink is not None:
        sk = jnp.broadcast_to(sink.astype(F32).reshape(KV, G, 1, 1), s.shape[:-1] + (1,))
        p = jax.nn.softmax(jnp.concatenate([sk, s], axis=-1), axis=-1)[..., 1:]
    else:
        p = jax.nn.softmax(s, axis=-1)
    o = jnp.einsum('bkgqt,btkd->bqkgd', p.astype(v.dtype), v)
    return o.reshape(b, T, H * dh)


def window_attention(q, k, v, k_ctx, v_ctx, sink):
    b, S, H, dh = q.shape
    KV = k.shape[2]
    G = H // KV
    T = k_ctx.shape[1]
    nb = S // BLOCK
    qg = q.reshape(b, S, KV, G, dh)
    pad = ((0, 0), (BLOCK, BLOCK), (0, 0), (0, 0))
    kp = jnp.pad(k, pad)
    vp = jnp.pad(v, pad)
    scale = dh ** -0.5
    qpos = BLOCK + jnp.arange(BLOCK)
    kpos = jnp.arange(3 * BLOCK)
    in_window = jnp.abs(kpos[None, :] - qpos[:, None]) <= WINDOW
    sink_logit = sink.astype(F32).reshape(KV, G, 1, 1)

    def one_block(j):
        start = j * BLOCK
        qb = lax.dynamic_slice_in_dim(qg, start, BLOCK, axis=1)
        kb = lax.dynamic_slice_in_dim(kp, start, 3 * BLOCK, axis=1)
        vb = lax.dynamic_slice_in_dim(vp, start, 3 * BLOCK, axis=1)
        kabs = start - BLOCK + kpos
        valid = in_window & ((kabs >= 0) & (kabs < S))[None, :]
        s_loc = jnp.where(valid, jnp.einsum('bqkgd,bskd->bkgqs', qb, kb).astype(F32) * scale, -jnp.inf)
        s_ctx = jnp.einsum('bqkgd,btkd->bkgqt', qb, k_ctx).astype(F32) * scale
        s_sink = jnp.broadcast_to(sink_logit, s_ctx.shape[:-1] + (1,))
        p = jax.nn.softmax(jnp.concatenate([s_sink, s_ctx, s_loc], axis=-1), axis=-1).astype(v.dtype)
        return (jnp.einsum('bkgqt,btkd->bqkgd', p[..., 1:1 + T], v_ctx)
                + jnp.einsum('bkgqs,bskd->bqkgd', p[..., 1 + T:], vb))

    out = lax.map(one_block, jnp.arange(nb))
    return jnp.moveaxis(out, 0, 1).reshape(b, S, H * dh)


def dense_block_attention(q, k_all, v_all):
    b, S, H, dh = q.shape
    KV = k_all.shape[2]
    G = H // KV
    nb = S // BLOCK
    qb = jnp.moveaxis(q.reshape(b, nb, BLOCK, KV, G, dh), 1, 0)

    def one_block(qblk):
        s = jnp.einsum('bqkgd,btkd->bkgqt', qblk, k_all).astype(F32) * (dh ** -0.5)
        p = jax.nn.softmax(s, axis=-1).astype(v_all.dtype)
        return jnp.einsum('bkgqt,btkd->bqkgd', p, v_all)

    out = lax.map(one_block, qb)
    return jnp.moveaxis(out, 0, 1).reshape(b, S, H * dh)


def attention_mixer(a_ctx, a_lat, w_in, sink, q_norm_w, k_norm_w, w_out, cos, sin, need_ctx):
    def project(a):
        b, L, _ = a.shape
        qs, ks, vs, qd, kd, vd = _split(a @ w_in, ODD_SPLITS)
        hd = lambda t, n: t.reshape(b, L, n, ATT_HEAD_DIM)
        return (hd(qs, SWA_HEADS), hd(ks, SWA_KV_HEADS), hd(vs, SWA_KV_HEADS),
                rmsnorm(hd(qd, DENSE_HEADS), q_norm_w), rmsnorm(hd(kd, DENSE_KV_HEADS), k_norm_w),
                hd(vd, DENSE_KV_HEADS))

    qs_c, ks_c, vs_c, qd_c, kd_c, vd_c = project(a_ctx)
    qs_l, ks_l, vs_l, qd_l, kd_l, vd_l = project(a_lat)
    rot = lambda t: apply_rope(t, cos, sin)
    y_swa = window_attention(rot(qs_l), rot(ks_l), vs_l, ks_c, vs_c, sink)
    y_den = dense_block_attention(rot(qd_l), jnp.concatenate([kd_c, rot(kd_l)], axis=1),
                                  jnp.concatenate([vd_c, vd_l], axis=1))
    y_lat = jnp.concatenate([y_swa, y_den], axis=-1) @ w_out
    y_ctx = None
    if need_ctx:
        y_ctx = jnp.concatenate([full_attention(qs_c, ks_c, vs_c, sink),
                                 full_attention(qd_c, kd_c, vd_c, None)], axis=-1) @ w_out
    return y_ctx, y_lat


def hier_moe(h, w_group, w_expert, w_gate, w_up, w_down):
    b, L, Dm = h.shape
    t = h.reshape(b * L, Dm)
    p_group = jax.nn.softmax((t @ w_group).astype(F32), axis=-1)
    p_top, g_idx = lax.top_k(p_group, 1)
    e_logits = (t @ w_expert).astype(F32).reshape(-1, N_GROUPS, EXPERTS_PER_GROUP)
    e_sel = jnp.take_along_axis(e_logits, g_idx[:, :, None], axis=1)[:, 0]
    w_top, e_idx = lax.top_k(jax.nn.softmax(e_sel, axis=-1), TOP_K)
    w_top = w_top / jnp.sum(w_top, axis=-1, keepdims=True) * p_top
    gate_in_group = jnp.sum(jax.nn.one_hot(e_idx, EXPERTS_PER_GROUP, dtype=F32) * w_top[..., None], axis=1)
    gates = (jax.nn.one_hot(g_idx[:, 0], N_GROUPS, dtype=F32)[:, :, None] * gate_in_group[:, None, :]).astype(t.dtype)
    out = jnp.zeros_like(t)
    for g in range(N_GROUPS):
        a = jnp.einsum('nd,edf->nef', t, w_gate[g])
        u = jnp.einsum('nd,edf->nef', t, w_up[g])
        hid = jax.nn.silu(a) * u * gates[:, g, :, None]
        out = out + jnp.einsum('nef,efd->nd', hid, w_down[g])
    return out.reshape(b, L, Dm)


def setup_inputs(seed: int = 0) -> dict:
    key = jax.random.key(seed)
    ks = jax.random.split(key, 32)
    nrm = lambda k, shape, s: jax.random.normal(k, shape, F32) * s
    gain = lambda k, shape: 1.0 + 0.05 * jax.random.normal(k, shape, F32)
    dt0 = jnp.exp(jax.random.uniform(ks[11], (N_EVEN, 2, SSD_HEADS), F32, math.log(DT_MIN), math.log(DT_MAX)))
    return {
        'x': nrm(ks[0], (BATCH, SEQ, D_MODEL), 1.0),
        'c': nrm(ks[1], (BATCH, D_MODEL), 1.0),
        'ctx': nrm(ks[2], (BATCH, CTX_LEN, D_MODEL), 1.0),
        'c_ctx': nrm(ks[3], (D_MODEL,), 1.0),
        'ada_w': nrm(ks[4], (DEPTH, D_MODEL, 6 * D_MODEL), 0.5 * D_MODEL ** -0.5),
        'ada_b': nrm(ks[5], (DEPTH, 6 * D_MODEL), 0.02),
        'norm1_w': gain(ks[6], (DEPTH, D_MODEL)),
        'norm2_w': gain(ks[7], (DEPTH, D_MODEL)),
        'ev_w_in': nrm(ks[8], (N_EVEN, D_MODEL, EVEN_PROJ), D_MODEL ** -0.5),
        'ev_conv_w': nrm(ks[9], (N_EVEN, CONV_K, CONV_CH), CONV_K ** -0.5),
        'ev_conv_b': nrm(ks[10], (N_EVEN, CONV_CH), 0.02),
        'ev_dt_bias': dt0 + jnp.log(-jnp.expm1(-dt0)),
        'ev_a_log': jnp.log(jax.random.uniform(ks[12], (N_EVEN, 2, SSD_HEADS), F32, 1.0, 16.0)),
        'ev_d_skip': gain(ks[13], (N_EVEN, SSD_HEADS)),
        'ev_ssd_norm_w': gain(ks[14], (N_EVEN, SSD_INNER)),
        'ev_ig_bias': nrm(ks[15], (N_EVEN, 2, MLSTM_HEADS), 0.1),
        'ev_fg_bias': jax.random.uniform(ks[16], (N_EVEN, 2, MLSTM_HEADS), F32, 3.0, 6.0),
        'ev_mlstm_norm_w': gain(ks[17], (N_EVEN, MLSTM_INNER)),
        'ev_w_out': nrm(ks[18], (N_EVEN, EVEN_MIX, D_MODEL), EVEN_MIX ** -0.5),
        'od_w_in': nrm(ks[19], (N_ODD, D_MODEL, ODD_PROJ), D_MODEL ** -0.5),
        'od_sink': nrm(ks[20], (N_ODD, SWA_HEADS), 0.5),
        'od_q_norm_w': gain(ks[21], (N_ODD, ATT_HEAD_DIM)),
        'od_k_norm_w': gain(ks[22], (N_ODD, ATT_HEAD_DIM)),
        'od_w_out': nrm(ks[23], (N_ODD, ODD_MIX, D_MODEL), ODD_MIX ** -0.5),
        'moe_w_group': nrm(ks[24], (DEPTH, D_MODEL, N_GROUPS), D_MODEL ** -0.5),
        'moe_w_expert': nrm(ks[25], (DEPTH, D_MODEL, N_GROUPS * EXPERTS_PER_GROUP), D_MODEL ** -0.5),
        'moe_w_gate': nrm(ks[26], (DEPTH, N_GROUPS, EXPERTS_PER_GROUP, D_MODEL, EXPERT_FF), D_MODEL ** -0.5),
        'moe_w_up': nrm(ks[27], (DEPTH, N_GROUPS, EXPERTS_PER_GROUP, D_MODEL, EXPERT_FF), D_MODEL ** -0.5),
        'moe_w_down': nrm(ks[28], (DEPTH, N_GROUPS, EXPERTS_PER_GROUP, EXPERT_FF, D_MODEL), EXPERT_FF ** -0.5),
        'final_norm_w': gain(ks[29], (D_MODEL,)),
    }


def reference(x, c, ctx, c_ctx, ada_w, ada_b, norm1_w, norm2_w,
              ev_w_in, ev_conv_w, ev_conv_b, ev_dt_bias, ev_a_log, ev_d_skip, ev_ssd_norm_w,
              ev_ig_bias, ev_fg_bias, ev_mlstm_norm_w, ev_w_out,
              od_w_in, od_sink, od_q_norm_w, od_k_norm_w, od_w_out,
              moe_w_group, moe_w_expert, moe_w_gate, moe_w_up, moe_w_down, final_norm_w):
    S = x.shape[1]
    cos, sin = rope_tables(S)
    silu_c = jax.nn.silu(c)
    silu_cc = jax.nn.silu(c_ctx)[None]
    h_lat, h_ctx = x, ctx
    for layer in range(DEPTH):
        li = layer // 2
        need_ctx = layer < DEPTH - 1
        sh1, sc1, g1, sh2, sc2, g2 = adaln(silu_c, ada_w[layer], ada_b[layer])
        csh1, csc1, cg1, csh2, csc2, cg2 = adaln(silu_cc, ada_w[layer], ada_b[layer])
        a_lat = modulate(rmsnorm(h_lat, norm1_w[layer]), sh1, sc1)
        a_ctx = modulate(rmsnorm(h_ctx, norm1_w[layer]), csh1, csc1)
        if layer % 2 == 0:
            y_ctx, y_lat = recurrent_mixer(a_ctx, a_lat, ev_w_in[li], ev_conv_w[li], ev_conv_b[li], ev_dt_bias[li],
                                           ev_a_log[li], ev_d_skip[li], ev_ssd_norm_w[li], ev_ig_bias[li],
                                           ev_fg_bias[li], ev_mlstm_norm_w[li], ev_w_out[li])
        else:
            y_ctx, y_lat = attention_mixer(a_ctx, a_lat, od_w_in[li], od_sink[li], od_q_norm_w[li],
                                           od_k_norm_w[li], od_w_out[li], cos, sin, need_ctx)
        h_lat = h_lat + g1 * y_lat
        f_lat = hier_moe(modulate(rmsnorm(h_lat, norm2_w[layer]), sh2, sc2), moe_w_group[layer],
                         moe_w_expert[layer], moe_w_gate[layer], moe_w_up[layer], moe_w_down[layer])
        h_lat = h_lat + g2 * f_lat
        if need_ctx:
            h_ctx = h_ctx + cg1 * y_ctx
            f_ctx = hier_moe(modulate(rmsnorm(h_ctx, norm2_w[layer]), csh2, csc2), moe_w_group[layer],
                             moe_w_expert[layer], moe_w_gate[layer], moe_w_up[layer], moe_w_down[layer])
            h_ctx = h_ctx + cg2 * f_ctx
    return rmsnorm(h_lat, final_norm_w)
```

```python
import functools
import math

import jax
import jax.numpy as jnp
from jax import lax
from jax.experimental import pallas as pl
from jax.experimental.pallas import tpu as pltpu

F32 = jnp.float32
BF16 = jnp.bfloat16
EPS = 1e-6
NEG_INF = float("-inf")

D_MODEL = 1024
GRID_W = 64
ROPE_THETA = 10000.0

SSD_HEADS = 16
SSD_HEAD_DIM = 64
SSD_INNER = 1024
SSD_GROUPS = 2
SSD_STATE = 128
MLSTM_HEADS = 4
MLSTM_QK_DIM = 128
MLSTM_V_DIM = 256
MLSTM_QK = 512
MLSTM_INNER = 1024
CHUNK = 128
CONV_CH = 2560
REC_W = MLSTM_INNER + CONV_CH
N_GATE = 48
GATE_W = 128
COL_DT, COL_IG, COL_FG = 0, 32, 40

ATT_DH = 64
ATT_HEADS = 8
ATT_KV = 2
ATT_G = ATT_HEADS // ATT_KV
WINDOW = 128
QBLK = 128
ODD_PROJ = 1536

N_GROUPS = 4
EXPERTS_PER_GROUP = 4
N_EXPERTS = N_GROUPS * EXPERTS_PER_GROUP
EXPERT_FF = 256
ROUTE_W = 128
EXPERT_LANE0 = N_GROUPS

ROW_TILE = 256
VMEM_LIMIT = 56 * 1024 * 1024


def _cparams(*sem):
    return pltpu.CompilerParams(dimension_semantics=sem, vmem_limit_bytes=VMEM_LIMIT)


def _sigmoid(x):
    return 1.0 / (1.0 + jnp.exp(-x))


def _silu(x):
    return x * _sigmoid(x)


def _softplus(x):
    return jnp.maximum(x, 0.0) + jnp.log1p(jnp.exp(-jnp.abs(x)))


def _dot(a, b):
    return jnp.dot(a, b, preferred_element_type=F32)


def _dot_nt(a, b):
    return lax.dot_general(a, b, (((1,), (1,)), ((), ())), preferred_element_type=F32)


def _split3(a):
    hi = a.astype(BF16)
    r1 = a - hi.astype(F32)
    mid = r1.astype(BF16)
    lo = (r1 - mid.astype(F32)).astype(BF16)
    return hi, mid, lo


def _dot3_rhs(exact, a):
    hi, mid, lo = _split3(a)
    return _dot(exact, hi) + _dot(exact, mid) + _dot(exact, lo)


def _dot3_lhs(a, exact):
    hi, mid, lo = _split3(a)
    return _dot(hi, exact) + _dot(mid, exact) + _dot(lo, exact)


def _adaln_kernel(cond_ref, w_ref, b_ref, o_ref):
    s = _silu(cond_ref[...])
    o_ref[0] = _dot(s.astype(BF16), w_ref[0].astype(BF16)) + b_ref[0]


def _adaln(cond, ada_w, ada_b):
    depth, d, n = ada_w.shape
    rows = cond.shape[0]
    tn = 1536
    return pl.pallas_call(
        _adaln_kernel,
        grid=(depth, n // tn),
        in_specs=[pl.BlockSpec((rows, d), lambda l, j: (0, 0)),
                  pl.BlockSpec((1, d, tn), lambda l, j: (l, 0, j)),
                  pl.BlockSpec((1, 1, tn), lambda l, j: (l, 0, j))],
        out_specs=pl.BlockSpec((1, rows, tn), lambda l, j: (l, 0, j)),
        out_shape=jax.ShapeDtypeStruct((depth, rows, n), F32),
        compiler_params=_cparams("parallel", "parallel"),
        name="adaln",
    )(cond, ada_w, ada_b.reshape(depth, 1, n))


def _norm_mod(x, nw, mod, shift_row, scale_row):
    var = jnp.mean(x * x, axis=-1, keepdims=True)
    y = x * lax.rsqrt(var + EPS) * nw
    return y * (1.0 + mod[scale_row:scale_row + 1]) + mod[shift_row:shift_row + 1]


def _nmm_kernel(h_ref, mod_ref, nw_ref, *rest, n_out, shift_row, scale_row, tn):
    w_refs, o_refs = rest[:n_out], rest[n_out:]
    a16 = _norm_mod(h_ref[0], nw_ref[...], mod_ref[0, 0], shift_row, scale_row).astype(BF16)
    for w_ref, o_ref in zip(w_refs, o_refs):
        n = w_ref.shape[1]
        for j in range(0, n, tn):
            w = min(tn, n - j)
            o_ref[0, :, j:j + w] = _dot(a16, w_ref[:, j:j + w]).astype(o_ref.dtype)


def _mod_spec(nct_tiles):
    return pl.BlockSpec((1, 1, 6, D_MODEL), lambda b, i: (b, (i >= nct_tiles).astype(jnp.int32), 0, 0))


def _norm_mod_matmul(h, mod, nw, weights, out_dtypes, shift_row, scale_row, ctx_len):
    B, L, d = h.shape
    tm = ROW_TILE
    in_specs = [pl.BlockSpec((1, tm, d), lambda b, i: (b, i, 0)),
                _mod_spec(ctx_len // tm),
                pl.BlockSpec((1, d), lambda b, i: (0, 0))]
    out_specs, out_shape = [], []
    for w, dt in zip(weights, out_dtypes):
        n = w.shape[1]
        in_specs.append(pl.BlockSpec((d, n), lambda b, i: (0, 0)))
        out_specs.append(pl.BlockSpec((1, tm, n), lambda b, i: (b, i, 0)))
        out_shape.append(jax.ShapeDtypeStruct((B, L, n), dt))
    return pl.pallas_call(
        functools.partial(_nmm_kernel, n_out=len(weights), shift_row=shift_row, scale_row=scale_row, tn=512),
        grid=(B, L // tm),
        in_specs=in_specs, out_specs=out_specs, out_shape=out_shape,
        compiler_params=_cparams("parallel", "parallel"),
        name="norm_mod_matmul",
    )(h, mod, nw.reshape(1, d), *weights)


def _mixer_kernel(rec_ref, prev_ref, next_ref, g_ref, gt_ref, cw_ref, cb_ref, prow_ref, pcol_ref, dsk_ref,
                  y_ref, conv_ref, S_ref, C_ref, n_ref, m_ref, *, reverse, nct, nc):
    i = pl.program_id(1)
    if reverse:
        c = jnp.where(i < nct, nct - 1 - i, nc - 1 - (i - nct))
    else:
        c = i
    d = 1 if reverse else 0

    @pl.when(i == 0)
    def _():
        S_ref[...] = jnp.zeros_like(S_ref)
        C_ref[...] = jnp.zeros_like(C_ref)
        n_ref[...] = jnp.zeros_like(n_ref)
        m_ref[...] = jnp.zeros_like(m_ref)

    seq_start = jnp.logical_or(c == 0, c == nct)
    seq_end = jnp.logical_or(c == nct - 1, c == nc - 1)

    xc = rec_ref[0, :, MLSTM_INNER:].astype(F32)
    before = prev_ref[0].astype(F32)[CHUNK // 8 - 1:CHUNK // 8, MLSTM_INNER:]
    after = next_ref[0].astype(F32)[0:1, MLSTM_INNER:]
    before = jnp.where(seq_start, 0.0, before)
    after = jnp.where(seq_end, 0.0, after)
    rid = lax.broadcasted_iota(jnp.int32, (CHUNK, 1), 0)
    x_prev = jnp.where(rid == 0, before, pltpu.roll(xc, 1, 0))
    x_next = jnp.where(rid == CHUNK - 1, after, pltpu.roll(xc, CHUNK - 1, 0))
    cw = cw_ref[...]
    conv_ref[...] = _silu(cw[0:1] * x_prev + cw[1:2] * xc + cw[2:3] * x_next + cb_ref[...])

    li = lax.broadcasted_iota(jnp.int32, (CHUNK, CHUNK), 0)
    si = lax.broadcasted_iota(jnp.int32, (CHUNK, CHUNK), 1)
    mask = (si >= li) if reverse else (si <= li)
    tri = mask.astype(BF16)
    tri_t = ((li >= si) if reverse else (li <= si)).astype(BF16)
    last = 0 if reverse else CHUNK - 1

    G = g_ref[0] + prow_ref[0:1]
    GT = gt_ref[0] + pcol_ref[:, 0:1]
    lane = lax.broadcasted_iota(jnp.int32, (CHUNK, GATE_W), 1)
    row = lax.broadcasted_iota(jnp.int32, (N_GATE, CHUNK), 0)
    dt_c = _softplus(G)
    dt_r = _softplus(GT)
    nega_c = -jnp.exp(prow_ref[1:2])
    nega_r = -jnp.exp(pcol_ref[:, 1:2])
    pre_c = jnp.where(lane < COL_IG, dt_c * nega_c, jnp.where(lane >= COL_FG, -_softplus(-G), 0.0))
    pre_c = jnp.where(lane < N_GATE, pre_c, 0.0)
    pre_r = jnp.where(row < COL_IG, dt_r * nega_r, jnp.where(row >= COL_FG, -_softplus(-GT), 0.0))
    cum_c = _dot3_rhs(tri, pre_c)
    cum_r = _dot3_lhs(pre_r, tri_t)

    a_c = cum_c[:, d * SSD_HEADS:(d + 1) * SSD_HEADS]
    a_r = cum_r[d * SSD_HEADS:(d + 1) * SSD_HEADS, :]
    a_last = a_c[last:last + 1, :]
    hh = lax.broadcasted_iota(jnp.int32, (SSD_HEADS, SSD_INNER), 0)
    hj = lax.broadcasted_iota(jnp.int32, (SSD_HEADS, SSD_INNER), 1)
    expand = (hj // SSD_HEAD_DIM == hh).astype(BF16)
    dt_x = _dot3_lhs(dt_c[:, d * SSD_HEADS:(d + 1) * SSD_HEADS], expand)
    dec_x = _dot3_lhs(jnp.exp(a_last - a_c), expand)
    ein_x = _dot3_lhs(jnp.exp(a_c), expand)
    cdec_x = _dot3_lhs(jnp.broadcast_to(jnp.exp(a_last), (8, SSD_HEADS)), expand)[0:1]

    xs = conv_ref[:, 0:SSD_INNER]
    xsdt = xs * dt_x
    xsdt16 = xsdt.astype(BF16)
    xdec16 = (xsdt * dec_x).astype(BF16)
    lane128 = lax.broadcasted_iota(jnp.int32, (CHUNK, 128), 1)
    heads_per_group = SSD_HEADS // SSD_GROUPS
    gw = heads_per_group * SSD_HEAD_DIM
    for g in range(SSD_GROUPS):
        Bg = conv_ref[:, SSD_INNER + g * SSD_STATE:SSD_INNER + (g + 1) * SSD_STATE]
        Cg16 = conv_ref[:, SSD_INNER + 256 + g * SSD_STATE:SSD_INNER + 256 + (g + 1) * SSD_STATE].astype(BF16)
        CB = _dot_nt(Cg16, Bg.astype(BF16))
        S_g = S_ref[:, g * gw:(g + 1) * gw]
        y_off = _dot(Cg16, S_g.astype(BF16)) * ein_x[:, g * gw:(g + 1) * gw]
        for pr in range(heads_per_group // 2):
            col0 = g * gw + pr * 128
            xpair = xsdt16[:, col0:col0 + 128]
            acc = None
            for half in range(2):
                h = g * heads_per_group + pr * 2 + half
                seg = a_c[:, h:h + 1] - a_r[h:h + 1, :]
                Lm = jnp.exp(jnp.where(mask, seg, NEG_INF))
                Mh = (CB * Lm).astype(BF16)
                keep = (lane128 < 64) if half == 0 else (lane128 >= 64)
                part = _dot(Mh, jnp.where(keep, xpair, jnp.zeros_like(xpair)))
                acc = part if acc is None else acc + part
            yv = acc + y_off[:, pr * 128:(pr + 1) * 128]
            if not reverse:
                yv = yv + xs[:, col0:col0 + 128] * dsk_ref[:, col0:col0 + 128]
            y_ref[0, :, col0:col0 + 128] = yv
        S_ref[:, g * gw:(g + 1) * gw] = (cdec_x[:, g * gw:(g + 1) * gw] * S_g
                                         + _dot(Bg.T.astype(BF16), xdec16[:, g * gw:(g + 1) * gw]))

    for h in range(MLSTM_HEADS):
        gi = COL_IG + d * MLSTM_HEADS + h
        gf = COL_FG + d * MLSTM_HEADS + h
        qh16 = conv_ref[:, 1536 + h * 128:1536 + (h + 1) * 128].astype(BF16)
        kh = conv_ref[:, 2048 + h * 128:2048 + (h + 1) * 128] * (MLSTM_QK_DIM ** -0.5)
        kh16 = kh.astype(BF16)
        vh16 = rec_ref[0, :, h * MLSTM_V_DIM:(h + 1) * MLSTM_V_DIM]
        ig_c, ig_r = G[:, gi:gi + 1], GT[gi:gi + 1, :]
        b_c, b_r = cum_c[:, gf:gf + 1], cum_r[gf:gf + 1, :]
        b_last = b_c[last:last + 1, :]
        Dlog = jnp.where(mask, b_c - b_r + ig_r, NEG_INF)
        m_loc = jnp.max(b_last - b_r + ig_r, axis=-1, keepdims=True)
        ek = jnp.exp(b_last - b_c + ig_c - m_loc) * kh
        C_loc = _dot(ek.T.astype(BF16), vh16)
        n_loc = jnp.sum(ek, axis=0, keepdims=True)
        C_in = C_ref[h]
        n_in = n_ref[h:h + 1, :]
        m_in = m_ref[h:h + 1, 0:1]
        m_inter = b_c + m_in
        m_t = jnp.maximum(m_inter, jnp.max(Dlog, axis=-1, keepdims=True))
        P = jnp.exp(Dlog - m_t) * _dot_nt(qh16, kh16)
        w_inter = jnp.exp(m_inter - m_t)
        num = _dot(P.astype(BF16), vh16) + w_inter * _dot(qh16, C_in.astype(BF16))
        qn = jnp.sum(qh16.astype(F32) * n_in, axis=-1, keepdims=True)
        den = jnp.sum(P, axis=-1, keepdims=True) + w_inter * qn
        den = jnp.maximum(jnp.abs(den), jnp.exp(-m_t))
        y_ref[0, :, SSD_INNER + h * MLSTM_V_DIM:SSD_INNER + (h + 1) * MLSTM_V_DIM] = num / den
        m_new = jnp.maximum(b_last + m_in, m_loc)
        fa = jnp.exp(b_last + m_in - m_new)
        fb = jnp.exp(m_loc - m_new)
        C_ref[h] = fa * C_in + fb * C_loc
        n_ref[h:h + 1, :] = fa * n_in + fb * n_loc
        m_ref[h:h + 1, :] = jnp.broadcast_to(m_new, (1, 128))


def _mixer_scan(rec, gates, gates_t, conv_w, conv_b, prow, pcol, dskip, ctx_len, reverse):
    B, L, _ = rec.shape
    nc, nct = L // CHUNK, ctx_len // CHUNK
    halo = 16
    hb = CHUNK // halo

    def chunk_of(i):
        if reverse:
            return jnp.where(i < nct, nct - 1 - i, nc - 1 - (i - nct))
        return i

    return pl.pallas_call(
        functools.partial(_mixer_kernel, reverse=reverse, nct=nct, nc=nc),
        grid=(B, nc),
        in_specs=[pl.BlockSpec((1, CHUNK, REC_W), lambda b, i: (b, chunk_of(i), 0)),
                  pl.BlockSpec((1, halo, REC_W), lambda b, i: (b, jnp.maximum(chunk_of(i) * hb - 1, 0), 0)),
                  pl.BlockSpec((1, halo, REC_W),
                               lambda b, i: (b, jnp.minimum((chunk_of(i) + 1) * hb, L // halo - 1), 0)),
                  pl.BlockSpec((1, CHUNK, GATE_W), lambda b, i: (b, chunk_of(i), 0)),
                  pl.BlockSpec((1, N_GATE, CHUNK), lambda b, i: (b, 0, chunk_of(i))),
                  pl.BlockSpec((3, CONV_CH), lambda b, i: (0, 0)),
                  pl.BlockSpec((1, CONV_CH), lambda b, i: (0, 0)),
                  pl.BlockSpec((2, GATE_W), lambda b, i: (0, 0)),
                  pl.BlockSpec((N_GATE, 2), lambda b, i: (0, 0)),
                  pl.BlockSpec((1, SSD_INNER), lambda b, i: (0, 0))],
        out_specs=pl.BlockSpec((1, CHUNK, SSD_INNER + MLSTM_INNER), lambda b, i: (b, chunk_of(i), 0)),
        out_shape=jax.ShapeDtypeStruct((B, L, SSD_INNER + MLSTM_INNER), F32),
        scratch_shapes=[pltpu.VMEM((CHUNK, CONV_CH), F32),
                        pltpu.VMEM((SSD_STATE, SSD_INNER), F32),
                        pltpu.VMEM((MLSTM_HEADS, MLSTM_QK_DIM, MLSTM_V_DIM), F32),
                        pltpu.VMEM((8, MLSTM_QK_DIM), F32),
                        pltpu.VMEM((8, 128), F32)],
        compiler_params=_cparams("parallel", "arbitrary"),
        name="mixer_bwd" if reverse else "mixer_fwd",
    )(rec, rec, rec, gates, gates_t, conv_w, conv_b, prow, pcol, dskip)


def _group_rms(x, groups):
    w = x.shape[-1] // groups
    parts = []
    for g in range(groups):
        seg = x[:, g * w:(g + 1) * w]
        parts.append(seg * lax.rsqrt(jnp.mean(seg * seg, axis=-1, keepdims=True) + EPS))
    return jnp.concatenate(parts, axis=-1)


def _even_out_kernel(yf_ref, yb_ref, zo_ref, h_ref, mod_ref, snw_ref, mnw_ref, w_ref, o_ref):
    y = yf_ref[0] + yb_ref[0]
    z = zo_ref[0, :, 0:SSD_INNER].astype(F32)
    o = zo_ref[0, :, SSD_INNER:].astype(F32)
    ys = _group_rms(y[:, 0:SSD_INNER] * _silu(z), SSD_GROUPS) * snw_ref[...]
    hm = _group_rms(y[:, SSD_INNER:], MLSTM_HEADS) * mnw_ref[...] * _sigmoid(o)
    mix = jnp.concatenate([ys, hm], axis=-1).astype(BF16)
    o_ref[0] = h_ref[0] + mod_ref[0, 0][2:3] * _dot(mix, w_ref[...])


def _even_out(yf, yb, zo, h, mod, snw, mnw, w_out, ctx_len):
    B, L, d = h.shape
    tm = ROW_TILE
    wide = SSD_INNER + MLSTM_INNER
    row = lambda n: pl.BlockSpec((1, tm, n), lambda b, i: (b, i, 0))
    const = lambda r, n: pl.BlockSpec((r, n), lambda b, i: (0, 0))
    return pl.pallas_call(
        _even_out_kernel,
        grid=(B, L // tm),
        in_specs=[row(wide), row(wide), row(wide), row(d), _mod_spec(ctx_len // tm),
                  const(1, SSD_INNER), const(1, MLSTM_INNER), const(wide, d)],
        out_specs=row(d),
        out_shape=jax.ShapeDtypeStruct((B, L, d), F32),
        compiler_params=_cparams("parallel", "parallel"),
        name="even_out",
    )(yf, yb, zo, h, mod, snw.reshape(1, -1), mnw.reshape(1, -1), w_out)


def _qkv_prep_kernel(p_ref, cs_ref, sn_ref, qnw_ref, knw_ref,
                     qs_ref, ks_ref, vsl_ref, vsr_ref, qd_ref, kd_ref, vdl_ref, vdr_ref, *, nct):
    is_lat = pl.program_id(1) >= nct
    p = p_ref[0].astype(F32)
    cs, sn = cs_ref[...], sn_ref[...]

    def rope(x):
        w = x.shape[1]
        reps = w // 128
        c = jnp.concatenate([cs] * reps, axis=1) if reps > 1 else cs
        s = jnp.concatenate([sn] * reps, axis=1) if reps > 1 else sn
        lane = lax.broadcasted_iota(jnp.int32, x.shape, 1)
        partner = jnp.where(lane % 2 == 0, pltpu.roll(x, w - 1, 1), pltpu.roll(x, 1, 1))
        return jnp.where(is_lat, x * c + partner * s, x)

    def head_rms(x, nw):
        w = x.shape[1]
        a = lax.broadcasted_iota(jnp.int32, (w, w), 0) // ATT_DH
        b = lax.broadcasted_iota(jnp.int32, (w, w), 1) // ATT_DH
        same = (a == b).astype(BF16)
        ms = _dot3_lhs(x * x, same) * (1.0 / ATT_DH)
        return x * lax.rsqrt(ms + EPS) * nw

    qw, kw = ATT_HEADS * ATT_DH, ATT_KV * ATT_DH
    o = 0
    q_s = rope(p[:, o:o + qw]) * (ATT_DH ** -0.5); o += qw
    k_s = rope(p[:, o:o + kw]); o += kw
    v_s = p[:, o:o + kw]; o += kw
    q_d = rope(head_rms(p[:, o:o + qw], qnw_ref[...])) * (ATT_DH ** -0.5); o += qw
    k_d = rope(head_rms(p[:, o:o + kw], knw_ref[...])); o += kw
    v_d = p[:, o:o + kw]

    lane = lax.broadcasted_iota(jnp.int32, (QBLK, 128), 1)
    for h in range(ATT_HEADS):
        qs_ref[0, h] = q_s[:, h * ATT_DH:(h + 1) * ATT_DH].astype(BF16)
        qd_ref[0, h] = q_d[:, h * ATT_DH:(h + 1) * ATT_DH].astype(BF16)
    for kv in range(ATT_KV):
        ks_ref[0, kv] = k_s[:, kv * ATT_DH:(kv + 1) * ATT_DH].astype(BF16)
        kd_ref[0, kv] = k_d[:, kv * ATT_DH:(kv + 1) * ATT_DH].astype(BF16)
    for v, l_ref, r_ref in ((v_s, vsl_ref, vsr_ref), (v_d, vdl_ref, vdr_ref)):
        for kv in range(ATT_KV):
            own = (lane // ATT_DH) == kv
            same = jnp.where(own, v, 0.0)
            other = pltpu.roll(same, ATT_DH, 1)
            left, right = (same, other) if kv == 0 else (other, same)
            l_ref[0, kv] = left.astype(BF16)
            r_ref[0, kv] = right.astype(BF16)


def _qkv_prep(proj, cosf, sinf, qnw, knw, ctx_len):
    B, L, _ = proj.shape
    nct = ctx_len // QBLK
    tab = pl.BlockSpec((QBLK, 128), lambda b, i: (jnp.maximum(i - nct, 0), 0))
    head = lambda n, w: pl.BlockSpec((1, n, QBLK, w), lambda b, i: (b, 0, i, 0))
    shp = lambda n, w: jax.ShapeDtypeStruct((B, n, L, w), BF16)
    return pl.pallas_call(
        functools.partial(_qkv_prep_kernel, nct=nct),
        grid=(B, L // QBLK),
        in_specs=[pl.BlockSpec((1, QBLK, ODD_PROJ), lambda b, i: (b, i, 0)), tab, tab,
                  pl.BlockSpec((1, ATT_HEADS * ATT_DH), lambda b, i: (0, 0)),
                  pl.BlockSpec((1, ATT_KV * ATT_DH), lambda b, i: (0, 0))],
        out_specs=[head(ATT_HEADS, ATT_DH), head(ATT_KV, ATT_DH), head(ATT_KV, 128), head(ATT_KV, 128),
                   head(ATT_HEADS, ATT_DH), head(ATT_KV, ATT_DH), head(ATT_KV, 128), head(ATT_KV, 128)],
        out_shape=[shp(ATT_HEADS, ATT_DH), shp(ATT_KV, ATT_DH), shp(ATT_KV, 128), shp(ATT_KV, 128),
                   shp(ATT_HEADS, ATT_DH), shp(ATT_KV, ATT_DH), shp(ATT_KV, 128), shp(ATT_KV, 128)],
        compiler_params=_cparams("parallel", "parallel"),
        name="qkv_prep",
    )(proj, cosf, sinf, qnw, knw)


def _pair_out(P16, vl, vr, inv_l, rows):
    outs = []
    for pr in range(ATT_G // 2):
        a, b = 2 * pr, 2 * pr + 1
        oa = _dot(P16[a * rows:(a + 1) * rows], vl) * inv_l[a * rows:(a + 1) * rows]
        ob = _dot(P16[b * rows:(b + 1) * rows], vr) * inv_l[b * rows:(b + 1) * rows]
        outs.append(oa + ob)
    return jnp.concatenate(outs, axis=-1)


def _dense_attn_kernel(q_ref, k_ref, vl_ref, vr_ref, o_ref):
    for kv in range(ATT_KV):
        Q = q_ref[0, kv * ATT_G:(kv + 1) * ATT_G].reshape(ATT_G * QBLK, ATT_DH)
        S = _dot_nt(Q, k_ref[0, kv])
        m = jnp.max(S, axis=-1, keepdims=True)
        P = jnp.exp(S - m)
        inv_l = 1.0 / jnp.sum(P, axis=-1, keepdims=True)
        o_ref[0, :, kv * ATT_G * ATT_DH:(kv + 1) * ATT_G * ATT_DH] = _pair_out(
            P.astype(BF16), vl_ref[0, kv], vr_ref[0, kv], inv_l, QBLK).astype(o_ref.dtype)


def _dense_attn(qd, kd, vdl, vdr, ctx_len):
    B, _, L, _ = qd.shape
    S = L - ctx_len
    nct = ctx_len // QBLK
    full = lambda w: pl.BlockSpec((1, ATT_KV, L, w), lambda b, j: (b, 0, 0, 0))
    return pl.pallas_call(
        _dense_attn_kernel,
        grid=(B, S // QBLK),
        in_specs=[pl.BlockSpec((1, ATT_HEADS, QBLK, ATT_DH), lambda b, j: (b, 0, j + nct, 0)),
                  full(ATT_DH), full(128), full(128)],
        out_specs=pl.BlockSpec((1, QBLK, ATT_HEADS * ATT_DH), lambda b, j: (b, j, 0)),
        out_shape=jax.ShapeDtypeStruct((B, S, ATT_HEADS * ATT_DH), BF16),
        compiler_params=_cparams("parallel", "parallel"),
        name="dense_attn",
    )(qd, kd, vdl, vdr)


def _sink_rows(sink_ref, kv, rows):
    return jnp.concatenate(
        [jnp.broadcast_to(sink_ref[kv * ATT_G + g:kv * ATT_G + g + 1, 0:1], (rows, 1)) for g in range(ATT_G)], axis=0)


def _window_attn_kernel(q_ref, k_ref, vl_ref, vr_ref, sink_ref, o_ref, *, ctx_len, seq):
    j = pl.program_id(1)
    span = 3 * QBLK
    start = jnp.clip((j - 1) * QBLK, 0, seq - span)
    kstart = pl.multiple_of(ctx_len + start, QBLK)
    rows = ATT_G * QBLK
    qpos = j * QBLK + lax.broadcasted_iota(jnp.int32, (rows, span), 0) % QBLK
    kpos = start + lax.broadcasted_iota(jnp.int32, (rows, span), 1)
    valid = jnp.abs(kpos - qpos) <= WINDOW
    for kv in range(ATT_KV):
        Q = q_ref[0, kv * ATT_G:(kv + 1) * ATT_G].reshape(rows, ATT_DH)
        Sc = _dot_nt(Q, k_ref[0, kv, 0:ctx_len])
        Sl = jnp.where(valid, _dot_nt(Q, k_ref[0, kv, pl.ds(kstart, span)]), NEG_INF)
        sk = _sink_rows(sink_ref, kv, QBLK)
        m = jnp.maximum(jnp.maximum(jnp.max(Sc, axis=-1, keepdims=True), jnp.max(Sl, axis=-1, keepdims=True)), sk)
        Pc = jnp.exp(Sc - m)
        Pl = jnp.exp(Sl - m)
        inv_l = 1.0 / (jnp.exp(sk - m) + jnp.sum(Pc, axis=-1, keepdims=True) + jnp.sum(Pl, axis=-1, keepdims=True))
        out = (_pair_out(Pc.astype(BF16), vl_ref[0, kv, 0:ctx_len], vr_ref[0, kv, 0:ctx_len], inv_l, QBLK)
               + _pair_out(Pl.astype(BF16), vl_ref[0, kv, pl.ds(kstart, span)], vr_ref[0, kv, pl.ds(kstart, span)],
                           inv_l, QBLK))
        o_ref[0, :, kv * ATT_G * ATT_DH:(kv + 1) * ATT_G * ATT_DH] = out.astype(o_ref.dtype)


def _window_attn(qs, ks, vsl, vsr, sink_b, ctx_len):
    B, _, L, _ = qs.shape
    S = L - ctx_len
    nct = ctx_len // QBLK
    full = lambda w: pl.BlockSpec((1, ATT_KV, L, w), lambda b, j: (b, 0, 0, 0))
    return pl.pallas_call(
        functools.partial(_window_attn_kernel, ctx_len=ctx_len, seq=S),
        grid=(B, S // QBLK),
        in_specs=[pl.BlockSpec((1, ATT_HEADS, QBLK, ATT_DH), lambda b, j: (b, 0, j + nct, 0)),
                  full(ATT_DH), full(128), full(128),
                  pl.BlockSpec((ATT_HEADS, 128), lambda b, j: (0, 0))],
        out_specs=pl.BlockSpec((1, QBLK, ATT_HEADS * ATT_DH), lambda b, j: (b, j, 0)),
        out_shape=jax.ShapeDtypeStruct((B, S, ATT_HEADS * ATT_DH), BF16),
        compiler_params=_cparams("parallel", "parallel"),
        name="window_attn",
    )(qs, ks, vsl, vsr, sink_b)


def _ctx_attn_kernel(qs_ref, ks_ref, vsl_ref, vsr_ref, qd_ref, kd_ref, vdl_ref, vdr_ref, sink_ref,
                     os_ref, od_ref, *, ctx_len):
    T = ctx_len
    for q_ref, k_ref, vl_ref, vr_ref, o_ref, with_sink in ((qs_ref, ks_ref, vsl_ref, vsr_ref, os_ref, True),
                                                           (qd_ref, kd_ref, vdl_ref, vdr_ref, od_ref, False)):
        for kv in range(ATT_KV):
            Q = q_ref[0, kv * ATT_G:(kv + 1) * ATT_G].reshape(ATT_G * T, ATT_DH)
            S = _dot_nt(Q, k_ref[0, kv])
            m = jnp.max(S, axis=-1, keepdims=True)
            if with_sink:
                sk = _sink_rows(sink_ref, kv, T)
                m = jnp.maximum(m, sk)
            P = jnp.exp(S - m)
            l = jnp.sum(P, axis=-1, keepdims=True)
            if with_sink:
                l = l + jnp.exp(sk - m)
            o_ref[0, :, kv * ATT_G * ATT_DH:(kv + 1) * ATT_G * ATT_DH] = _pair_out(
                P.astype(BF16), vl_ref[0, kv], vr_ref[0, kv], 1.0 / l, T).astype(o_ref.dtype)


def _ctx_attn(qs, ks, vsl, vsr, qd, kd, vdl, vdr, sink_b, ctx_len):
    B = qs.shape[0]
    T = ctx_len
    blk = lambda n, w: pl.BlockSpec((1, n, T, w), lambda b: (b, 0, 0, 0))
    out = pl.BlockSpec((1, T, ATT_HEADS * ATT_DH), lambda b: (b, 0, 0))
    return pl.pallas_call(
        functools.partial(_ctx_attn_kernel, ctx_len=T),
        grid=(B,),
        in_specs=[blk(ATT_HEADS, ATT_DH), blk(ATT_KV, ATT_DH), blk(ATT_KV, 128), blk(ATT_KV, 128),
                  blk(ATT_HEADS, ATT_DH), blk(ATT_KV, ATT_DH), blk(ATT_KV, 128), blk(ATT_KV, 128),
                  pl.BlockSpec((ATT_HEADS, 128), lambda b: (0, 0))],
        out_specs=[out, out],
        out_shape=[jax.ShapeDtypeStruct((B, T, ATT_HEADS * ATT_DH), BF16)] * 2,
        compiler_params=_cparams("parallel"),
        name="ctx_attn",
    )(qs, ks, vsl, vsr, qd, kd, vdl, vdr, sink_b)


def _odd_out_kernel(ys_ref, yd_ref, h_ref, mod_ref, w_ref, o_ref):
    mix = jnp.concatenate([ys_ref[0], yd_ref[0]], axis=-1)
    o_ref[0] = h_ref[0] + mod_ref[0, 0][2:3] * _dot(mix, w_ref[...])


def _odd_out(ys, yd, h, mod, w_out, ctx_len):
    B, L, d = h.shape
    tm = ROW_TILE
    half = ATT_HEADS * ATT_DH
    row = lambda n: pl.BlockSpec((1, tm, n), lambda b, i: (b, i, 0))
    return pl.pallas_call(
        _odd_out_kernel,
        grid=(B, L // tm),
        in_specs=[row(half), row(half), row(d), _mod_spec(ctx_len // tm),
                  pl.BlockSpec((2 * half, d), lambda b, i: (0, 0))],
        out_specs=row(d),
        out_shape=jax.ShapeDtypeStruct((B, L, d), F32),
        compiler_params=_cparams("parallel", "parallel"),
        name="odd_out",
    )(ys, yd, h, mod, w_out)


def _first_lane_of(cond, lane):
    return jnp.min(jnp.where(cond, lane, ROUTE_W), axis=-1, keepdims=True)


def _router_kernel(h_ref, mod_ref, nw_ref, wr_ref, a_ref, g_ref):
    a = _norm_mod(h_ref[0], nw_ref[...], mod_ref[0, 0], 3, 4)
    a_ref[0] = a.astype(BF16)
    logits = jnp.dot(a, wr_ref[...], preferred_element_type=F32, precision=lax.Precision.HIGHEST)
    lane = lax.broadcasted_iota(jnp.int32, logits.shape, 1)
    gl = jnp.where(lane < N_GROUPS, logits, NEG_INF)
    gmax = jnp.max(gl, axis=-1, keepdims=True)
    p_top = 1.0 / jnp.sum(jnp.exp(gl - gmax), axis=-1, keepdims=True)
    g_idx = _first_lane_of(gl == gmax, lane)
    in_group = jnp.logical_and(lane >= EXPERT_LANE0, (lane - EXPERT_LANE0) // EXPERTS_PER_GROUP == g_idx)
    in_group = jnp.logical_and(in_group, lane < EXPERT_LANE0 + N_EXPERTS)
    el = jnp.where(in_group, logits, NEG_INF)
    m1 = jnp.max(el, axis=-1, keepdims=True)
    i1 = _first_lane_of(el == m1, lane)
    el2 = jnp.where(lane == i1, NEG_INF, el)
    m2 = jnp.max(el2, axis=-1, keepdims=True)
    i2 = _first_lane_of(el2 == m2, lane)
    e2 = jnp.exp(m2 - m1)
    w1 = 1.0 / (1.0 + e2) * p_top
    w2 = e2 / (1.0 + e2) * p_top
    g_ref[0] = jnp.where(lane == i1, w1, 0.0) + jnp.where(lane == i2, w2, 0.0)


def _router(h, mod, nw, w_route, ctx_len):
    B, L, d = h.shape
    tm = ROW_TILE
    row = lambda n: pl.BlockSpec((1, tm, n), lambda b, i: (b, i, 0))
    return pl.pallas_call(
        _router_kernel,
        grid=(B, L // tm),
        in_specs=[row(d), _mod_spec(ctx_len // tm), pl.BlockSpec((1, d), lambda b, i: (0, 0)),
                  pl.BlockSpec((d, ROUTE_W), lambda b, i: (0, 0))],
        out_specs=[row(d), row(ROUTE_W)],
        out_shape=[jax.ShapeDtypeStruct((B, L, d), BF16), jax.ShapeDtypeStruct((B, L, ROUTE_W), F32)],
        compiler_params=_cparams("parallel", "parallel"),
        name="moe_router",
    )(h, mod, nw.reshape(1, d), w_route)


def _experts_kernel(a_ref, g_ref, h_ref, mod_ref, wg_ref, wu_ref, wd_ref, o_ref, acc_ref):
    e = pl.program_id(2)

    @pl.when(e == 0)
    def _():
        acc_ref[...] = jnp.zeros_like(acc_ref)

    a = a_ref[0]
    gates = g_ref[0]
    lane = lax.broadcasted_iota(jnp.int32, gates.shape, 1)
    gate = jnp.sum(jnp.where(lane == e + EXPERT_LANE0, gates, 0.0), axis=-1, keepdims=True)
    hid = _silu(_dot(a, wg_ref[0])) * _dot(a, wu_ref[0]) * gate
    acc_ref[...] += _dot(hid.astype(BF16), wd_ref[0])

    @pl.when(e == N_EXPERTS - 1)
    def _():
        o_ref[0] = h_ref[0] + mod_ref[0, 0][5:6] * acc_ref[...]


def _experts(a, gates, h, mod, wg, wu, wd, ctx_len):
    B, L, d = h.shape
    tm = ROW_TILE
    nct = ctx_len // tm
    row = lambda n: pl.BlockSpec((1, tm, n), lambda b, i, e: (b, i, 0))
    return pl.pallas_call(
        _experts_kernel,
        grid=(B, L // tm, N_EXPERTS),
        in_specs=[row(d), row(ROUTE_W), row(d),
                  pl.BlockSpec((1, 1, 6, d), lambda b, i, e: (b, (i >= nct).astype(jnp.int32), 0, 0)),
                  pl.BlockSpec((1, d, EXPERT_FF), lambda b, i, e: (e, 0, 0)),
                  pl.BlockSpec((1, d, EXPERT_FF), lambda b, i, e: (e, 0, 0)),
                  pl.BlockSpec((1, EXPERT_FF, d), lambda b, i, e: (e, 0, 0))],
        out_specs=row(d),
        out_shape=jax.ShapeDtypeStruct((B, L, d), F32),
        scratch_shapes=[pltpu.VMEM((tm, d), F32)],
        compiler_params=_cparams("parallel", "parallel", "arbitrary"),
        name="moe_experts",
    )(a, gates, h, mod, wg, wu, wd)


def _final_norm_kernel(h_ref, nw_ref, o_ref):
    x = h_ref[0]
    o_ref[0] = x * lax.rsqrt(jnp.mean(x * x, axis=-1, keepdims=True) + EPS) * nw_ref[...]


def _final_norm(h, nw, ctx_len):
    B, L, d = h.shape
    tm = ROW_TILE
    nct = ctx_len // tm
    return pl.pallas_call(
        _final_norm_kernel,
        grid=(B, (L - ctx_len) // tm),
        in_specs=[pl.BlockSpec((1, tm, d), lambda b, i: (b, i + nct, 0)),
                  pl.BlockSpec((1, d), lambda b, i: (0, 0))],
        out_specs=pl.BlockSpec((1, tm, d), lambda b, i: (b, i, 0)),
        out_shape=jax.ShapeDtypeStruct((B, L - ctx_len, d), F32),
        compiler_params=_cparams("parallel", "parallel"),
        name="final_norm",
    )(h, nw.reshape(1, d))


def _rope_tables(seq):
    rows = seq // GRID_W
    row = jnp.repeat(jnp.arange(rows), GRID_W).astype(F32)
    col = (jnp.arange(rows * GRID_W) % GRID_W).astype(F32)
    axis_dim = ATT_DH // 2
    inv = ROPE_THETA ** (-jnp.arange(0, axis_dim, 2, dtype=F32) / axis_dim)
    ang = jnp.concatenate([row[:, None] * inv, col[:, None] * inv], axis=-1)
    cosf = jnp.repeat(jnp.cos(ang), 2, axis=-1)
    sinf = jnp.repeat(jnp.sin(ang), 2, axis=-1) * jnp.tile(jnp.array([-1.0, 1.0], F32), ATT_DH // 2)
    return jnp.tile(cosf, (1, 2)), jnp.tile(sinf, (1, 2))


def kernel(x, c, ctx, c_ctx, ada_w, ada_b, norm1_w, norm2_w, ev_w_in, ev_conv_w, ev_conv_b, ev_dt_bias, ev_a_log, ev_d_skip, ev_ssd_norm_w, ev_ig_bias, ev_fg_bias, ev_mlstm_norm_w, ev_w_out, od_w_in, od_sink, od_q_norm_w, od_k_norm_w, od_w_out, moe_w_group, moe_w_expert, moe_w_gate, moe_w_up, moe_w_down, final_norm_w):
    B, S, d = x.shape
    T = ctx.shape[1]
    depth = ada_w.shape[0]
    assert d == D_MODEL and T % ROW_TILE == 0 and S % ROW_TILE == 0 and S % GRID_W == 0 and S >= 3 * QBLK

    h = jnp.concatenate([ctx, x], axis=1)

    rows = -(-(B + 1) // 8) * 8
    cond = jnp.zeros((rows, d), F32).at[:B].set(c).at[B].set(c_ctx)
    ada = _adaln(cond, ada_w, ada_b).reshape(depth, rows, 6, d)
    mods = jnp.stack([jnp.broadcast_to(ada[:, B:B + 1], (depth, B, 6, d)), ada[:, :B]], axis=2)

    cosf, sinf = _rope_tables(S)

    for layer in range(depth):
        li = layer // 2
        mod = mods[layer]
        if layer % 2 == 0:
            w_in = ev_w_in[li]
            conv, z, v, o, gate_cols = (w_in[:, :CONV_CH], w_in[:, CONV_CH:CONV_CH + SSD_INNER],
                                        w_in[:, CONV_CH + SSD_INNER:CONV_CH + 2 * SSD_INNER],
                                        w_in[:, CONV_CH + 2 * SSD_INNER:CONV_CH + 3 * SSD_INNER],
                                        w_in[:, CONV_CH + 3 * SSD_INNER:])
            w_rec = jnp.concatenate([v, conv], axis=1).astype(BF16)
            w_zo = jnp.concatenate([z, o], axis=1).astype(BF16)
            w_g = jnp.pad(gate_cols, ((0, 0), (0, GATE_W - N_GATE))).astype(BF16)
            rec, zo, gates = _norm_mod_matmul(h, mod, norm1_w[layer], [w_rec, w_zo, w_g], [BF16, BF16, F32], 0, 1, T)
            gates_t = jnp.swapaxes(gates[:, :, :N_GATE], 1, 2)
            bias = jnp.concatenate([ev_dt_bias[li].reshape(-1), ev_ig_bias[li].reshape(-1), ev_fg_bias[li].reshape(-1)])
            alog = jnp.pad(ev_a_log[li].reshape(-1), (0, N_GATE - 2 * SSD_HEADS))
            prow = jnp.pad(jnp.stack([bias, alog]), ((0, 0), (0, GATE_W - N_GATE)))
            pcol = jnp.stack([bias, alog], axis=1)
            dskip = jnp.repeat(ev_d_skip[li], SSD_HEAD_DIM).reshape(1, SSD_INNER)
            cb = ev_conv_b[li].reshape(1, CONV_CH)
            yf = _mixer_scan(rec, gates, gates_t, ev_conv_w[li], cb, prow, pcol, dskip, T, False)
            yb = _mixer_scan(rec, gates, gates_t, ev_conv_w[li], cb, prow, pcol, dskip, T, True)
            h = _even_out(yf, yb, zo, h, mod, ev_ssd_norm_w[li], ev_mlstm_norm_w[li], ev_w_out[li].astype(BF16), T)
        else:
            (proj,) = _norm_mod_matmul(h, mod, norm1_w[layer], [od_w_in[li].astype(BF16)], [BF16], 0, 1, T)
            qnw = jnp.tile(od_q_norm_w[li], ATT_HEADS).reshape(1, -1)
            knw = jnp.tile(od_k_norm_w[li], ATT_KV).reshape(1, -1)
            qs, ks, vsl, vsr, qd, kd, vdl, vdr = _qkv_prep(proj, cosf, sinf, qnw, knw, T)
            sink_b = jnp.broadcast_to(od_sink[li].reshape(ATT_HEADS, 1), (ATT_HEADS, 128))
            ys_lat = _window_attn(qs, ks, vsl, vsr, sink_b, T)
            yd_lat = _dense_attn(qd, kd, vdl, vdr, T)
            ys_ctx, yd_ctx = _ctx_attn(qs, ks, vsl, vsr, qd, kd, vdl, vdr, sink_b, T)
            ys = jnp.concatenate([ys_ctx, ys_lat], axis=1)
            yd = jnp.concatenate([yd_ctx, yd_lat], axis=1)
            h = _odd_out(ys, yd, h, mod, od_w_out[li].astype(BF16), T)

        w_route = jnp.pad(jnp.concatenate([moe_w_group[layer], moe_w_expert[layer]], axis=1),
                          ((0, 0), (0, ROUTE_W - N_GROUPS - N_EXPERTS)))
        a2, gates = _router(h, mod, norm2_w[layer], w_route, T)
        wg = moe_w_gate[layer].reshape(N_EXPERTS, d, EXPERT_FF).astype(BF16)
        wu = moe_w_up[layer].reshape(N_EXPERTS, d, EXPERT_FF).astype(BF16)
        wd = moe_w_down[layer].reshape(N_EXPERTS, EXPERT_FF, d).astype(BF16)
        h = _experts(a2, gates, h, mod, wg, wu, wd, T)

    return _final_norm(h, final_norm_w, T)
```

```python
import functools
import math

import jax
import jax.numpy as jnp
from jax import lax
from jax.experimental import pallas as pl
from jax.experimental.pallas import tpu as pltpu

F32 = jnp.float32
BF16 = jnp.bfloat16
EPS = 1e-6
NEG_INF = float("-inf")

D_MODEL = 1024
GRID_W = 64
ROPE_THETA = 10000.0

SSD_HEADS = 16
SSD_HEAD_DIM = 64
SSD_INNER = 1024
SSD_GROUPS = 2
SSD_STATE = 128
MLSTM_HEADS = 4
MLSTM_QK_DIM = 128
MLSTM_V_DIM = 256
MLSTM_QK = 512
MLSTM_INNER = 1024
CHUNK = 128
CONV_CH = 2560
REC_W = MLSTM_INNER + CONV_CH
N_GATE = 48
GATE_W = 128
COL_DT, COL_IG, COL_FG = 0, 32, 40

ATT_DH = 64
ATT_HEADS = 8
ATT_KV = 2
ATT_G = ATT_HEADS // ATT_KV
WINDOW = 128
QBLK = 128
ODD_PROJ = 1536

N_GROUPS = 4
EXPERTS_PER_GROUP = 4
N_EXPERTS = N_GROUPS * EXPERTS_PER_GROUP
EXPERT_FF = 256
ROUTE_W = 128
EXPERT_LANE0 = N_GROUPS

ROW_TILE = 256
GIDX_LANE = 8
RUN_ALIGN = 16
RUN_PIECES = (256, 128, 64, 32, 16)
SORT_ROWS = -(-(ROW_TILE + N_GROUPS * (RUN_ALIGN - 1)) // 128) * 128
EXPERT_TILE = 512
VMEM_LIMIT = 56 * 1024 * 1024


def _cparams(*sem):
    return pltpu.CompilerParams(dimension_semantics=sem, vmem_limit_bytes=VMEM_LIMIT)


def _sigmoid(x):
    return 1.0 / (1.0 + jnp.exp(-x))


def _silu(x):
    return x * _sigmoid(x)


def _softplus(x):
    return jnp.maximum(x, 0.0) + jnp.log1p(jnp.exp(-jnp.abs(x)))


def _dot(a, b):
    return jnp.dot(a, b, preferred_element_type=F32)


def _dot_nt(a, b):
    return lax.dot_general(a, b, (((1,), (1,)), ((), ())), preferred_element_type=F32)


def _split3(a):
    hi = a.astype(BF16)
    r1 = a - hi.astype(F32)
    mid = r1.astype(BF16)
    lo = (r1 - mid.astype(F32)).astype(BF16)
    return hi, mid, lo


def _dot3_rhs(exact, a):
    hi, mid, lo = _split3(a)
    return _dot(exact, hi) + _dot(exact, mid) + _dot(exact, lo)


def _dot3_lhs(a, exact):
    hi, mid, lo = _split3(a)
    return _dot(hi, exact) + _dot(mid, exact) + _dot(lo, exact)


def _adaln_kernel(cond_ref, w_ref, b_ref, o_ref):
    s = _silu(cond_ref[...])
    o_ref[0] = _dot(s.astype(BF16), w_ref[0].astype(BF16)) + b_ref[0]


def _adaln(cond, ada_w, ada_b):
    depth, d, n = ada_w.shape
    rows = cond.shape[0]
    tn = 1536
    return pl.pallas_call(
        _adaln_kernel,
        grid=(depth, n // tn),
        in_specs=[pl.BlockSpec((rows, d), lambda l, j: (0, 0)),
                  pl.BlockSpec((1, d, tn), lambda l, j: (l, 0, j)),
                  pl.BlockSpec((1, 1, tn), lambda l, j: (l, 0, j))],
        out_specs=pl.BlockSpec((1, rows, tn), lambda l, j: (l, 0, j)),
        out_shape=jax.ShapeDtypeStruct((depth, rows, n), F32),
        compiler_params=_cparams("parallel", "parallel"),
        name="adaln",
    )(cond, ada_w, ada_b.reshape(depth, 1, n))


def _norm_mod(x, nw, mod, shift_row, scale_row):
    var = jnp.mean(x * x, axis=-1, keepdims=True)
    y = x * lax.rsqrt(var + EPS) * nw
    return y * (1.0 + mod[scale_row:scale_row + 1]) + mod[shift_row:shift_row + 1]


def _nmm_kernel(h_ref, mod_ref, nw_ref, *rest, n_out, shift_row, scale_row, tn):
    w_refs, o_refs = rest[:n_out], rest[n_out:]
    a16 = _norm_mod(h_ref[0], nw_ref[...], mod_ref[0, 0], shift_row, scale_row).astype(BF16)
    for w_ref, o_ref in zip(w_refs, o_refs):
        n = w_ref.shape[1]
        for j in range(0, n, tn):
            w = min(tn, n - j)
            o_ref[0, :, j:j + w] = _dot(a16, w_ref[:, j:j + w]).astype(o_ref.dtype)


def _mod_spec(nct_tiles):
    return pl.BlockSpec((1, 1, 6, D_MODEL), lambda b, i: (b, (i >= nct_tiles).astype(jnp.int32), 0, 0))


def _norm_mod_matmul(h, mod, nw, weights, out_dtypes, shift_row, scale_row, ctx_len):
    B, L, d = h.shape
    tm = ROW_TILE
    in_specs = [pl.BlockSpec((1, tm, d), lambda b, i: (b, i, 0)),
                _mod_spec(ctx_len // tm),
                pl.BlockSpec((1, d), lambda b, i: (0, 0))]
    out_specs, out_shape = [], []
    for w, dt in zip(weights, out_dtypes):
        n = w.shape[1]
        in_specs.append(pl.BlockSpec((d, n), lambda b, i: (0, 0)))
        out_specs.append(pl.BlockSpec((1, tm, n), lambda b, i: (b, i, 0)))
        out_shape.append(jax.ShapeDtypeStruct((B, L, n), dt))
    return pl.pallas_call(
        functools.partial(_nmm_kernel, n_out=len(weights), shift_row=shift_row, scale_row=scale_row, tn=512),
        grid=(B, L // tm),
        in_specs=in_specs, out_specs=out_specs, out_shape=out_shape,
        compiler_params=_cparams("parallel", "parallel"),
        name="norm_mod_matmul",
    )(h, mod, nw.reshape(1, d), *weights)


def _mixer_kernel(rec_ref, prev_ref, next_ref, g_ref, gt_ref, cw_ref, cb_ref, prow_ref, pcol_ref, dsk_ref,
                  y_ref, conv_ref, S_ref, C_ref, n_ref, m_ref, *, reverse, nct, nc):
    i = pl.program_id(1)
    if reverse:
        c = jnp.where(i < nct, nct - 1 - i, nc - 1 - (i - nct))
    else:
        c = i
    d = 1 if reverse else 0

    @pl.when(i == 0)
    def _():
        S_ref[...] = jnp.zeros_like(S_ref)
        C_ref[...] = jnp.zeros_like(C_ref)
        n_ref[...] = jnp.zeros_like(n_ref)
        m_ref[...] = jnp.zeros_like(m_ref)

    seq_start = jnp.logical_or(c == 0, c == nct)
    seq_end = jnp.logical_or(c == nct - 1, c == nc - 1)

    xc = rec_ref[0, :, MLSTM_INNER:].astype(F32)
    before = prev_ref[0].astype(F32)[CHUNK // 8 - 1:CHUNK // 8, MLSTM_INNER:]
    after = next_ref[0].astype(F32)[0:1, MLSTM_INNER:]
    before = jnp.where(seq_start, 0.0, before)
    after = jnp.where(seq_end, 0.0, after)
    rid = lax.broadcasted_iota(jnp.int32, (CHUNK, 1), 0)
    x_prev = jnp.where(rid == 0, before, pltpu.roll(xc, 1, 0))
    x_next = jnp.where(rid == CHUNK - 1, after, pltpu.roll(xc, CHUNK - 1, 0))
    cw = cw_ref[...]
    conv_ref[...] = _silu(cw[0:1] * x_prev + cw[1:2] * xc + cw[2:3] * x_next + cb_ref[...])

    li = lax.broadcasted_iota(jnp.int32, (CHUNK, CHUNK), 0)
    si = lax.broadcasted_iota(jnp.int32, (CHUNK, CHUNK), 1)
    mask = (si >= li) if reverse else (si <= li)
    tri = mask.astype(BF16)
    tri_t = ((li >= si) if reverse else (li <= si)).astype(BF16)
    last = 0 if reverse else CHUNK - 1

    G = g_ref[0] + prow_ref[0:1]
    GT = gt_ref[0] + pcol_ref[:, 0:1]
    lane = lax.broadcasted_iota(jnp.int32, (CHUNK, GATE_W), 1)
    row = lax.broadcasted_iota(jnp.int32, (N_GATE, CHUNK), 0)
    dt_c = _softplus(G)
    dt_r = _softplus(GT)
    nega_c = -jnp.exp(prow_ref[1:2])
    nega_r = -jnp.exp(pcol_ref[:, 1:2])
    pre_c = jnp.where(lane < COL_IG, dt_c * nega_c, jnp.where(lane >= COL_FG, -_softplus(-G), 0.0))
    pre_c = jnp.where(lane < N_GATE, pre_c, 0.0)
    pre_r = jnp.where(row < COL_IG, dt_r * nega_r, jnp.where(row >= COL_FG, -_softplus(-GT), 0.0))
    cum_c = _dot3_rhs(tri, pre_c)
    cum_r = _dot3_lhs(pre_r, tri_t)

    a_c = cum_c[:, d * SSD_HEADS:(d + 1) * SSD_HEADS]
    a_r = cum_r[d * SSD_HEADS:(d + 1) * SSD_HEADS, :]
    a_last = a_c[last:last + 1, :]
    hh = lax.broadcasted_iota(jnp.int32, (SSD_HEADS, SSD_INNER), 0)
    hj = lax.broadcasted_iota(jnp.int32, (SSD_HEADS, SSD_INNER), 1)
    expand = (hj // SSD_HEAD_DIM == hh).astype(BF16)
    dt_x = _dot3_lhs(dt_c[:, d * SSD_HEADS:(d + 1) * SSD_HEADS], expand)
    dec_x = _dot3_lhs(jnp.exp(a_last - a_c), expand)
    ein_x = _dot3_lhs(jnp.exp(a_c), expand)
    cdec_x = _dot3_lhs(jnp.broadcast_to(jnp.exp(a_last), (8, SSD_HEADS)), expand)[0:1]

    xs = conv_ref[:, 0:SSD_INNER]
    xsdt = xs * dt_x
    xsdt16 = xsdt.astype(BF16)
    xdec16 = (xsdt * dec_x).astype(BF16)
    lane128 = lax.broadcasted_iota(jnp.int32, (CHUNK, 128), 1)
    heads_per_group = SSD_HEADS // SSD_GROUPS
    gw = heads_per_group * SSD_HEAD_DIM
    for g in range(SSD_GROUPS):
        Bg = conv_ref[:, SSD_INNER + g * SSD_STATE:SSD_INNER + (g + 1) * SSD_STATE]
        Cg16 = conv_ref[:, SSD_INNER + 256 + g * SSD_STATE:SSD_INNER + 256 + (g + 1) * SSD_STATE].astype(BF16)
        CB = _dot_nt(Cg16, Bg.astype(BF16))
        S_g = S_ref[:, g * gw:(g + 1) * gw]
        y_off = _dot(Cg16, S_g.astype(BF16)) * ein_x[:, g * gw:(g + 1) * gw]
        for pr in range(heads_per_group // 2):
            col0 = g * gw + pr * 128
            xpair = xsdt16[:, col0:col0 + 128]
            acc = None
            for half in range(2):
                h = g * heads_per_group + pr * 2 + half
                seg = a_c[:, h:h + 1] - a_r[h:h + 1, :]
                Lm = jnp.exp(jnp.where(mask, seg, NEG_INF))
                Mh = (CB * Lm).astype(BF16)
                keep = (lane128 < 64) if half == 0 else (lane128 >= 64)
                part = _dot(Mh, jnp.where(keep, xpair, jnp.zeros_like(xpair)))
                acc = part if acc is None else acc + part
            yv = acc + y_off[:, pr * 128:(pr + 1) * 128]
            if not reverse:
                yv = yv + xs[:, col0:col0 + 128] * dsk_ref[:, col0:col0 + 128]
            y_ref[0, :, col0:col0 + 128] = yv
        S_ref[:, g * gw:(g + 1) * gw] = (cdec_x[:, g * gw:(g + 1) * gw] * S_g
                                         + _dot(Bg.T.astype(BF16), xdec16[:, g * gw:(g + 1) * gw]))

    for h in range(MLSTM_HEADS):
        gi = COL_IG + d * MLSTM_HEADS + h
        gf = COL_FG + d * MLSTM_HEADS + h
        qh16 = conv_ref[:, 1536 + h * 128:1536 + (h + 1) * 128].astype(BF16)
        kh = conv_ref[:, 2048 + h * 128:2048 + (h + 1) * 128] * (MLSTM_QK_DIM ** -0.5)
        kh16 = kh.astype(BF16)
        vh16 = rec_ref[0, :, h * MLSTM_V_DIM:(h + 1) * MLSTM_V_DIM]
        ig_c, ig_r = G[:, gi:gi + 1], GT[gi:gi + 1, :]
        b_c, b_r = cum_c[:, gf:gf + 1], cum_r[gf:gf + 1, :]
        b_last = b_c[last:last + 1, :]
        Dlog = jnp.where(mask, b_c - b_r + ig_r, NEG_INF)
        m_loc = jnp.max(b_last - b_r + ig_r, axis=-1, keepdims=True)
        ek = jnp.exp(b_last - b_c + ig_c - m_loc) * kh
        C_loc = _dot(ek.T.astype(BF16), vh16)
        n_loc = jnp.sum(ek, axis=0, keepdims=True)
        C_in = C_ref[h]
        n_in = n_ref[h:h + 1, :]
        m_in = m_ref[h:h + 1, 0:1]
        m_inter = b_c + m_in
        m_t = jnp.maximum(m_inter, jnp.max(Dlog, axis=-1, keepdims=True))
        P = jnp.exp(Dlog - m_t) * _dot_nt(qh16, kh16)
        w_inter = jnp.exp(m_inter - m_t)
        num = _dot(P.astype(BF16), vh16) + w_inter * _dot(qh16, C_in.astype(BF16))
        qn = jnp.sum(qh16.astype(F32) * n_in, axis=-1, keepdims=True)
        den = jnp.sum(P, axis=-1, keepdims=True) + w_inter * qn
        den = jnp.maximum(jnp.abs(den), jnp.exp(-m_t))
        y_ref[0, :, SSD_INNER + h * MLSTM_V_DIM:SSD_INNER + (h + 1) * MLSTM_V_DIM] = num / den
        m_new = jnp.maximum(b_last + m_in, m_loc)
        fa = jnp.exp(b_last + m_in - m_new)
        fb = jnp.exp(m_loc - m_new)
        C_ref[h] = fa * C_in + fb * C_loc
        n_ref[h:h + 1, :] = fa * n_in + fb * n_loc
        m_ref[h:h + 1, :] = jnp.broadcast_to(m_new, (1, 128))


def _mixer_scan(rec, gates, gates_t, conv_w, conv_b, prow, pcol, dskip, ctx_len, reverse):
    B, L, _ = rec.shape
    nc, nct = L // CHUNK, ctx_len // CHUNK
    halo = 16
    hb = CHUNK // halo

    def chunk_of(i):
        if reverse:
            return jnp.where(i < nct, nct - 1 - i, nc - 1 - (i - nct))
        return i

    return pl.pallas_call(
        functools.partial(_mixer_kernel, reverse=reverse, nct=nct, nc=nc),
        grid=(B, nc),
        in_specs=[pl.BlockSpec((1, CHUNK, REC_W), lambda b, i: (b, chunk_of(i), 0)),
                  pl.BlockSpec((1, halo, REC_W), lambda b, i: (b, jnp.maximum(chunk_of(i) * hb - 1, 0), 0)),
                  pl.BlockSpec((1, halo, REC_W),
                               lambda b, i: (b, jnp.minimum((chunk_of(i) + 1) * hb, L // halo - 1), 0)),
                  pl.BlockSpec((1, CHUNK, GATE_W), lambda b, i: (b, chunk_of(i), 0)),
                  pl.BlockSpec((1, N_GATE, CHUNK), lambda b, i: (b, 0, chunk_of(i))),
                  pl.BlockSpec((3, CONV_CH), lambda b, i: (0, 0)),
                  pl.BlockSpec((1, CONV_CH), lambda b, i: (0, 0)),
                  pl.BlockSpec((2, GATE_W), lambda b, i: (0, 0)),
                  pl.BlockSpec((N_GATE, 2), lambda b, i: (0, 0)),
                  pl.BlockSpec((1, SSD_INNER), lambda b, i: (0, 0))],
        out_specs=pl.BlockSpec((1, CHUNK, SSD_INNER + MLSTM_INNER), lambda b, i: (b, chunk_of(i), 0)),
        out_shape=jax.ShapeDtypeStruct((B, L, SSD_INNER + MLSTM_INNER), F32),
        scratch_shapes=[pltpu.VMEM((CHUNK, CONV_CH), F32),
                        pltpu.VMEM((SSD_STATE, SSD_INNER), F32),
                        pltpu.VMEM((MLSTM_HEADS, MLSTM_QK_DIM, MLSTM_V_DIM), F32),
                        pltpu.VMEM((8, MLSTM_QK_DIM), F32),
                        pltpu.VMEM((8, 128), F32)],
        compiler_params=_cparams("parallel", "arbitrary"),
        name="mixer_bwd" if reverse else "mixer_fwd",
    )(rec, rec, rec, gates, gates_t, conv_w, conv_b, prow, pcol, dskip)


def _group_rms(x, groups):
    w = x.shape[-1] // groups
    parts = []
    for g in range(groups):
        seg = x[:, g * w:(g + 1) * w]
        parts.append(seg * lax.rsqrt(jnp.mean(seg * seg, axis=-1, keepdims=True) + EPS))
    return jnp.concatenate(parts, axis=-1)


def _even_out_kernel(yf_ref, yb_ref, zo_ref, h_ref, mod_ref, snw_ref, mnw_ref, w_ref, o_ref):
    y = yf_ref[0] + yb_ref[0]
    z = zo_ref[0, :, 0:SSD_INNER].astype(F32)
    o = zo_ref[0, :, SSD_INNER:].astype(F32)
    ys = _group_rms(y[:, 0:SSD_INNER] * _silu(z), SSD_GROUPS) * snw_ref[...]
    hm = _group_rms(y[:, SSD_INNER:], MLSTM_HEADS) * mnw_ref[...] * _sigmoid(o)
    mix = jnp.concatenate([ys, hm], axis=-1).astype(BF16)
    o_ref[0] = h_ref[0] + mod_ref[0, 0][2:3] * _dot(mix, w_ref[...])


def _even_out(yf, yb, zo, h, mod, snw, mnw, w_out, ctx_len):
    B, L, d = h.shape
    tm = ROW_TILE
    wide = SSD_INNER + MLSTM_INNER
    row = lambda n: pl.BlockSpec((1, tm, n), lambda b, i: (b, i, 0))
    const = lambda r, n: pl.BlockSpec((r, n), lambda b, i: (0, 0))
    return pl.pallas_call(
        _even_out_kernel,
        grid=(B, L // tm),
        in_specs=[row(wide), row(wide), row(wide), row(d), _mod_spec(ctx_len // tm),
                  const(1, SSD_INNER), const(1, MLSTM_INNER), const(wide, d)],
        out_specs=row(d),
        out_shape=jax.ShapeDtypeStruct((B, L, d), F32),
        compiler_params=_cparams("parallel", "parallel"),
        name="even_out",
    )(yf, yb, zo, h, mod, snw.reshape(1, -1), mnw.reshape(1, -1), w_out)


def _qkv_prep_kernel(p_ref, cs_ref, sn_ref, qnw_ref, knw_ref,
                     qs_ref, ks_ref, vsl_ref, vsr_ref, qd_ref, kd_ref, vdl_ref, vdr_ref, *, nct):
    is_lat = pl.program_id(1) >= nct
    p = p_ref[0].astype(F32)
    cs, sn = cs_ref[...], sn_ref[...]

    def rope(x):
        w = x.shape[1]
        reps = w // 128
        c = jnp.concatenate([cs] * reps, axis=1) if reps > 1 else cs
        s = jnp.concatenate([sn] * reps, axis=1) if reps > 1 else sn
        lane = lax.broadcasted_iota(jnp.int32, x.shape, 1)
        partner = jnp.where(lane % 2 == 0, pltpu.roll(x, w - 1, 1), pltpu.roll(x, 1, 1))
        return jnp.where(is_lat, x * c + partner * s, x)

    def head_rms(x, nw):
        w = x.shape[1]
        a = lax.broadcasted_iota(jnp.int32, (w, w), 0) // ATT_DH
        b = lax.broadcasted_iota(jnp.int32, (w, w), 1) // ATT_DH
        same = (a == b).astype(BF16)
        ms = _dot3_lhs(x * x, same) * (1.0 / ATT_DH)
        return x * lax.rsqrt(ms + EPS) * nw

    qw, kw = ATT_HEADS * ATT_DH, ATT_KV * ATT_DH
    o = 0
    q_s = rope(p[:, o:o + qw]) * (ATT_DH ** -0.5); o += qw
    k_s = rope(p[:, o:o + kw]); o += kw
    v_s = p[:, o:o + kw]; o += kw
    q_d = rope(head_rms(p[:, o:o + qw], qnw_ref[...])) * (ATT_DH ** -0.5); o += qw
    k_d = rope(head_rms(p[:, o:o + kw], knw_ref[...])); o += kw
    v_d = p[:, o:o + kw]

    lane = lax.broadcasted_iota(jnp.int32, (QBLK, 128), 1)
    for h in range(ATT_HEADS):
        qs_ref[0, h] = q_s[:, h * ATT_DH:(h + 1) * ATT_DH].astype(BF16)
        qd_ref[0, h] = q_d[:, h * ATT_DH:(h + 1) * ATT_DH].astype(BF16)
    for kv in range(ATT_KV):
        ks_ref[0, kv] = k_s[:, kv * ATT_DH:(kv + 1) * ATT_DH].astype(BF16)
        kd_ref[0, kv] = k_d[:, kv * ATT_DH:(kv + 1) * ATT_DH].astype(BF16)
    for v, l_ref, r_ref in ((v_s, vsl_ref, vsr_ref), (v_d, vdl_ref, vdr_ref)):
        for kv in range(ATT_KV):
            own = (lane // ATT_DH) == kv
            same = jnp.where(own, v, 0.0)
            other = pltpu.roll(same, ATT_DH, 1)
            left, right = (same, other) if kv == 0 else (other, same)
            l_ref[0, kv] = left.astype(BF16)
            r_ref[0, kv] = right.astype(BF16)


def _qkv_prep(proj, cosf, sinf, qnw, knw, ctx_len):
    B, L, _ = proj.shape
    nct = ctx_len // QBLK
    tab = pl.BlockSpec((QBLK, 128), lambda b, i: (jnp.maximum(i - nct, 0), 0))
    head = lambda n, w: pl.BlockSpec((1, n, QBLK, w), lambda b, i: (b, 0, i, 0))
    shp = lambda n, w: jax.ShapeDtypeStruct((B, n, L, w), BF16)
    return pl.pallas_call(
        functools.partial(_qkv_prep_kernel, nct=nct),
        grid=(B, L // QBLK),
        in_specs=[pl.BlockSpec((1, QBLK, ODD_PROJ), lambda b, i: (b, i, 0)), tab, tab,
                  pl.BlockSpec((1, ATT_HEADS * ATT_DH), lambda b, i: (0, 0)),
                  pl.BlockSpec((1, ATT_KV * ATT_DH), lambda b, i: (0, 0))],
        out_specs=[head(ATT_HEADS, ATT_DH), head(ATT_KV, ATT_DH), head(ATT_KV, 128), head(ATT_KV, 128),
                   head(ATT_HEADS, ATT_DH), head(ATT_KV, ATT_DH), head(ATT_KV, 128), head(ATT_KV, 128)],
        out_shape=[shp(ATT_HEADS, ATT_DH), shp(ATT_KV, ATT_DH), shp(ATT_KV, 128), shp(ATT_KV, 128),
                   shp(ATT_HEADS, ATT_DH), shp(ATT_KV, ATT_DH), shp(ATT_KV, 128), shp(ATT_KV, 128)],
        compiler_params=_cparams("parallel", "parallel"),
        name="qkv_prep",
    )(proj, cosf, sinf, qnw, knw)


def _pair_out(P16, vl, vr, inv_l, rows):
    outs = []
    for pr in range(ATT_G // 2):
        a, b = 2 * pr, 2 * pr + 1
        oa = _dot(P16[a * rows:(a + 1) * rows], vl) * inv_l[a * rows:(a + 1) * rows]
        ob = _dot(P16[b * rows:(b + 1) * rows], vr) * inv_l[b * rows:(b + 1) * rows]
        outs.append(oa + ob)
    return jnp.concatenate(outs, axis=-1)


def _dense_attn_kernel(q_ref, k_ref, vl_ref, vr_ref, o_ref):
    for kv in range(ATT_KV):
        Q = q_ref[0, kv * ATT_G:(kv + 1) * ATT_G].reshape(ATT_G * QBLK, ATT_DH)
        S = _dot_nt(Q, k_ref[0, kv])
        m = jnp.max(S, axis=-1, keepdims=True)
        P = jnp.exp(S - m)
        inv_l = 1.0 / jnp.sum(P, axis=-1, keepdims=True)
        o_ref[0, :, kv * ATT_G * ATT_DH:(kv + 1) * ATT_G * ATT_DH] = _pair_out(
            P.astype(BF16), vl_ref[0, kv], vr_ref[0, kv], inv_l, QBLK).astype(o_ref.dtype)


def _dense_attn(qd, kd, vdl, vdr, ctx_len):
    B, _, L, _ = qd.shape
    S = L - ctx_len
    nct = ctx_len // QBLK
    full = lambda w: pl.BlockSpec((1, ATT_KV, L, w), lambda b, j: (b, 0, 0, 0))
    return pl.pallas_call(
        _dense_attn_kernel,
        grid=(B, S // QBLK),
        in_specs=[pl.BlockSpec((1, ATT_HEADS, QBLK, ATT_DH), lambda b, j: (b, 0, j + nct, 0)),
                  full(ATT_DH), full(128), full(128)],
        out_specs=pl.BlockSpec((1, QBLK, ATT_HEADS * ATT_DH), lambda b, j: (b, j, 0)),
        out_shape=jax.ShapeDtypeStruct((B, S, ATT_HEADS * ATT_DH), BF16),
        compiler_params=_cparams("parallel", "parallel"),
        name="dense_attn",
    )(qd, kd, vdl, vdr)


def _sink_rows(sink_ref, kv, rows):
    return jnp.concatenate(
        [jnp.broadcast_to(sink_ref[kv * ATT_G + g:kv * ATT_G + g + 1, 0:1], (rows, 1)) for g in range(ATT_G)], axis=0)


def _window_attn_kernel(q_ref, k_ref, vl_ref, vr_ref, sink_ref, o_ref, *, ctx_len, seq):
    j = pl.program_id(1)
    span = 3 * QBLK
    start = jnp.clip((j - 1) * QBLK, 0, seq - span)
    kstart = pl.multiple_of(ctx_len + start, QBLK)
    rows = ATT_G * QBLK
    qpos = j * QBLK + lax.broadcasted_iota(jnp.int32, (rows, span), 0) % QBLK
    kpos = start + lax.broadcasted_iota(jnp.int32, (rows, span), 1)
    valid = jnp.abs(kpos - qpos) <= WINDOW
    for kv in range(ATT_KV):
        Q = q_ref[0, kv * ATT_G:(kv + 1) * ATT_G].reshape(rows, ATT_DH)
        Sc = _dot_nt(Q, k_ref[0, kv, 0:ctx_len])
        Sl = jnp.where(valid, _dot_nt(Q, k_ref[0, kv, pl.ds(kstart, span)]), NEG_INF)
        sk = _sink_rows(sink_ref, kv, QBLK)
        m = jnp.maximum(jnp.maximum(jnp.max(Sc, axis=-1, keepdims=True), jnp.max(Sl, axis=-1, keepdims=True)), sk)
        Pc = jnp.exp(Sc - m)
        Pl = jnp.exp(Sl - m)
        inv_l = 1.0 / (jnp.exp(sk - m) + jnp.sum(Pc, axis=-1, keepdims=True) + jnp.sum(Pl, axis=-1, keepdims=True))
        out = (_pair_out(Pc.astype(BF16), vl_ref[0, kv, 0:ctx_len], vr_ref[0, kv, 0:ctx_len], inv_l, QBLK)
               + _pair_out(Pl.astype(BF16), vl_ref[0, kv, pl.ds(kstart, span)], vr_ref[0, kv, pl.ds(kstart, span)],
                           inv_l, QBLK))
        o_ref[0, :, kv * ATT_G * ATT_DH:(kv + 1) * ATT_G * ATT_DH] = out.astype(o_ref.dtype)


def _window_attn(qs, ks, vsl, vsr, sink_b, ctx_len):
    B, _, L, _ = qs.shape
    S = L - ctx_len
    nct = ctx_len // QBLK
    full = lambda w: pl.BlockSpec((1, ATT_KV, L, w), lambda b, j: (b, 0, 0, 0))
    return pl.pallas_call(
        functools.partial(_window_attn_kernel, ctx_len=ctx_len, seq=S),
        grid=(B, S // QBLK),
        in_specs=[pl.BlockSpec((1, ATT_HEADS, QBLK, ATT_DH), lambda b, j: (b, 0, j + nct, 0)),
                  full(ATT_DH), full(128), full(128),
                  pl.BlockSpec((ATT_HEADS, 128), lambda b, j: (0, 0))],
        out_specs=pl.BlockSpec((1, QBLK, ATT_HEADS * ATT_DH), lambda b, j: (b, j, 0)),
        out_shape=jax.ShapeDtypeStruct((B, S, ATT_HEADS * ATT_DH), BF16),
        compiler_params=_cparams("parallel", "parallel"),
        name="window_attn",
    )(qs, ks, vsl, vsr, sink_b)


def _ctx_attn_kernel(qs_ref, ks_ref, vsl_ref, vsr_ref, qd_ref, kd_ref, vdl_ref, vdr_ref, sink_ref,
                     os_ref, od_ref, *, ctx_len):
    T = ctx_len
    for q_ref, k_ref, vl_ref, vr_ref, o_ref, with_sink in ((qs_ref, ks_ref, vsl_ref, vsr_ref, os_ref, True),
                                                           (qd_ref, kd_ref, vdl_ref, vdr_ref, od_ref, False)):
        for kv in range(ATT_KV):
            Q = q_ref[0, kv * ATT_G:(kv + 1) * ATT_G].reshape(ATT_G * T, ATT_DH)
            S = _dot_nt(Q, k_ref[0, kv])
            m = jnp.max(S, axis=-1, keepdims=True)
            if with_sink:
                sk = _sink_rows(sink_ref, kv, T)
                m = jnp.maximum(m, sk)
            P = jnp.exp(S - m)
            l = jnp.sum(P, axis=-1, keepdims=True)
            if with_sink:
                l = l + jnp.exp(sk - m)
            o_ref[0, :, kv * ATT_G * ATT_DH:(kv + 1) * ATT_G * ATT_DH] = _pair_out(
                P.astype(BF16), vl_ref[0, kv], vr_ref[0, kv], 1.0 / l, T).astype(o_ref.dtype)


def _ctx_attn(qs, ks, vsl, vsr, qd, kd, vdl, vdr, sink_b, ctx_len):
    B = qs.shape[0]
    T = ctx_len
    blk = lambda n, w: pl.BlockSpec((1, n, T, w), lambda b: (b, 0, 0, 0))
    out = pl.BlockSpec((1, T, ATT_HEADS * ATT_DH), lambda b: (b, 0, 0))
    return pl.pallas_call(
        functools.partial(_ctx_attn_kernel, ctx_len=T),
        grid=(B,),
        in_specs=[blk(ATT_HEADS, ATT_DH), blk(ATT_KV, ATT_DH), blk(ATT_KV, 128), blk(ATT_KV, 128),
                  blk(ATT_HEADS, ATT_DH), blk(ATT_KV, ATT_DH), blk(ATT_KV, 128), blk(ATT_KV, 128),
                  pl.BlockSpec((ATT_HEADS, 128), lambda b: (0, 0))],
        out_specs=[out, out],
        out_shape=[jax.ShapeDtypeStruct((B, T, ATT_HEADS * ATT_DH), BF16)] * 2,
        compiler_params=_cparams("parallel"),
        name="ctx_attn",
    )(qs, ks, vsl, vsr, qd, kd, vdl, vdr, sink_b)


def _odd_out_kernel(ys_ref, yd_ref, h_ref, mod_ref, w_ref, o_ref):
    mix = jnp.concatenate([ys_ref[0], yd_ref[0]], axis=-1)
    o_ref[0] = h_ref[0] + mod_ref[0, 0][2:3] * _dot(mix, w_ref[...])


def _odd_out(ys, yd, h, mod, w_out, ctx_len):
    B, L, d = h.shape
    tm = ROW_TILE
    half = ATT_HEADS * ATT_DH
    row = lambda n: pl.BlockSpec((1, tm, n), lambda b, i: (b, i, 0))
    return pl.pallas_call(
        _odd_out_kernel,
        grid=(B, L // tm),
        in_specs=[row(half), row(half), row(d), _mod_spec(ctx_len // tm),
                  pl.BlockSpec((2 * half, d), lambda b, i: (0, 0))],
        out_specs=row(d),
        out_shape=jax.ShapeDtypeStruct((B, L, d), F32),
        compiler_params=_cparams("parallel", "parallel"),
        name="odd_out",
    )(ys, yd, h, mod, w_out)


def _first_lane_of(cond, lane):
    return jnp.min(jnp.where(cond, lane, ROUTE_W), axis=-1, keepdims=True)


def _router_kernel(h_ref, mod_ref, nw_ref, wr_ref, a_ref, g_ref, c_ref):
    a = _norm_mod(h_ref[0], nw_ref[...], mod_ref[0, 0], 3, 4)
    a_ref[0] = a.astype(BF16)
    logits = jnp.dot(a, wr_ref[...], preferred_element_type=F32, precision=lax.Precision.HIGHEST)
    lane = lax.broadcasted_iota(jnp.int32, logits.shape, 1)
    gl = jnp.where(lane < N_GROUPS, logits, NEG_INF)
    gmax = jnp.max(gl, axis=-1, keepdims=True)
    p_top = 1.0 / jnp.sum(jnp.exp(gl - gmax), axis=-1, keepdims=True)
    g_idx = _first_lane_of(gl == gmax, lane)
    in_group = jnp.logical_and(lane >= EXPERT_LANE0, (lane - EXPERT_LANE0) // EXPERTS_PER_GROUP == g_idx)
    in_group = jnp.logical_and(in_group, lane < EXPERT_LANE0 + N_EXPERTS)
    el = jnp.where(in_group, logits, NEG_INF)
    m1 = jnp.max(el, axis=-1, keepdims=True)
    i1 = _first_lane_of(el == m1, lane)
    el2 = jnp.where(lane == i1, NEG_INF, el)
    m2 = jnp.max(el2, axis=-1, keepdims=True)
    i2 = _first_lane_of(el2 == m2, lane)
    e2 = jnp.exp(m2 - m1)
    w1 = 1.0 / (1.0 + e2) * p_top
    w2 = e2 / (1.0 + e2) * p_top
    j1 = i1 - EXPERT_LANE0 - EXPERTS_PER_GROUP * g_idx
    j2 = i2 - EXPERT_LANE0 - EXPERTS_PER_GROUP * g_idx
    g_ref[0] = (jnp.where(lane == j1, w1, 0.0) + jnp.where(lane == j2, w2, 0.0)
                + jnp.where(lane == GIDX_LANE, g_idx.astype(F32), 0.0))
    counts = jnp.sum((lane == g_idx).astype(jnp.int32), axis=0, keepdims=True)
    c_ref[0, 0] = jnp.broadcast_to(counts, (8, ROUTE_W))


def _router(h, mod, nw, w_route, ctx_len):
    B, L, d = h.shape
    tm = ROW_TILE
    row = lambda n: pl.BlockSpec((1, tm, n), lambda b, i: (b, i, 0))
    return pl.pallas_call(
        _router_kernel,
        grid=(B, L // tm),
        in_specs=[row(d), _mod_spec(ctx_len // tm), pl.BlockSpec((1, d), lambda b, i: (0, 0)),
                  pl.BlockSpec((d, ROUTE_W), lambda b, i: (0, 0))],
        out_specs=[row(d), row(ROUTE_W), pl.BlockSpec((1, 1, 8, ROUTE_W), lambda b, i: (b, i, 0, 0))],
        out_shape=[jax.ShapeDtypeStruct((B, L, d), BF16), jax.ShapeDtypeStruct((B, L, ROUTE_W), F32),
                   jax.ShapeDtypeStruct((B, L // tm, 8, ROUTE_W), jnp.int32)],
        compiler_params=_cparams("parallel", "parallel"),
        name="moe_router",
    )(h, mod, nw.reshape(1, d), w_route)


def _route_plan(counts, n_row_tiles):
    pc = (counts + RUN_ALIGN - 1) // RUN_ALIGN * RUN_ALIGN
    region = (jnp.sum(pc, axis=0) + EXPERT_TILE - 1) // EXPERT_TILE * EXPERT_TILE
    region_end = jnp.cumsum(region)
    off = (region_end - region)[None, :] + jnp.cumsum(pc, axis=0) - pc
    n_used = (region_end[-1] // EXPERT_TILE).reshape(1)
    tile_row0 = jnp.arange(n_row_tiles, dtype=jnp.int32) * EXPERT_TILE
    tile_group = jnp.minimum(jnp.sum(tile_row0[:, None] >= region_end[None, :], axis=1), N_GROUPS - 1)
    return (pc.reshape(-1).astype(jnp.int32), off.reshape(-1).astype(jnp.int32),
            tile_group.astype(jnp.int32), n_used.astype(jnp.int32))


def _tile_slots(G, pc_ref, t):
    tm = G.shape[0]
    lane = lax.broadcasted_iota(jnp.int32, (tm, ROUTE_W), 1)
    g_idx = G[:, GIDX_LANE:GIDX_LANE + 1].astype(jnp.int32)
    onehot = (lane == g_idx).astype(BF16)
    ti = lax.broadcasted_iota(jnp.int32, (tm, tm), 0)
    tj = lax.broadcasted_iota(jnp.int32, (tm, tm), 1)
    earlier = _dot((tj < ti).astype(BF16), onehot)
    slot = jnp.sum(jnp.where(lane == g_idx, earlier, 0.0), axis=-1, keepdims=True).astype(jnp.int32)
    bases, base = [], 0
    for g in range(N_GROUPS):
        bases.append(base)
        slot = slot + jnp.where(g_idx == g, base, 0)
        base = base + pc_ref[t * N_GROUPS + g]
    return slot, bases


def _run_copies(pc_ref, off_ref, t, bases, pairs, sems, to_sorted):
    out = []
    for g in range(N_GROUPS):
        n = pc_ref[t * N_GROUPS + g]
        dst0 = off_ref[t * N_GROUPS + g]
        for k, size in enumerate(RUN_PIECES):
            po = n & ~(2 * size - 1)
            present = (n & size) != 0
            v0 = pl.multiple_of(bases[g] + po, RUN_ALIGN)
            h0 = pl.multiple_of(dst0 + po, RUN_ALIGN)
            for a, (v_ref, h_ref) in enumerate(pairs):
                v, hb = v_ref.at[pl.ds(v0, size)], h_ref.at[pl.ds(h0, size)]
                src, dst = (v, hb) if to_sorted else (hb, v)
                out.append((present, pltpu.make_async_copy(src, dst, sems.at[a, g, k])))
    return out


def _dispatch_kernel(pc_ref, off_ref, a_ref, g_ref, xs_in, gs_in, xs_ref, gs_ref, sx_ref, sg_ref, sems):
    del xs_in, gs_in
    t = pl.program_id(0) * pl.num_programs(1) + pl.program_id(1)
    G = g_ref[0]
    slot, bases = _tile_slots(G, pc_ref, t)
    tm = G.shape[0]
    lane = lax.broadcasted_iota(jnp.int32, (tm, SORT_ROWS), 1)
    perm = (lane == slot).astype(F32).T.astype(BF16)
    sx_ref[...] = _dot(perm, a_ref[0]).astype(BF16)
    sg_ref[...] = _dot3_rhs(perm, G)
    copies = _run_copies(pc_ref, off_ref, t, bases, [(sx_ref, xs_ref), (sg_ref, gs_ref)], sems, True)
    for present, cp in copies:
        pl.when(present)(cp.start)
    for present, cp in copies:
        pl.when(present)(cp.wait)


def _dispatch(a2, gates, pc, off, n_rows):
    B, L, d = a2.shape
    tm = ROW_TILE
    row = lambda n: pl.BlockSpec((1, tm, n), lambda b, i, *_: (b, i, 0))
    anyspec = pl.BlockSpec(memory_space=pl.ANY)
    return pl.pallas_call(
        _dispatch_kernel,
        grid_spec=pltpu.PrefetchScalarGridSpec(
            num_scalar_prefetch=2, grid=(B, L // tm),
            in_specs=[row(d), row(ROUTE_W), anyspec, anyspec],
            out_specs=[anyspec, anyspec],
            scratch_shapes=[pltpu.VMEM((SORT_ROWS, d), BF16), pltpu.VMEM((SORT_ROWS, ROUTE_W), F32),
                            pltpu.SemaphoreType.DMA((2, N_GROUPS, len(RUN_PIECES)))]),
        out_shape=[jax.ShapeDtypeStruct((n_rows, d), BF16), jax.ShapeDtypeStruct((n_rows, ROUTE_W), F32)],
        input_output_aliases={4: 0, 5: 1},
        compiler_params=_cparams("arbitrary", "arbitrary"),
        name="moe_dispatch",
    )(pc, off, a2, gates, jnp.zeros((n_rows, d), BF16), jnp.zeros((n_rows, ROUTE_W), F32))


def _group_experts_kernel(tg_ref, nu_ref, x_ref, g_ref, wg_ref, wu_ref, wd_ref, y_ref):
    del tg_ref
    used = pl.program_id(0) < nu_ref[0]

    @pl.when(jnp.logical_not(used))
    def _():
        y_ref[...] = jnp.zeros_like(y_ref)

    @pl.when(used)
    def _():
        x = x_ref[...]
        hid = _silu(_dot(x, wg_ref[0])) * _dot(x, wu_ref[0])
        gates = g_ref[...]
        hid16 = jnp.concatenate(
            [(hid[:, e * EXPERT_FF:(e + 1) * EXPERT_FF] * gates[:, e:e + 1]).astype(BF16)
             for e in range(EXPERTS_PER_GROUP)], axis=-1)
        y_ref[...] = _dot(hid16, wd_ref[0])


def _group_experts(xs, gs, tile_group, n_used, wg, wu, wd):
    n_rows, d = xs.shape
    tm = EXPERT_TILE
    ff = EXPERTS_PER_GROUP * EXPERT_FF
    row = lambda n: pl.BlockSpec((tm, n), lambda i, tg, nu: (jnp.minimum(i, nu[0] - 1), 0))
    wspec = lambda r, c: pl.BlockSpec((1, r, c), lambda i, tg, nu: (tg[jnp.minimum(i, nu[0] - 1)], 0, 0))
    return pl.pallas_call(
        _group_experts_kernel,
        grid_spec=pltpu.PrefetchScalarGridSpec(
            num_scalar_prefetch=2, grid=(n_rows // tm,),
            in_specs=[row(d), row(ROUTE_W), wspec(d, ff), wspec(d, ff), wspec(ff, d)],
            out_specs=pl.BlockSpec((tm, d), lambda i, tg, nu: (i, 0))),
        out_shape=jax.ShapeDtypeStruct((n_rows, d), F32),
        compiler_params=_cparams("arbitrary"),
        name="moe_experts",
    )(tile_group, n_used, xs, gs, wg, wu, wd)


def _combine_kernel(pc_ref, off_ref, g_ref, h_ref, mod_ref, ys_ref, o_ref, sy_ref, sems):
    first = jnp.logical_and(pl.program_id(0) == 0, pl.program_id(1) == 0)

    @pl.when(first)
    def _():
        sy_ref[...] = jnp.zeros_like(sy_ref)

    t = pl.program_id(0) * pl.num_programs(1) + pl.program_id(1)
    G = g_ref[0]
    slot, bases = _tile_slots(G, pc_ref, t)
    copies = _run_copies(pc_ref, off_ref, t, bases, [(sy_ref, ys_ref)], sems, False)
    for present, cp in copies:
        pl.when(present)(cp.start)
    tm = G.shape[0]
    lane = lax.broadcasted_iota(jnp.int32, (tm, SORT_ROWS), 1)
    pick = (lane == slot).astype(BF16)
    for present, cp in copies:
        pl.when(present)(cp.wait)
    ys = sy_ref[...]
    hi = ys.astype(BF16)
    lo = (ys - hi.astype(F32)).astype(BF16)
    o_ref[0] = h_ref[0] + mod_ref[0, 0][5:6] * (_dot(pick, hi) + _dot(pick, lo))


def _combine(ys, gates, h, mod, pc, off, ctx_len):
    B, L, d = h.shape
    tm = ROW_TILE
    nct = ctx_len // tm
    row = lambda n: pl.BlockSpec((1, tm, n), lambda b, i, *_: (b, i, 0))
    return pl.pallas_call(
        _combine_kernel,
        grid_spec=pltpu.PrefetchScalarGridSpec(
            num_scalar_prefetch=2, grid=(B, L // tm),
            in_specs=[row(ROUTE_W), row(d),
                      pl.BlockSpec((1, 1, 6, d), lambda b, i, *_: (b, (i >= nct).astype(jnp.int32), 0, 0)),
                      pl.BlockSpec(memory_space=pl.ANY)],
            out_specs=row(d),
            scratch_shapes=[pltpu.VMEM((SORT_ROWS, d), F32),
                            pltpu.SemaphoreType.DMA((1, N_GROUPS, len(RUN_PIECES)))]),
        out_shape=jax.ShapeDtypeStruct((B, L, d), F32),
        compiler_params=_cparams("arbitrary", "arbitrary"),
        name="moe_combine",
    )(pc, off, gates, h, mod, ys)


def _final_norm_kernel(h_ref, nw_ref, o_ref):
    x = h_ref[0]
    o_ref[0] = x * lax.rsqrt(jnp.mean(x * x, axis=-1, keepdims=True) + EPS) * nw_ref[...]


def _final_norm(h, nw, ctx_len):
    B, L, d = h.shape
    tm = ROW_TILE
    nct = ctx_len // tm
    return pl.pallas_call(
        _final_norm_kernel,
        grid=(B, (L - ctx_len) // tm),
        in_specs=[pl.BlockSpec((1, tm, d), lambda b, i: (b, i + nct, 0)),
                  pl.BlockSpec((1, d), lambda b, i: (0, 0))],
        out_specs=pl.BlockSpec((1, tm, d), lambda b, i: (b, i, 0)),
        out_shape=jax.ShapeDtypeStruct((B, L - ctx_len, d), F32),
        compiler_params=_cparams("parallel", "parallel"),
        name="final_norm",
    )(h, nw.reshape(1, d))


def _rope_tables(seq):
    rows = seq // GRID_W
    row = jnp.repeat(jnp.arange(rows), GRID_W).astype(F32)
    col = (jnp.arange(rows * GRID_W) % GRID_W).astype(F32)
    axis_dim = ATT_DH // 2
    inv = ROPE_THETA ** (-jnp.arange(0, axis_dim, 2, dtype=F32) / axis_dim)
    ang = jnp.concatenate([row[:, None] * inv, col[:, None] * inv], axis=-1)
    cosf = jnp.repeat(jnp.cos(ang), 2, axis=-1)
    sinf = jnp.repeat(jnp.sin(ang), 2, axis=-1) * jnp.tile(jnp.array([-1.0, 1.0], F32), ATT_DH // 2)
    return jnp.tile(cosf, (1, 2)), jnp.tile(sinf, (1, 2))


def kernel(x, c, ctx, c_ctx, ada_w, ada_b, norm1_w, norm2_w, ev_w_in, ev_conv_w, ev_conv_b, ev_dt_bias, ev_a_log, ev_d_skip, ev_ssd_norm_w, ev_ig_bias, ev_fg_bias, ev_mlstm_norm_w, ev_w_out, od_w_in, od_sink, od_q_norm_w, od_k_norm_w, od_w_out, moe_w_group, moe_w_expert, moe_w_gate, moe_w_up, moe_w_down, final_norm_w):
    B, S, d = x.shape
    T = ctx.shape[1]
    depth = ada_w.shape[0]
    assert d == D_MODEL and T % ROW_TILE == 0 and S % ROW_TILE == 0 and S % GRID_W == 0 and S >= 3 * QBLK

    h = jnp.concatenate([ctx, x], axis=1)

    rows = -(-(B + 1) // 8) * 8
    cond = jnp.zeros((rows, d), F32).at[:B].set(c).at[B].set(c_ctx)
    ada = _adaln(cond, ada_w, ada_b).reshape(depth, rows, 6, d)
    mods = jnp.stack([jnp.broadcast_to(ada[:, B:B + 1], (depth, B, 6, d)), ada[:, :B]], axis=2)

    cosf, sinf = _rope_tables(S)

    for layer in range(depth):
        li = layer // 2
        mod = mods[layer]
        if layer % 2 == 0:
            w_in = ev_w_in[li]
            conv, z, v, o, gate_cols = (w_in[:, :CONV_CH], w_in[:, CONV_CH:CONV_CH + SSD_INNER],
                                        w_in[:, CONV_CH + SSD_INNER:CONV_CH + 2 * SSD_INNER],
                                        w_in[:, CONV_CH + 2 * SSD_INNER:CONV_CH + 3 * SSD_INNER],
                                        w_in[:, CONV_CH + 3 * SSD_INNER:])
            w_rec = jnp.concatenate([v, conv], axis=1).astype(BF16)
            w_zo = jnp.concatenate([z, o], axis=1).astype(BF16)
            w_g = jnp.pad(gate_cols, ((0, 0), (0, GATE_W - N_GATE))).astype(BF16)
            rec, zo, gates = _norm_mod_matmul(h, mod, norm1_w[layer], [w_rec, w_zo, w_g], [BF16, BF16, F32], 0, 1, T)
            gates_t = jnp.swapaxes(gates[:, :, :N_GATE], 1, 2)
            bias = jnp.concatenate([ev_dt_bias[li].reshape(-1), ev_ig_bias[li].reshape(-1), ev_fg_bias[li].reshape(-1)])
            alog = jnp.pad(ev_a_log[li].reshape(-1), (0, N_GATE - 2 * SSD_HEADS))
            prow = jnp.pad(jnp.stack([bias, alog]), ((0, 0), (0, GATE_W - N_GATE)))
            pcol = jnp.stack([bias, alog], axis=1)
            dskip = jnp.repeat(ev_d_skip[li], SSD_HEAD_DIM).reshape(1, SSD_INNER)
            cb = ev_conv_b[li].reshape(1, CONV_CH)
            yf = _mixer_scan(rec, gates, gates_t, ev_conv_w[li], cb, prow, pcol, dskip, T, False)
            yb = _mixer_scan(rec, gates, gates_t, ev_conv_w[li], cb, prow, pcol, dskip, T, True)
            h = _even_out(yf, yb, zo, h, mod, ev_ssd_norm_w[li], ev_mlstm_norm_w[li], ev_w_out[li].astype(BF16), T)
        else:
            (proj,) = _norm_mod_matmul(h, mod, norm1_w[layer], [od_w_in[li].astype(BF16)], [BF16], 0, 1, T)
            qnw = jnp.tile(od_q_norm_w[li], ATT_HEADS).reshape(1, -1)
            knw = jnp.tile(od_k_norm_w[li], ATT_KV).reshape(1, -1)
            qs, ks, vsl, vsr, qd, kd, vdl, vdr = _qkv_prep(proj, cosf, sinf, qnw, knw, T)
            sink_b = jnp.broadcast_to(od_sink[li].reshape(ATT_HEADS, 1), (ATT_HEADS, 128))
            ys_lat = _window_attn(qs, ks, vsl, vsr, sink_b, T)
            yd_lat = _dense_attn(qd, kd, vdl, vdr, T)
            ys_ctx, yd_ctx = _ctx_attn(qs, ks, vsl, vsr, qd, kd, vdl, vdr, sink_b, T)
            ys = jnp.concatenate([ys_ctx, ys_lat], axis=1)
            yd = jnp.concatenate([yd_ctx, yd_lat], axis=1)
            h = _odd_out(ys, yd, h, mod, od_w_out[li].astype(BF16), T)

        w_route = jnp.pad(jnp.concatenate([moe_w_group[layer], moe_w_expert[layer]], axis=1),
                          ((0, 0), (0, ROUTE_W - N_GROUPS - N_EXPERTS)))
        a2, gates, counts = _router(h, mod, norm2_w[layer], w_route, T)
        n_tiles = B * ((T + S) // ROW_TILE)
        n_rows = -(-(n_tiles * (ROW_TILE + N_GROUPS * RUN_ALIGN) + N_GROUPS * EXPERT_TILE) // EXPERT_TILE) * EXPERT_TILE
        pc, off, tile_group, n_used = _route_plan(counts[:, :, 0, :N_GROUPS].reshape(n_tiles, N_GROUPS),
                                                  n_rows // EXPERT_TILE)
        xs, gs = _dispatch(a2, gates, pc, off, n_rows)
        ff = EXPERTS_PER_GROUP * EXPERT_FF
        wg = moe_w_gate[layer].transpose(0, 2, 1, 3).reshape(N_GROUPS, d, ff).astype(BF16)
        wu = moe_w_up[layer].transpose(0, 2, 1, 3).reshape(N_GROUPS, d, ff).astype(BF16)
        wd = moe_w_down[layer].reshape(N_GROUPS, ff, d).astype(BF16)
        ys = _group_experts(xs, gs, tile_group, n_used, wg, wu, wd)
        h = _combine(ys, gates, h, mod, pc, off, T)

    return _final_norm(h, final_norm_w, T)
```

```python
import functools
import math

import jax
import jax.numpy as jnp
from jax import lax
from jax.experimental import pallas as pl
from jax.experimental.pallas import tpu as pltpu

F32 = jnp.float32
BF16 = jnp.bfloat16
EPS = 1e-6
NEG_INF = float("-inf")
LOG2E = 1.4426950408889634

D_MODEL = 1024
GRID_W = 64
ROPE_THETA = 10000.0

SSD_HEADS = 16
SSD_HEAD_DIM = 64
SSD_INNER = 1024
SSD_GROUPS = 2
SSD_STATE = 128
MLSTM_HEADS = 4
MLSTM_QK_DIM = 128
MLSTM_V_DIM = 256
MLSTM_QK = 512
MLSTM_INNER = 1024
CHUNK = 128
MIXER_BATCH = 2
HALO = 16
CONV_CH = 2560
REC_W = MLSTM_INNER + CONV_CH
N_GATE = 48
GATE_W = 128
COL_DT, COL_IG, COL_FG = 0, 32, 40

ATT_DH = 64
ATT_HEADS = 8
ATT_KV = 2
ATT_G = ATT_HEADS // ATT_KV
WINDOW = 128
QBLK = 128
ODD_PROJ = 1536

N_GROUPS = 4
EXPERTS_PER_GROUP = 4
N_EXPERTS = N_GROUPS * EXPERTS_PER_GROUP
EXPERT_FF = 256
ROUTE_W = 128
EXPERT_LANE0 = N_GROUPS

ROW_TILE = 256
GIDX_LANE = 8
RUN_ALIGN = 16
RUN_PIECES = (256, 128, 64, 32, 16)
SORT_ROWS = -(-(ROW_TILE + N_GROUPS * (RUN_ALIGN - 1)) // 128) * 128
EXPERT_TILE = 512
VMEM_LIMIT = 56 * 1024 * 1024


def _cparams(*sem):
    return pltpu.CompilerParams(dimension_semantics=sem, vmem_limit_bytes=VMEM_LIMIT)


def _sigmoid(x):
    return 1.0 / (1.0 + jnp.exp(-x))


def _silu(x):
    return x * _sigmoid(x)


def _softplus(x):
    return jnp.maximum(x, 0.0) + jnp.log1p(jnp.exp(-jnp.abs(x)))


def _dot(a, b):
    return jnp.dot(a, b, preferred_element_type=F32)


def _dot_nt(a, b):
    return lax.dot_general(a, b, (((1,), (1,)), ((), ())), preferred_element_type=F32)


def _split3(a):
    hi = a.astype(BF16)
    r1 = a - hi.astype(F32)
    mid = r1.astype(BF16)
    lo = (r1 - mid.astype(F32)).astype(BF16)
    return hi, mid, lo


def _dot3_rhs(exact, a):
    hi, mid, lo = _split3(a)
    return _dot(exact, hi) + _dot(exact, mid) + _dot(exact, lo)


def _dot3_lhs(a, exact):
    hi, mid, lo = _split3(a)
    return _dot(hi, exact) + _dot(mid, exact) + _dot(lo, exact)


def _adaln_kernel(cond_ref, w_ref, b_ref, o_ref):
    s = _silu(cond_ref[...])
    o_ref[0] = _dot(s.astype(BF16), w_ref[0].astype(BF16)) + b_ref[0]


def _adaln(cond, ada_w, ada_b):
    depth, d, n = ada_w.shape
    rows = cond.shape[0]
    tn = 1536
    return pl.pallas_call(
        _adaln_kernel,
        grid=(depth, n // tn),
        in_specs=[pl.BlockSpec((rows, d), lambda l, j: (0, 0)),
                  pl.BlockSpec((1, d, tn), lambda l, j: (l, 0, j)),
                  pl.BlockSpec((1, 1, tn), lambda l, j: (l, 0, j))],
        out_specs=pl.BlockSpec((1, rows, tn), lambda l, j: (l, 0, j)),
        out_shape=jax.ShapeDtypeStruct((depth, rows, n), F32),
        compiler_params=_cparams("parallel", "parallel"),
        name="adaln",
    )(cond, ada_w, ada_b.reshape(depth, 1, n))


def _norm_mod(x, nw, mod, shift_row, scale_row):
    var = jnp.mean(x * x, axis=-1, keepdims=True)
    y = x * lax.rsqrt(var + EPS) * nw
    return y * (1.0 + mod[scale_row:scale_row + 1]) + mod[shift_row:shift_row + 1]


def _nmm_kernel(h_ref, mod_ref, nw_ref, *rest, n_out, shift_row, scale_row, tn):
    w_refs, o_refs = rest[:n_out], rest[n_out:]
    a16 = _norm_mod(h_ref[0], nw_ref[...], mod_ref[0, 0], shift_row, scale_row).astype(BF16)
    for w_ref, o_ref in zip(w_refs, o_refs):
        n = w_ref.shape[1]
        for j in range(0, n, tn):
            w = min(tn, n - j)
            o_ref[0, :, j:j + w] = _dot(a16, w_ref[:, j:j + w]).astype(o_ref.dtype)


def _mod_spec(nct_tiles):
    return pl.BlockSpec((1, 1, 6, D_MODEL), lambda b, i: (b, (i >= nct_tiles).astype(jnp.int32), 0, 0))


def _norm_mod_matmul(h, mod, nw, weights, out_dtypes, shift_row, scale_row, ctx_len):
    B, L, d = h.shape
    tm = ROW_TILE
    in_specs = [pl.BlockSpec((1, tm, d), lambda b, i: (b, i, 0)),
                _mod_spec(ctx_len // tm),
                pl.BlockSpec((1, d), lambda b, i: (0, 0))]
    out_specs, out_shape = [], []
    for w, dt in zip(weights, out_dtypes):
        n = w.shape[1]
        in_specs.append(pl.BlockSpec((d, n), lambda b, i: (0, 0)))
        out_specs.append(pl.BlockSpec((1, tm, n), lambda b, i: (b, i, 0)))
        out_shape.append(jax.ShapeDtypeStruct((B, L, n), dt))
    return pl.pallas_call(
        functools.partial(_nmm_kernel, n_out=len(weights), shift_row=shift_row, scale_row=scale_row, tn=512),
        grid=(B, L // tm),
        in_specs=in_specs, out_specs=out_specs, out_shape=out_shape,
        compiler_params=_cparams("parallel", "parallel"),
        name="norm_mod_matmul",
    )(h, mod, nw.reshape(1, d), *weights)


def _mixer_kernel(rec_ref, prev_ref, next_ref, g_ref, gt_ref, cw_ref, cb_ref, prow_ref, pcol_ref, extra_ref,
                  y_ref, conv_ref, S_ref, C_ref, n_ref, m_ref, *, reverse, nct, nc, nb):
    i = pl.program_id(1)
    c = jnp.where(i < nct, nct - 1 - i, nc - 1 - (i - nct)) if reverse else i
    seq_start = jnp.logical_or(c == 0, c == nct)
    seq_end = jnp.logical_or(c == nct - 1, c == nc - 1)

    @pl.when(i == 0)
    def _():
        S_ref[...] = jnp.zeros_like(S_ref)
        C_ref[...] = jnp.zeros_like(C_ref)
        n_ref[...] = jnp.zeros_like(n_ref)
        m_ref[...] = jnp.zeros_like(m_ref)

    for bb in range(nb):
        xc = rec_ref[bb, :, MLSTM_INNER:].astype(F32)
        before = prev_ref[bb].astype(F32)[HALO - 1:HALO, MLSTM_INNER:]
        after = next_ref[bb].astype(F32)[0:1, MLSTM_INNER:]
        before = jnp.where(seq_start, 0.0, before)
        after = jnp.where(seq_end, 0.0, after)
        rid = lax.broadcasted_iota(jnp.int32, (CHUNK, 1), 0)
        x_prev = jnp.where(rid == 0, before, pltpu.roll(xc, 1, 0))
        x_next = jnp.where(rid == CHUNK - 1, after, pltpu.roll(xc, CHUNK - 1, 0))
        cw = cw_ref[...]
        conv = conv_ref.at[bb]
        conv[...] = _silu(cw[0:1] * x_prev + cw[1:2] * xc + cw[2:3] * x_next + cb_ref[...])

        if reverse:
            def emit(col0, width, val, bb=bb):
                y_ref[bb, :, col0:col0 + width] = (extra_ref[bb, :, col0:col0 + width] + val).astype(y_ref.dtype)
        else:
            def emit(col0, width, val, bb=bb, conv=conv):
                if col0 < SSD_INNER:
                    val = val + conv[:, col0:col0 + width] * extra_ref[:, col0:col0 + width]
                y_ref[bb, :, col0:col0 + width] = val

        _mixer_chunk(rec_ref.at[bb], g_ref.at[bb], gt_ref.at[bb], prow_ref, pcol_ref, conv,
                     S_ref.at[bb], C_ref.at[bb], n_ref.at[bb], m_ref.at[bb], emit, reverse=reverse)


def _mixer_chunk(v_ref, g_ref, gt_ref, prow_ref, pcol_ref, conv_ref, S_ref, C_ref, n_ref, m_ref, emit, *, reverse):
    d = 1 if reverse else 0
    li = lax.broadcasted_iota(jnp.int32, (CHUNK, CHUNK), 0)
    si = lax.broadcasted_iota(jnp.int32, (CHUNK, CHUNK), 1)
    mask = (si >= li) if reverse else (si <= li)
    tri = mask.astype(BF16)
    tri_t = ((li >= si) if reverse else (li <= si)).astype(BF16)
    last = 0 if reverse else CHUNK - 1

    G = g_ref[...] + prow_ref[0:1]
    GT = gt_ref[...] + pcol_ref[:, 0:1]
    lane = lax.broadcasted_iota(jnp.int32, (CHUNK, GATE_W), 1)
    row = lax.broadcasted_iota(jnp.int32, (N_GATE, CHUNK), 0)
    dt_c = _softplus(jnp.where(lane < COL_IG, G, -G))
    dt_r = _softplus(jnp.where(row < COL_IG, GT, -GT))
    nega_c = -jnp.exp(prow_ref[1:2])
    nega_r = -jnp.exp(pcol_ref[:, 1:2])
    pre_c = jnp.where(lane < COL_IG, dt_c * nega_c, jnp.where(lane >= COL_FG, -dt_c, 0.0))
    pre_c = jnp.where(lane < N_GATE, pre_c, 0.0)
    pre_r = jnp.where(row < COL_IG, dt_r * nega_r, jnp.where(row >= COL_FG, -dt_r, 0.0))
    cum_c = _dot3_rhs(tri, pre_c)
    cum_r = _dot3_lhs(pre_r, tri_t)

    a_c = cum_c[:, d * SSD_HEADS:(d + 1) * SSD_HEADS]
    a_r = cum_r[d * SSD_HEADS:(d + 1) * SSD_HEADS, :]
    a_last = a_c[last:last + 1, :]
    hh = lax.broadcasted_iota(jnp.int32, (SSD_HEADS, SSD_INNER), 0)
    hj = lax.broadcasted_iota(jnp.int32, (SSD_HEADS, SSD_INNER), 1)
    expand = (hj // SSD_HEAD_DIM == hh).astype(BF16)
    dt_x = _dot3_lhs(dt_c[:, d * SSD_HEADS:(d + 1) * SSD_HEADS], expand)
    dec_x = _dot3_lhs(jnp.exp(a_last - a_c), expand)
    ein_x = _dot3_lhs(jnp.exp(a_c), expand)
    cdec_x = _dot3_lhs(jnp.broadcast_to(jnp.exp(a_last), (8, SSD_HEADS)), expand)[0:1]

    xsdt = conv_ref[:, 0:SSD_INNER] * dt_x
    xsdt16 = xsdt.astype(BF16)
    xdec16 = (xsdt * dec_x).astype(BF16)
    lane128 = lax.broadcasted_iota(jnp.int32, (CHUNK, 128), 1)
    heads_per_group = SSD_HEADS // SSD_GROUPS
    gw = heads_per_group * SSD_HEAD_DIM
    for g in range(SSD_GROUPS):
        Bg = conv_ref[:, SSD_INNER + g * SSD_STATE:SSD_INNER + (g + 1) * SSD_STATE]
        Cg16 = conv_ref[:, SSD_INNER + 256 + g * SSD_STATE:SSD_INNER + 256 + (g + 1) * SSD_STATE].astype(BF16)
        CB = _dot_nt(Cg16, Bg.astype(BF16))
        S_g = S_ref[:, g * gw:(g + 1) * gw]
        y_off = _dot(Cg16, S_g.astype(BF16)) * ein_x[:, g * gw:(g + 1) * gw]
        for pr in range(heads_per_group // 2):
            col0 = g * gw + pr * 128
            xpair = xsdt16[:, col0:col0 + 128]
            acc = None
            for half in range(2):
                h = g * heads_per_group + pr * 2 + half
                seg = a_c[:, h:h + 1] - a_r[h:h + 1, :]
                Lm = jnp.exp(jnp.where(mask, seg, NEG_INF))
                Mh = (CB * Lm).astype(BF16)
                keep = (lane128 < 64) if half == 0 else (lane128 >= 64)
                part = _dot(Mh, jnp.where(keep, xpair, jnp.zeros_like(xpair)))
                acc = part if acc is None else acc + part
            emit(col0, 128, acc + y_off[:, pr * 128:(pr + 1) * 128])
        S_ref[:, g * gw:(g + 1) * gw] = (cdec_x[:, g * gw:(g + 1) * gw] * S_g
                                         + _dot(Bg.T.astype(BF16), xdec16[:, g * gw:(g + 1) * gw]))

    for h in range(MLSTM_HEADS):
        gi = COL_IG + d * MLSTM_HEADS + h
        gf = COL_FG + d * MLSTM_HEADS + h
        qh16 = conv_ref[:, 1536 + h * 128:1536 + (h + 1) * 128].astype(BF16)
        kh = conv_ref[:, 2048 + h * 128:2048 + (h + 1) * 128] * (MLSTM_QK_DIM ** -0.5)
        kh16 = kh.astype(BF16)
        vh16 = v_ref[:, h * MLSTM_V_DIM:(h + 1) * MLSTM_V_DIM]
        ig_c, ig_r = G[:, gi:gi + 1], GT[gi:gi + 1, :]
        b_c, b_r = cum_c[:, gf:gf + 1], cum_r[gf:gf + 1, :]
        b_last = b_c[last:last + 1, :]
        Dlog = jnp.where(mask, b_c - b_r + ig_r, NEG_INF)
        m_loc = jnp.max(b_last - b_r + ig_r, axis=-1, keepdims=True)
        ek = jnp.exp(b_last - b_c + ig_c - m_loc) * kh
        C_loc = _dot(ek.T.astype(BF16), vh16)
        n_loc = jnp.sum(ek, axis=0, keepdims=True)
        C_in = C_ref[h]
        n_in = n_ref[h:h + 1, :]
        m_in = m_ref[h:h + 1, 0:1]
        m_inter = b_c + m_in
        m_t = jnp.maximum(m_inter, jnp.max(Dlog, axis=-1, keepdims=True))
        P = jnp.exp(Dlog - m_t) * _dot_nt(qh16, kh16)
        w_inter = jnp.exp(m_inter - m_t)
        num = _dot(P.astype(BF16), vh16) + w_inter * _dot(qh16, C_in.astype(BF16))
        qn = jnp.sum(qh16.astype(F32) * n_in, axis=-1, keepdims=True)
        den = jnp.sum(P, axis=-1, keepdims=True) + w_inter * qn
        den = jnp.maximum(jnp.abs(den), jnp.exp(-m_t))
        emit(SSD_INNER + h * MLSTM_V_DIM, MLSTM_V_DIM, num / den)
        m_new = jnp.maximum(b_last + m_in, m_loc)
        fa = jnp.exp(b_last + m_in - m_new)
        fb = jnp.exp(m_loc - m_new)
        C_ref[h] = fa * C_in + fb * C_loc
        n_ref[h:h + 1, :] = fa * n_in + fb * n_loc
        m_ref[h:h + 1, :] = jnp.broadcast_to(m_new, (1, 128))


def _mixer_scan(rec, gates, gates_t, conv_w, conv_b, prow, pcol, extra, ctx_len, reverse):
    B, L, _ = rec.shape
    nc, nct = L // CHUNK, ctx_len // CHUNK
    nb = MIXER_BATCH
    hb = CHUNK // HALO
    wide = SSD_INNER + MLSTM_INNER

    def chunk_of(i):
        return jnp.where(i < nct, nct - 1 - i, nc - 1 - (i - nct)) if reverse else i

    const = lambda r, n: pl.BlockSpec((r, n), lambda b, i: (0, 0))
    chunk = lambda n: pl.BlockSpec((nb, CHUNK, n), lambda b, i: (b, chunk_of(i), 0))
    extra_spec = chunk(wide) if reverse else const(1, SSD_INNER)
    return pl.pallas_call(
        functools.partial(_mixer_kernel, reverse=reverse, nct=nct, nc=nc, nb=nb),
        grid=(B // nb, nc),
        in_specs=[chunk(REC_W),
                  pl.BlockSpec((nb, HALO, REC_W), lambda b, i: (b, jnp.maximum(chunk_of(i) * hb - 1, 0), 0)),
                  pl.BlockSpec((nb, HALO, REC_W),
                               lambda b, i: (b, jnp.minimum((chunk_of(i) + 1) * hb, L // HALO - 1), 0)),
                  chunk(GATE_W),
                  pl.BlockSpec((nb, N_GATE, CHUNK), lambda b, i: (b, 0, chunk_of(i))),
                  const(3, CONV_CH), const(1, CONV_CH), const(2, GATE_W), const(N_GATE, 2), extra_spec],
        out_specs=chunk(wide),
        out_shape=jax.ShapeDtypeStruct((B, L, wide), BF16 if reverse else F32),
        scratch_shapes=[pltpu.VMEM((nb, CHUNK, CONV_CH), F32),
                        pltpu.VMEM((nb, SSD_STATE, SSD_INNER), F32),
                        pltpu.VMEM((nb, MLSTM_HEADS, MLSTM_QK_DIM, MLSTM_V_DIM), F32),
                        pltpu.VMEM((nb, 8, MLSTM_QK_DIM), F32),
                        pltpu.VMEM((nb, 8, 128), F32)],
        compiler_params=_cparams("parallel", "arbitrary"),
        name="mixer_bwd" if reverse else "mixer_fwd",
    )(rec, rec, rec, gates, gates_t, conv_w, conv_b, prow, pcol, extra)


def _group_rms(x, groups):
    w = x.shape[-1] // groups
    parts = []
    for g in range(groups):
        seg = x[:, g * w:(g + 1) * w]
        parts.append(seg * lax.rsqrt(jnp.mean(seg * seg, axis=-1, keepdims=True) + EPS))
    return jnp.concatenate(parts, axis=-1)


def _even_out_kernel(y_ref, zo_ref, h_ref, mod_ref, snw_ref, mnw_ref, w_ref, o_ref):
    y = y_ref[0].astype(F32)
    z = zo_ref[0, :, 0:SSD_INNER].astype(F32)
    o = zo_ref[0, :, SSD_INNER:].astype(F32)
    ys = _group_rms(y[:, 0:SSD_INNER] * _silu(z), SSD_GROUPS) * snw_ref[...]
    hm = _group_rms(y[:, SSD_INNER:], MLSTM_HEADS) * mnw_ref[...] * _sigmoid(o)
    mix = jnp.concatenate([ys, hm], axis=-1).astype(BF16)
    o_ref[0] = h_ref[0] + mod_ref[0, 0][2:3] * _dot(mix, w_ref[...])


def _even_out(y, zo, h, mod, snw, mnw, w_out, ctx_len):
    B, L, d = h.shape
    tm = ROW_TILE
    wide = SSD_INNER + MLSTM_INNER
    row = lambda n: pl.BlockSpec((1, tm, n), lambda b, i: (b, i, 0))
    const = lambda r, n: pl.BlockSpec((r, n), lambda b, i: (0, 0))
    return pl.pallas_call(
        _even_out_kernel,
        grid=(B, L // tm),
        in_specs=[row(wide), row(wide), row(d), _mod_spec(ctx_len // tm),
                  const(1, SSD_INNER), const(1, MLSTM_INNER), const(wide, d)],
        out_specs=row(d),
        out_shape=jax.ShapeDtypeStruct((B, L, d), F32),
        compiler_params=_cparams("parallel", "parallel"),
        name="even_out",
    )(y, zo, h, mod, snw.reshape(1, -1), mnw.reshape(1, -1), w_out)


def _qkv_prep_kernel(p_ref, cs_ref, sn_ref, qnw_ref, knw_ref,
                     qs_ref, ks_ref, vsl_ref, vsr_ref, qd_ref, kd_ref, vd1_ref, *, nct):
    is_lat = pl.program_id(1) >= nct
    p = p_ref[0].astype(F32)
    cs, sn = cs_ref[...], sn_ref[...]

    def rope(x):
        w = x.shape[1]
        reps = w // 128
        c = jnp.concatenate([cs] * reps, axis=1) if reps > 1 else cs
        s = jnp.concatenate([sn] * reps, axis=1) if reps > 1 else sn
        lane = lax.broadcasted_iota(jnp.int32, x.shape, 1)
        partner = jnp.where(lane % 2 == 0, pltpu.roll(x, w - 1, 1), pltpu.roll(x, 1, 1))
        return jnp.where(is_lat, x * c + partner * s, x)

    def head_rms(x, nw):
        w = x.shape[1]
        a = lax.broadcasted_iota(jnp.int32, (w, w), 0) // ATT_DH
        b = lax.broadcasted_iota(jnp.int32, (w, w), 1) // ATT_DH
        same = (a == b).astype(BF16)
        ms = _dot3_lhs(x * x, same) * (1.0 / ATT_DH)
        return x * lax.rsqrt(ms + EPS) * nw

    qw, kw = ATT_HEADS * ATT_DH, ATT_KV * ATT_DH
    o = 0
    q_s = rope(p[:, o:o + qw]) * (ATT_DH ** -0.5); o += qw
    k_s = rope(p[:, o:o + kw]); o += kw
    v_s = p[:, o:o + kw]; o += kw
    q_d = rope(head_rms(p[:, o:o + qw], qnw_ref[...])) * (ATT_DH ** -0.5 * LOG2E); o += qw
    k_d = rope(head_rms(p[:, o:o + kw], knw_ref[...])); o += kw
    v_d = p[:, o:o + kw]

    lane = lax.broadcasted_iota(jnp.int32, (QBLK, 128), 1)
    for h in range(ATT_HEADS):
        qs_ref[0, h] = q_s[:, h * ATT_DH:(h + 1) * ATT_DH].astype(BF16)
        qd_ref[0, h] = q_d[:, h * ATT_DH:(h + 1) * ATT_DH].astype(BF16)
    for kv in range(ATT_KV):
        ks_ref[0, kv] = k_s[:, kv * ATT_DH:(kv + 1) * ATT_DH].astype(BF16)
        kd_ref[0, kv] = k_d[:, kv * ATT_DH:(kv + 1) * ATT_DH].astype(BF16)
    for kv in range(ATT_KV):
        own = (lane // ATT_DH) == kv
        same = jnp.where(own, v_s, 0.0)
        other = pltpu.roll(same, ATT_DH, 1)
        left, right = (same, other) if kv == 0 else (other, same)
        vsl_ref[0, kv] = left.astype(BF16)
        vsr_ref[0, kv] = right.astype(BF16)
        same = jnp.where(own, v_d, 0.0)
        low = same if kv == 0 else pltpu.roll(same, ATT_DH, 1)
        vd1_ref[0, kv] = jnp.where(lane == ATT_DH, 1.0, low).astype(BF16)


def _qkv_prep(proj, cosf, sinf, qnw, knw, ctx_len):
    B, L, _ = proj.shape
    nct = ctx_len // QBLK
    tab = pl.BlockSpec((QBLK, 128), lambda b, i: (jnp.maximum(i - nct, 0), 0))
    head = lambda n, w: pl.BlockSpec((1, n, QBLK, w), lambda b, i: (b, 0, i, 0))
    shp = lambda n, w: jax.ShapeDtypeStruct((B, n, L, w), BF16)
    return pl.pallas_call(
        functools.partial(_qkv_prep_kernel, nct=nct),
        grid=(B, L // QBLK),
        in_specs=[pl.BlockSpec((1, QBLK, ODD_PROJ), lambda b, i: (b, i, 0)), tab, tab,
                  pl.BlockSpec((1, ATT_HEADS * ATT_DH), lambda b, i: (0, 0)),
                  pl.BlockSpec((1, ATT_KV * ATT_DH), lambda b, i: (0, 0))],
        out_specs=[head(ATT_HEADS, ATT_DH), head(ATT_KV, ATT_DH), head(ATT_KV, 128), head(ATT_KV, 128),
                   head(ATT_HEADS, ATT_DH), head(ATT_KV, ATT_DH), head(ATT_KV, 128)],
        out_shape=[shp(ATT_HEADS, ATT_DH), shp(ATT_KV, ATT_DH), shp(ATT_KV, 128), shp(ATT_KV, 128),
                   shp(ATT_HEADS, ATT_DH), shp(ATT_KV, ATT_DH), shp(ATT_KV, 128)],
        compiler_params=_cparams("parallel", "parallel"),
        name="qkv_prep",
    )(proj, cosf, sinf, qnw, knw)


def _pair_out(P16, vl, vr, inv_l, rows):
    outs = []
    for pr in range(ATT_G // 2):
        a, b = 2 * pr, 2 * pr + 1
        oa = _dot(P16[a * rows:(a + 1) * rows], vl) * inv_l[a * rows:(a + 1) * rows]
        ob = _dot(P16[b * rows:(b + 1) * rows], vr) * inv_l[b * rows:(b + 1) * rows]
        outs.append(oa + ob)
    return jnp.concatenate(outs, axis=-1)


def _dense_group_out(Q, K, V1, rows):
    S = _dot_nt(Q, K)
    P16 = jnp.exp2(S - jnp.max(S, axis=-1, keepdims=True)).astype(BF16)
    lane = lax.broadcasted_iota(jnp.int32, (rows, 128), 1)
    outs = []
    for pr in range(ATT_G // 2):
        heads = []
        for g in (2 * pr, 2 * pr + 1):
            O = _dot(P16[g * rows:(g + 1) * rows], V1)
            heads.append(O * (1.0 / O[:, ATT_DH:ATT_DH + 1]))
        outs.append(jnp.where(lane < ATT_DH, heads[0], pltpu.roll(heads[1], ATT_DH, 1)))
    return jnp.concatenate(outs, axis=-1)


def _dense_attn_kernel(q_ref, k_ref, v1_ref, o_ref):
    for kv in range(ATT_KV):
        Q = q_ref[0, kv * ATT_G:(kv + 1) * ATT_G].reshape(ATT_G * QBLK, ATT_DH)
        o_ref[0, :, kv * ATT_G * ATT_DH:(kv + 1) * ATT_G * ATT_DH] = _dense_group_out(
            Q, k_ref[0, kv], v1_ref[0, kv], QBLK).astype(o_ref.dtype)


def _dense_attn(qd, kd, vd1, ctx_len):
    B, _, L, _ = qd.shape
    S = L - ctx_len
    nct = ctx_len // QBLK
    full = lambda w: pl.BlockSpec((1, ATT_KV, L, w), lambda b, j: (b, 0, 0, 0))
    return pl.pallas_call(
        _dense_attn_kernel,
        grid=(B, S // QBLK),
        in_specs=[pl.BlockSpec((1, ATT_HEADS, QBLK, ATT_DH), lambda b, j: (b, 0, j + nct, 0)),
                  full(ATT_DH), full(128)],
        out_specs=pl.BlockSpec((1, QBLK, ATT_HEADS * ATT_DH), lambda b, j: (b, j, 0)),
        out_shape=jax.ShapeDtypeStruct((B, S, ATT_HEADS * ATT_DH), BF16),
        compiler_params=_cparams("parallel", "parallel"),
        name="dense_attn",
    )(qd, kd, vd1)


def _sink_rows(sink_ref, kv, rows):
    return jnp.concatenate(
        [jnp.broadcast_to(sink_ref[kv * ATT_G + g:kv * ATT_G + g + 1, 0:1], (rows, 1)) for g in range(ATT_G)], axis=0)


def _window_attn_kernel(q_ref, k_ref, vl_ref, vr_ref, sink_ref, o_ref, *, ctx_len, seq):
    j = pl.program_id(1)
    span = 3 * QBLK
    start = jnp.clip((j - 1) * QBLK, 0, seq - span)
    kstart = pl.multiple_of(ctx_len + start, QBLK)
    rows = ATT_G * QBLK
    qpos = j * QBLK + lax.broadcasted_iota(jnp.int32, (rows, span), 0) % QBLK
    kpos = start + lax.broadcasted_iota(jnp.int32, (rows, span), 1)
    valid = jnp.abs(kpos - qpos) <= WINDOW
    for kv in range(ATT_KV):
        Q = q_ref[0, kv * ATT_G:(kv + 1) * ATT_G].reshape(rows, ATT_DH)
        Sc = _dot_nt(Q, k_ref[0, kv, 0:ctx_len])
        Sl = jnp.where(valid, _dot_nt(Q, k_ref[0, kv, pl.ds(kstart, span)]), NEG_INF)
        sk = _sink_rows(sink_ref, kv, QBLK)
        m = jnp.maximum(jnp.maximum(jnp.max(Sc, axis=-1, keepdims=True), jnp.max(Sl, axis=-1, keepdims=True)), sk)
        Pc = jnp.exp(Sc - m)
        Pl = jnp.exp(Sl - m)
        inv_l = 1.0 / (jnp.exp(sk - m) + jnp.sum(Pc, axis=-1, keepdims=True) + jnp.sum(Pl, axis=-1, keepdims=True))
        out = (_pair_out(Pc.astype(BF16), vl_ref[0, kv, 0:ctx_len], vr_ref[0, kv, 0:ctx_len], inv_l, QBLK)
               + _pair_out(Pl.astype(BF16), vl_ref[0, kv, pl.ds(kstart, span)], vr_ref[0, kv, pl.ds(kstart, span)],
                           inv_l, QBLK))
        o_ref[0, :, kv * ATT_G * ATT_DH:(kv + 1) * ATT_G * ATT_DH] = out.astype(o_ref.dtype)


def _window_attn(qs, ks, vsl, vsr, sink_b, ctx_len):
    B, _, L, _ = qs.shape
    S = L - ctx_len
    nct = ctx_len // QBLK
    full = lambda w: pl.BlockSpec((1, ATT_KV, L, w), lambda b, j: (b, 0, 0, 0))
    return pl.pallas_call(
        functools.partial(_window_attn_kernel, ctx_len=ctx_len, seq=S),
        grid=(B, S // QBLK),
        in_specs=[pl.BlockSpec((1, ATT_HEADS, QBLK, ATT_DH), lambda b, j: (b, 0, j + nct, 0)),
                  full(ATT_DH), full(128), full(128),
                  pl.BlockSpec((ATT_HEADS, 128), lambda b, j: (0, 0))],
        out_specs=pl.BlockSpec((1, QBLK, ATT_HEADS * ATT_DH), lambda b, j: (b, j, 0)),
        out_shape=jax.ShapeDtypeStruct((B, S, ATT_HEADS * ATT_DH), BF16),
        compiler_params=_cparams("parallel", "parallel"),
        name="window_attn",
    )(qs, ks, vsl, vsr, sink_b)


def _ctx_attn_kernel(qs_ref, ks_ref, vsl_ref, vsr_ref, qd_ref, kd_ref, vd1_ref, sink_ref,
                     os_ref, od_ref, *, ctx_len):
    T = ctx_len
    for kv in range(ATT_KV):
        cols = slice(kv * ATT_G * ATT_DH, (kv + 1) * ATT_G * ATT_DH)
        Q = qs_ref[0, kv * ATT_G:(kv + 1) * ATT_G].reshape(ATT_G * T, ATT_DH)
        S = _dot_nt(Q, ks_ref[0, kv])
        sk = _sink_rows(sink_ref, kv, T)
        m = jnp.maximum(jnp.max(S, axis=-1, keepdims=True), sk)
        P = jnp.exp(S - m)
        l = jnp.sum(P, axis=-1, keepdims=True) + jnp.exp(sk - m)
        os_ref[0, :, cols] = _pair_out(P.astype(BF16), vsl_ref[0, kv], vsr_ref[0, kv], 1.0 / l, T).astype(os_ref.dtype)
        Q = qd_ref[0, kv * ATT_G:(kv + 1) * ATT_G].reshape(ATT_G * T, ATT_DH)
        od_ref[0, :, cols] = _dense_group_out(Q, kd_ref[0, kv], vd1_ref[0, kv], T).astype(od_ref.dtype)


def _ctx_attn(qs, ks, vsl, vsr, qd, kd, vd1, sink_b, ctx_len):
    B = qs.shape[0]
    T = ctx_len
    blk = lambda n, w: pl.BlockSpec((1, n, T, w), lambda b: (b, 0, 0, 0))
    out = pl.BlockSpec((1, T, ATT_HEADS * ATT_DH), lambda b: (b, 0, 0))
    return pl.pallas_call(
        functools.partial(_ctx_attn_kernel, ctx_len=T),
        grid=(B,),
        in_specs=[blk(ATT_HEADS, ATT_DH), blk(ATT_KV, ATT_DH), blk(ATT_KV, 128), blk(ATT_KV, 128),
                  blk(ATT_HEADS, ATT_DH), blk(ATT_KV, ATT_DH), blk(ATT_KV, 128),
                  pl.BlockSpec((ATT_HEADS, 128), lambda b: (0, 0))],
        out_specs=[out, out],
        out_shape=[jax.ShapeDtypeStruct((B, T, ATT_HEADS * ATT_DH), BF16)] * 2,
        compiler_params=_cparams("parallel"),
        name="ctx_attn",
    )(qs, ks, vsl, vsr, qd, kd, vd1, sink_b)


def _odd_out_kernel(ys_ref, yd_ref, h_ref, mod_ref, w_ref, o_ref):
    mix = jnp.concatenate([ys_ref[0], yd_ref[0]], axis=-1)
    o_ref[0] = h_ref[0] + mod_ref[0, 0][2:3] * _dot(mix, w_ref[...])


def _odd_out(ys, yd, h, mod, w_out, ctx_len):
    B, L, d = h.shape
    tm = ROW_TILE
    half = ATT_HEADS * ATT_DH
    row = lambda n: pl.BlockSpec((1, tm, n), lambda b, i: (b, i, 0))
    return pl.pallas_call(
        _odd_out_kernel,
        grid=(B, L // tm),
        in_specs=[row(half), row(half), row(d), _mod_spec(ctx_len // tm),
                  pl.BlockSpec((2 * half, d), lambda b, i: (0, 0))],
        out_specs=row(d),
        out_shape=jax.ShapeDtypeStruct((B, L, d), F32),
        compiler_params=_cparams("parallel", "parallel"),
        name="odd_out",
    )(ys, yd, h, mod, w_out)


def _first_lane_of(cond, lane):
    return jnp.min(jnp.where(cond, lane, ROUTE_W), axis=-1, keepdims=True)


def _router_kernel(h_ref, mod_ref, nw_ref, wr_ref, a_ref, g_ref, c_ref):
    a = _norm_mod(h_ref[0], nw_ref[...], mod_ref[0, 0], 3, 4)
    a_hi = a.astype(BF16)
    a_ref[0] = a_hi
    a_lo = (a - a_hi.astype(F32)).astype(BF16)
    w = wr_ref[...]
    w_hi = w.astype(BF16)
    w_lo = (w - w_hi.astype(F32)).astype(BF16)
    logits = _dot(a_hi, w_hi) + (_dot(a_lo, w_hi) + _dot(a_hi, w_lo))
    lane = lax.broadcasted_iota(jnp.int32, logits.shape, 1)
    gl = jnp.where(lane < N_GROUPS, logits, NEG_INF)
    gmax = jnp.max(gl, axis=-1, keepdims=True)
    p_top = 1.0 / jnp.sum(jnp.exp(gl - gmax), axis=-1, keepdims=True)
    g_idx = _first_lane_of(gl == gmax, lane)
    in_group = jnp.logical_and(lane >= EXPERT_LANE0, (lane - EXPERT_LANE0) // EXPERTS_PER_GROUP == g_idx)
    in_group = jnp.logical_and(in_group, lane < EXPERT_LANE0 + N_EXPERTS)
    el = jnp.where(in_group, logits, NEG_INF)
    m1 = jnp.max(el, axis=-1, keepdims=True)
    i1 = _first_lane_of(el == m1, lane)
    el2 = jnp.where(lane == i1, NEG_INF, el)
    m2 = jnp.max(el2, axis=-1, keepdims=True)
    i2 = _first_lane_of(el2 == m2, lane)
    e2 = jnp.exp(m2 - m1)
    w1 = 1.0 / (1.0 + e2) * p_top
    w2 = e2 / (1.0 + e2) * p_top
    j1 = i1 - EXPERT_LANE0 - EXPERTS_PER_GROUP * g_idx
    j2 = i2 - EXPERT_LANE0 - EXPERTS_PER_GROUP * g_idx
    g_ref[0] = (jnp.where(lane == j1, w1, 0.0) + jnp.where(lane == j2, w2, 0.0)
                + jnp.where(lane == GIDX_LANE, g_idx.astype(F32), 0.0))
    counts = jnp.sum((lane == g_idx).astype(jnp.int32), axis=0, keepdims=True)
    c_ref[0, 0] = jnp.broadcast_to(counts, (8, ROUTE_W))


def _router(h, mod, nw, w_route, ctx_len):
    B, L, d = h.shape
    tm = ROW_TILE
    row = lambda n: pl.BlockSpec((1, tm, n), lambda b, i: (b, i, 0))
    return pl.pallas_call(
        _router_kernel,
        grid=(B, L // tm),
        in_specs=[row(d), _mod_spec(ctx_len // tm), pl.BlockSpec((1, d), lambda b, i: (0, 0)),
                  pl.BlockSpec((d, ROUTE_W), lambda b, i: (0, 0))],
        out_specs=[row(d), row(ROUTE_W), pl.BlockSpec((1, 1, 8, ROUTE_W), lambda b, i: (b, i, 0, 0))],
        out_shape=[jax.ShapeDtypeStruct((B, L, d), BF16), jax.ShapeDtypeStruct((B, L, ROUTE_W), F32),
                   jax.ShapeDtypeStruct((B, L // tm, 8, ROUTE_W), jnp.int32)],
        compiler_params=_cparams("parallel", "parallel"),
        name="moe_router",
    )(h, mod, nw.reshape(1, d), w_route)


def _route_plan(counts, n_row_tiles):
    pc = (counts + RUN_ALIGN - 1) // RUN_ALIGN * RUN_ALIGN
    region = (jnp.sum(pc, axis=0) + EXPERT_TILE - 1) // EXPERT_TILE * EXPERT_TILE
    region_end = jnp.cumsum(region)
    off = (region_end - region)[None, :] + jnp.cumsum(pc, axis=0) - pc
    n_used = (region_end[-1] // EXPERT_TILE).reshape(1)
    tile_row0 = jnp.arange(n_row_tiles, dtype=jnp.int32) * EXPERT_TILE
    tile_group = jnp.minimum(jnp.sum(tile_row0[:, None] >= region_end[None, :], axis=1), N_GROUPS - 1)
    return (pc.reshape(-1).astype(jnp.int32), off.reshape(-1).astype(jnp.int32),
            tile_group.astype(jnp.int32), n_used.astype(jnp.int32))


def _run_bases(pc_ref, t):
    bases, base = [], 0
    for g in range(N_GROUPS):
        bases.append(base)
        base = base + pc_ref[t * N_GROUPS + g]
    return bases


def _tile_slots(G, bases):
    tm = G.shape[0]
    lane = lax.broadcasted_iota(jnp.int32, (tm, ROUTE_W), 1)
    g_idx = G[:, GIDX_LANE:GIDX_LANE + 1].astype(jnp.int32)
    onehot = (lane == g_idx).astype(BF16)
    ti = lax.broadcasted_iota(jnp.int32, (tm, tm), 0)
    tj = lax.broadcasted_iota(jnp.int32, (tm, tm), 1)
    earlier = _dot((tj < ti).astype(BF16), onehot)
    slot = jnp.sum(jnp.where(lane == g_idx, earlier, 0.0), axis=-1, keepdims=True).astype(jnp.int32)
    for g in range(N_GROUPS):
        slot = slot + jnp.where(g_idx == g, bases[g], 0)
    return slot


def _run_copies(pc_ref, off_ref, t, buf, pairs, sems, to_sorted, live):
    bases = _run_bases(pc_ref, t)
    out = []
    for g in range(N_GROUPS):
        n = pc_ref[t * N_GROUPS + g]
        dst0 = off_ref[t * N_GROUPS + g]
        for k, size in enumerate(RUN_PIECES):
            po = n & ~(2 * size - 1)
            present = jnp.logical_and(live, (n & size) != 0)
            v0 = pl.multiple_of(bases[g] + po, RUN_ALIGN)
            h0 = pl.multiple_of(dst0 + po, RUN_ALIGN)
            for a, (v_ref, h_ref) in enumerate(pairs):
                v, hb = v_ref.at[buf, pl.ds(v0, size)], h_ref.at[pl.ds(h0, size)]
                src, dst = (v, hb) if to_sorted else (hb, v)
                out.append((present, pltpu.make_async_copy(src, dst, sems.at[buf, a, g, k])))
    return out


def _start(copies):
    for present, cp in copies:
        pl.when(present)(cp.start)


def _wait(copies):
    for present, cp in copies:
        pl.when(present)(cp.wait)


def _dispatch_kernel(pc_ref, off_ref, a_ref, g_ref, xs_in, gs_in, xs_ref, gs_ref, sx_ref, sg_ref, sems):
    del xs_in, gs_in
    n_tiles = pl.num_programs(0) * pl.num_programs(1)
    t = pl.program_id(0) * pl.num_programs(1) + pl.program_id(1)
    buf = t % 2
    G = g_ref[0]
    slot = _tile_slots(G, _run_bases(pc_ref, t))
    tm = G.shape[0]
    lane = lax.broadcasted_iota(jnp.int32, (tm, SORT_ROWS), 1)
    perm = (lane == slot).astype(F32).T.astype(BF16)
    sx_ref[buf] = _dot(perm, a_ref[0]).astype(BF16)
    sg_ref[buf] = _dot3_rhs(perm, G)
    pairs = [(sx_ref, xs_ref), (sg_ref, gs_ref)]
    _start(_run_copies(pc_ref, off_ref, t, buf, pairs, sems, True, True))
    _wait(_run_copies(pc_ref, off_ref, jnp.maximum(t - 1, 0), 1 - buf, pairs, sems, True, t >= 1))
    _wait(_run_copies(pc_ref, off_ref, t, buf, pairs, sems, True, t == n_tiles - 1))


def _dispatch(a2, gates, pc, off, n_rows):
    B, L, d = a2.shape
    tm = ROW_TILE
    row = lambda n: pl.BlockSpec((1, tm, n), lambda b, i, *_: (b, i, 0))
    anyspec = pl.BlockSpec(memory_space=pl.ANY)
    return pl.pallas_call(
        _dispatch_kernel,
        grid_spec=pltpu.PrefetchScalarGridSpec(
            num_scalar_prefetch=2, grid=(B, L // tm),
            in_specs=[row(d), row(ROUTE_W), anyspec, anyspec],
            out_specs=[anyspec, anyspec],
            scratch_shapes=[pltpu.VMEM((2, SORT_ROWS, d), BF16), pltpu.VMEM((2, SORT_ROWS, ROUTE_W), F32),
                            pltpu.SemaphoreType.DMA((2, 2, N_GROUPS, len(RUN_PIECES)))]),
        out_shape=[jax.ShapeDtypeStruct((n_rows, d), BF16), jax.ShapeDtypeStruct((n_rows, ROUTE_W), F32)],
        input_output_aliases={4: 0, 5: 1},
        compiler_params=_cparams("arbitrary", "arbitrary"),
        name="moe_dispatch",
    )(pc, off, a2, gates, jnp.zeros((n_rows, d), BF16), jnp.zeros((n_rows, ROUTE_W), F32))


def _group_experts_kernel(tg_ref, nu_ref, x_ref, g_ref, wg_ref, wu_ref, wd_ref, y_ref):
    del tg_ref
    used = pl.program_id(0) < nu_ref[0]

    @pl.when(jnp.logical_not(used))
    def _():
        y_ref[...] = jnp.zeros_like(y_ref)

    @pl.when(used)
    def _():
        x = x_ref[...]
        hid = _silu(_dot(x, wg_ref[0])) * _dot(x, wu_ref[0])
        gates = g_ref[...]
        hid16 = jnp.concatenate(
            [(hid[:, e * EXPERT_FF:(e + 1) * EXPERT_FF] * gates[:, e:e + 1]).astype(BF16)
             for e in range(EXPERTS_PER_GROUP)], axis=-1)
        y_ref[...] = _dot(hid16, wd_ref[0])


def _group_experts(xs, gs, tile_group, n_used, wg, wu, wd):
    n_rows, d = xs.shape
    tm = EXPERT_TILE
    ff = EXPERTS_PER_GROUP * EXPERT_FF
    row = lambda n: pl.BlockSpec((tm, n), lambda i, tg, nu: (jnp.minimum(i, nu[0] - 1), 0))
    wspec = lambda r, c: pl.BlockSpec((1, r, c), lambda i, tg, nu: (tg[jnp.minimum(i, nu[0] - 1)], 0, 0))
    return pl.pallas_call(
        _group_experts_kernel,
        grid_spec=pltpu.PrefetchScalarGridSpec(
            num_scalar_prefetch=2, grid=(n_rows // tm,),
            in_specs=[row(d), row(ROUTE_W), wspec(d, ff), wspec(d, ff), wspec(ff, d)],
            out_specs=pl.BlockSpec((tm, d), lambda i, tg, nu: (i, 0))),
        out_shape=jax.ShapeDtypeStruct((n_rows, d), F32),
        compiler_params=_cparams("arbitrary"),
        name="moe_experts",
    )(tile_group, n_used, xs, gs, wg, wu, wd)


def _combine_kernel(pc_ref, off_ref, g_ref, h_ref, mod_ref, ys_ref, o_ref, sy_ref, sems):
    n_tiles = pl.num_programs(0) * pl.num_programs(1)
    t = pl.program_id(0) * pl.num_programs(1) + pl.program_id(1)
    buf = t % 2
    pairs = [(sy_ref, ys_ref)]

    @pl.when(t == 0)
    def _():
        sy_ref[...] = jnp.zeros_like(sy_ref)

    _start(_run_copies(pc_ref, off_ref, t, buf, pairs, sems, False, t == 0))
    _start(_run_copies(pc_ref, off_ref, jnp.minimum(t + 1, n_tiles - 1), 1 - buf, pairs, sems, False, t + 1 < n_tiles))
    G = g_ref[0]
    slot = _tile_slots(G, _run_bases(pc_ref, t))
    tm = G.shape[0]
    lane = lax.broadcasted_iota(jnp.int32, (tm, SORT_ROWS), 1)
    pick = (lane == slot).astype(BF16)
    _wait(_run_copies(pc_ref, off_ref, t, buf, pairs, sems, False, True))
    ys = sy_ref[buf]
    hi = ys.astype(BF16)
    lo = (ys - hi.astype(F32)).astype(BF16)
    o_ref[0] = h_ref[0] + mod_ref[0, 0][5:6] * (_dot(pick, hi) + _dot(pick, lo))


def _combine(ys, gates, h, mod, pc, off, ctx_len):
    B, L, d = h.shape
    tm = ROW_TILE
    nct = ctx_len // tm
    row = lambda n: pl.BlockSpec((1, tm, n), lambda b, i, *_: (b, i, 0))
    return pl.pallas_call(
        _combine_kernel,
        grid_spec=pltpu.PrefetchScalarGridSpec(
            num_scalar_prefetch=2, grid=(B, L // tm),
            in_specs=[row(ROUTE_W), row(d),
                      pl.BlockSpec((1, 1, 6, d), lambda b, i, *_: (b, (i >= nct).astype(jnp.int32), 0, 0)),
                      pl.BlockSpec(memory_space=pl.ANY)],
            out_specs=row(d),
            scratch_shapes=[pltpu.VMEM((2, SORT_ROWS, d), F32),
                            pltpu.SemaphoreType.DMA((2, 1, N_GROUPS, len(RUN_PIECES)))]),
        out_shape=jax.ShapeDtypeStruct((B, L, d), F32),
        compiler_params=_cparams("arbitrary", "arbitrary"),
        name="moe_combine",
    )(pc, off, gates, h, mod, ys)


def _final_norm_kernel(h_ref, nw_ref, o_ref):
    x = h_ref[0]
    o_ref[0] = x * lax.rsqrt(jnp.mean(x * x, axis=-1, keepdims=True) + EPS) * nw_ref[...]


def _final_norm(h, nw, ctx_len):
    B, L, d = h.shape
    tm = ROW_TILE
    nct = ctx_len // tm
    return pl.pallas_call(
        _final_norm_kernel,
        grid=(B, (L - ctx_len) // tm),
        in_specs=[pl.BlockSpec((1, tm, d), lambda b, i: (b, i + nct, 0)),
                  pl.BlockSpec((1, d), lambda b, i: (0, 0))],
        out_specs=pl.BlockSpec((1, tm, d), lambda b, i: (b, i, 0)),
        out_shape=jax.ShapeDtypeStruct((B, L - ctx_len, d), F32),
        compiler_params=_cparams("parallel", "parallel"),
        name="final_norm",
    )(h, nw.reshape(1, d))


def _rope_tables(seq):
    rows = seq // GRID_W
    row = jnp.repeat(jnp.arange(rows), GRID_W).astype(F32)
    col = (jnp.arange(rows * GRID_W) % GRID_W).astype(F32)
    axis_dim = ATT_DH // 2
    inv = ROPE_THETA ** (-jnp.arange(0, axis_dim, 2, dtype=F32) / axis_dim)
    ang = jnp.concatenate([row[:, None] * inv, col[:, None] * inv], axis=-1)
    cosf = jnp.repeat(jnp.cos(ang), 2, axis=-1)
    sinf = jnp.repeat(jnp.sin(ang), 2, axis=-1) * jnp.tile(jnp.array([-1.0, 1.0], F32), ATT_DH // 2)
    return jnp.tile(cosf, (1, 2)), jnp.tile(sinf, (1, 2))


def kernel(x, c, ctx, c_ctx, ada_w, ada_b, norm1_w, norm2_w, ev_w_in, ev_conv_w, ev_conv_b, ev_dt_bias, ev_a_log, ev_d_skip, ev_ssd_norm_w, ev_ig_bias, ev_fg_bias, ev_mlstm_norm_w, ev_w_out, od_w_in, od_sink, od_q_norm_w, od_k_norm_w, od_w_out, moe_w_group, moe_w_expert, moe_w_gate, moe_w_up, moe_w_down, final_norm_w):
    B, S, d = x.shape
    T = ctx.shape[1]
    depth = ada_w.shape[0]
    assert d == D_MODEL and T % ROW_TILE == 0 and S % ROW_TILE == 0 and S % GRID_W == 0 and S >= 3 * QBLK

    h = jnp.concatenate([ctx, x], axis=1)

    rows = -(-(B + 1) // 8) * 8
    cond = jnp.zeros((rows, d), F32).at[:B].set(c).at[B].set(c_ctx)
    ada = _adaln(cond, ada_w, ada_b).reshape(depth, rows, 6, d)
    mods = jnp.stack([jnp.broadcast_to(ada[:, B:B + 1], (depth, B, 6, d)), ada[:, :B]], axis=2)

    cosf, sinf = _rope_tables(S)

    for layer in range(depth):
        li = layer // 2
        mod = mods[layer]
        if layer % 2 == 0:
            w_in = ev_w_in[li]
            conv, z, v, o, gate_cols = (w_in[:, :CONV_CH], w_in[:, CONV_CH:CONV_CH + SSD_INNER],
                                        w_in[:, CONV_CH + SSD_INNER:CONV_CH + 2 * SSD_INNER],
                                        w_in[:, CONV_CH + 2 * SSD_INNER:CONV_CH + 3 * SSD_INNER],
                                        w_in[:, CONV_CH + 3 * SSD_INNER:])
            w_rec = jnp.concatenate([v, conv], axis=1).astype(BF16)
            w_zo = jnp.concatenate([z, o], axis=1).astype(BF16)
            w_g = jnp.pad(gate_cols, ((0, 0), (0, GATE_W - N_GATE))).astype(BF16)
            rec, zo, gates = _norm_mod_matmul(h, mod, norm1_w[layer], [w_rec, w_zo, w_g], [BF16, BF16, F32], 0, 1, T)
            gates_t = jnp.swapaxes(gates[:, :, :N_GATE], 1, 2)
            bias = jnp.concatenate([ev_dt_bias[li].reshape(-1), ev_ig_bias[li].reshape(-1), ev_fg_bias[li].reshape(-1)])
            alog = jnp.pad(ev_a_log[li].reshape(-1), (0, N_GATE - 2 * SSD_HEADS))
            prow = jnp.pad(jnp.stack([bias, alog]), ((0, 0), (0, GATE_W - N_GATE)))
            pcol = jnp.stack([bias, alog], axis=1)
            dskip = jnp.repeat(ev_d_skip[li], SSD_HEAD_DIM).reshape(1, SSD_INNER)
            cb = ev_conv_b[li].reshape(1, CONV_CH)
            yf = _mixer_scan(rec, gates, gates_t, ev_conv_w[li], cb, prow, pcol, dskip, T, False)
            y = _mixer_scan(rec, gates, gates_t, ev_conv_w[li], cb, prow, pcol, yf, T, True)
            h = _even_out(y, zo, h, mod, ev_ssd_norm_w[li], ev_mlstm_norm_w[li], ev_w_out[li].astype(BF16), T)
        else:
            (proj,) = _norm_mod_matmul(h, mod, norm1_w[layer], [od_w_in[li].astype(BF16)], [BF16], 0, 1, T)
            qnw = jnp.tile(od_q_norm_w[li], ATT_HEADS).reshape(1, -1)
            knw = jnp.tile(od_k_norm_w[li], ATT_KV).reshape(1, -1)
            qs, ks, vsl, vsr, qd, kd, vd1 = _qkv_prep(proj, cosf, sinf, qnw, knw, T)
            sink_b = jnp.broadcast_to(od_sink[li].reshape(ATT_HEADS, 1), (ATT_HEADS, 128))
            ys_lat = _window_attn(qs, ks, vsl, vsr, sink_b, T)
            yd_lat = _dense_attn(qd, kd, vd1, T)
            ys_ctx, yd_ctx = _ctx_attn(qs, ks, vsl, vsr, qd, kd, vd1, sink_b, T)
            ys = jnp.concatenate([ys_ctx, ys_lat], axis=1)
            yd = jnp.concatenate([yd_ctx, yd_lat], axis=1)
            h = _odd_out(ys, yd, h, mod, od_w_out[li].astype(BF16), T)

        w_route = jnp.pad(jnp.concatenate([moe_w_group[layer], moe_w_expert[layer]], axis=1),
                          ((0, 0), (0, ROUTE_W - N_GROUPS - N_EXPERTS)))
        a2, gates, counts = _router(h, mod, norm2_w[layer], w_route, T)
        n_tiles = B * ((T + S) // ROW_TILE)
        n_rows = -(-(n_tiles * (ROW_TILE + N_GROUPS * RUN_ALIGN) + N_GROUPS * EXPERT_TILE) // EXPERT_TILE) * EXPERT_TILE
        pc, off, tile_group, n_used = _route_plan(counts[:, :, 0, :N_GROUPS].reshape(n_tiles, N_GROUPS),
                                                  n_rows // EXPERT_TILE)
        xs, gs = _dispatch(a2, gates, pc, off, n_rows)
        ff = EXPERTS_PER_GROUP * EXPERT_FF
        wg = moe_w_gate[layer].transpose(0, 2, 1, 3).reshape(N_GROUPS, d, ff).astype(BF16)
        wu = moe_w_up[layer].transpose(0, 2, 1, 3).reshape(N_GROUPS, d, ff).astype(BF16)
        wd = moe_w_down[layer].reshape(N_GROUPS, ff, d).astype(BF16)
        ys = _group_experts(xs, gs, tile_group, n_used, wg, wu, wd)
        h = _combine(ys, gates, h, mod, pc, off, T)

    return _final_norm(h, final_norm_w, T)
```

```python
import functools
import math

import jax
import jax.numpy as jnp
from jax import lax
from jax.experimental import pallas as pl
from jax.experimental.pallas import tpu as pltpu

F32 = jnp.float32
BF16 = jnp.bfloat16
EPS = 1e-6
NEG_INF = float("-inf")
LOG2E = 1.4426950408889634

D_MODEL = 1024
GRID_W = 64
ROPE_THETA = 10000.0

SSD_HEADS = 16
SSD_HEAD_DIM = 64
SSD_INNER = 1024
SSD_GROUPS = 2
SSD_STATE = 128
MLSTM_HEADS = 4
MLSTM_QK_DIM = 128
MLSTM_V_DIM = 256
MLSTM_QK = 512
MLSTM_INNER = 1024
CHUNK = 128
MIXER_BATCH = 2
HALO = 16
CONV_CH = 2560
REC_W = MLSTM_INNER + CONV_CH
N_GATE = 48
GATE_W = 128
COL_DT, COL_IG, COL_FG = 0, 32, 40

ATT_DH = 64
ATT_HEADS = 8
ATT_KV = 2
ATT_G = ATT_HEADS // ATT_KV
WINDOW = 128
QBLK = 128
ODD_PROJ = 1536

N_GROUPS = 4
EXPERTS_PER_GROUP = 4
N_EXPERTS = N_GROUPS * EXPERTS_PER_GROUP
EXPERT_FF = 256
ROUTE_W = 128
EXPERT_LANE0 = N_GROUPS

ROW_TILE = 256
GIDX_LANE = 8
RUN_ALIGN = 16
RUN_PIECES = (256, 128, 64, 32, 16)
SORT_ROWS = -(-(ROW_TILE + N_GROUPS * (RUN_ALIGN - 1)) // 128) * 128
EXPERT_TILE = 512
VMEM_LIMIT = 56 * 1024 * 1024


def _cparams(*sem):
    return pltpu.CompilerParams(dimension_semantics=sem, vmem_limit_bytes=VMEM_LIMIT)


def _sigmoid(x):
    return 0.5 * jnp.tanh(0.5 * x) + 0.5


def _silu(x):
    return x * _sigmoid(x)


def _softplus(x):
    return jnp.maximum(x, 0.0) + jnp.log(1.0 + jnp.exp(-jnp.abs(x)))


def _dot(a, b):
    return jnp.dot(a, b, preferred_element_type=F32)


def _dot_nt(a, b):
    return lax.dot_general(a, b, (((1,), (1,)), ((), ())), preferred_element_type=F32)


def _split3(a):
    hi = a.astype(BF16)
    r1 = a - hi.astype(F32)
    mid = r1.astype(BF16)
    lo = (r1 - mid.astype(F32)).astype(BF16)
    return hi, mid, lo


def _dot3_rhs(exact, a):
    hi, mid, lo = _split3(a)
    return _dot(exact, hi) + _dot(exact, mid) + _dot(exact, lo)


def _dot3_lhs(a, exact):
    hi, mid, lo = _split3(a)
    return _dot(hi, exact) + _dot(mid, exact) + _dot(lo, exact)


def _adaln_kernel(cond_ref, w_ref, b_ref, o_ref):
    s = _silu(cond_ref[...])
    o_ref[0] = _dot(s.astype(BF16), w_ref[0].astype(BF16)) + b_ref[0]


def _adaln(cond, ada_w, ada_b):
    depth, d, n = ada_w.shape
    rows = cond.shape[0]
    tn = 1536
    return pl.pallas_call(
        _adaln_kernel,
        grid=(depth, n // tn),
        in_specs=[pl.BlockSpec((rows, d), lambda l, j: (0, 0)),
                  pl.BlockSpec((1, d, tn), lambda l, j: (l, 0, j)),
                  pl.BlockSpec((1, 1, tn), lambda l, j: (l, 0, j))],
        out_specs=pl.BlockSpec((1, rows, tn), lambda l, j: (l, 0, j)),
        out_shape=jax.ShapeDtypeStruct((depth, rows, n), F32),
        compiler_params=_cparams("parallel", "parallel"),
        name="adaln",
    )(cond, ada_w, ada_b.reshape(depth, 1, n))


def _norm_mod(x, nw, mod, shift_row, scale_row):
    var = jnp.mean(x * x, axis=-1, keepdims=True)
    y = x * lax.rsqrt(var + EPS) * nw
    return y * (1.0 + mod[scale_row:scale_row + 1]) + mod[shift_row:shift_row + 1]


def _nmm_kernel(h_ref, mod_ref, nw_ref, *rest, n_out, shift_row, scale_row, tn):
    w_refs, o_refs = rest[:n_out], rest[n_out:]
    a16 = _norm_mod(h_ref[0], nw_ref[...], mod_ref[0, 0], shift_row, scale_row).astype(BF16)
    for w_ref, o_ref in zip(w_refs, o_refs):
        n = w_ref.shape[1]
        for j in range(0, n, tn):
            w = min(tn, n - j)
            o_ref[0, :, j:j + w] = _dot(a16, w_ref[:, j:j + w]).astype(o_ref.dtype)


def _mod_spec(nct_tiles):
    return pl.BlockSpec((1, 1, 6, D_MODEL), lambda b, i: (b, (i >= nct_tiles).astype(jnp.int32), 0, 0))


def _norm_mod_matmul(h, mod, nw, weights, out_dtypes, shift_row, scale_row, ctx_len):
    B, L, d = h.shape
    tm = ROW_TILE
    in_specs = [pl.BlockSpec((1, tm, d), lambda b, i: (b, i, 0)),
                _mod_spec(ctx_len // tm),
                pl.BlockSpec((1, d), lambda b, i: (0, 0))]
    out_specs, out_shape = [], []
    for w, dt in zip(weights, out_dtypes):
        n = w.shape[1]
        in_specs.append(pl.BlockSpec((d, n), lambda b, i: (0, 0)))
        out_specs.append(pl.BlockSpec((1, tm, n), lambda b, i: (b, i, 0)))
        out_shape.append(jax.ShapeDtypeStruct((B, L, n), dt))
    return pl.pallas_call(
        functools.partial(_nmm_kernel, n_out=len(weights), shift_row=shift_row, scale_row=scale_row, tn=512),
        grid=(B, L // tm),
        in_specs=in_specs, out_specs=out_specs, out_shape=out_shape,
        compiler_params=_cparams("parallel", "parallel"),
        name="norm_mod_matmul",
    )(h, mod, nw.reshape(1, d), *weights)


def _mixer_kernel(rec_ref, prev_ref, next_ref, g_ref, gt_ref, cw_ref, cb_ref, prow_ref, pcol_ref, extra_ref,
                  y_ref, conv_ref, S_ref, C_ref, n_ref, m_ref, *, reverse, nct, nc, nb):
    i = pl.program_id(1)
    c = jnp.where(i < nct, nct - 1 - i, nc - 1 - (i - nct)) if reverse else i
    seq_start = jnp.logical_or(c == 0, c == nct)
    seq_end = jnp.logical_or(c == nct - 1, c == nc - 1)

    @pl.when(i == 0)
    def _():
        S_ref[...] = jnp.zeros_like(S_ref)
        C_ref[...] = jnp.zeros_like(C_ref)
        n_ref[...] = jnp.zeros_like(n_ref)
        m_ref[...] = jnp.zeros_like(m_ref)

    for bb in range(nb):
        xc = rec_ref[bb, :, MLSTM_INNER:].astype(F32)
        before = prev_ref[bb].astype(F32)[HALO - 1:HALO, MLSTM_INNER:]
        after = next_ref[bb].astype(F32)[0:1, MLSTM_INNER:]
        before = jnp.where(seq_start, 0.0, before)
        after = jnp.where(seq_end, 0.0, after)
        rid = lax.broadcasted_iota(jnp.int32, (CHUNK, 1), 0)
        x_prev = jnp.where(rid == 0, before, pltpu.roll(xc, 1, 0))
        x_next = jnp.where(rid == CHUNK - 1, after, pltpu.roll(xc, CHUNK - 1, 0))
        cw = cw_ref[...]
        conv = conv_ref.at[bb]
        conv[...] = _silu(cw[0:1] * x_prev + cw[1:2] * xc + cw[2:3] * x_next + cb_ref[...])

        if reverse:
            def emit(col0, width, val, bb=bb):
                y_ref[bb, :, col0:col0 + width] = (extra_ref[bb, :, col0:col0 + width] + val).astype(y_ref.dtype)
        else:
            def emit(col0, width, val, bb=bb, conv=conv):
                if col0 < SSD_INNER:
                    val = val + conv[:, col0:col0 + width] * extra_ref[:, col0:col0 + width]
                y_ref[bb, :, col0:col0 + width] = val

        _mixer_chunk(rec_ref.at[bb], g_ref.at[bb], gt_ref.at[bb], prow_ref, pcol_ref, conv,
                     S_ref.at[bb], C_ref.at[bb], n_ref.at[bb], m_ref.at[bb], emit, reverse=reverse)


def _mixer_chunk(v_ref, g_ref, gt_ref, prow_ref, pcol_ref, conv_ref, S_ref, C_ref, n_ref, m_ref, emit, *, reverse):
    d = 1 if reverse else 0
    li = lax.broadcasted_iota(jnp.int32, (CHUNK, CHUNK), 0)
    si = lax.broadcasted_iota(jnp.int32, (CHUNK, CHUNK), 1)
    mask = (si >= li) if reverse else (si <= li)
    tri = mask.astype(BF16)
    tri_t = ((li >= si) if reverse else (li <= si)).astype(BF16)
    last = 0 if reverse else CHUNK - 1

    G = g_ref[...] + prow_ref[0:1]
    GT = gt_ref[...] + pcol_ref[:, 0:1]
    lane = lax.broadcasted_iota(jnp.int32, (CHUNK, GATE_W), 1)
    row = lax.broadcasted_iota(jnp.int32, (N_GATE, CHUNK), 0)
    dt_c = _softplus(jnp.where(lane < COL_IG, G, -G))
    dt_r = _softplus(jnp.where(row < COL_IG, GT, -GT))
    nega_c = -jnp.exp(prow_ref[1:2])
    nega_r = -jnp.exp(pcol_ref[:, 1:2])
    pre_c = jnp.where(lane < COL_IG, dt_c * nega_c, jnp.where(lane >= COL_FG, -dt_c, 0.0))
    pre_c = jnp.where(lane < N_GATE, pre_c, 0.0)
    pre_r = jnp.where(row < COL_IG, dt_r * nega_r, jnp.where(row >= COL_FG, -dt_r, 0.0))
    cum_c = _dot3_rhs(tri, pre_c)
    cum_r = _dot3_lhs(pre_r, tri_t)

    a_c = cum_c[:, d * SSD_HEADS:(d + 1) * SSD_HEADS]
    a_r = cum_r[d * SSD_HEADS:(d + 1) * SSD_HEADS, :]
    a_last = a_c[last:last + 1, :]
    hh = lax.broadcasted_iota(jnp.int32, (SSD_HEADS, SSD_INNER), 0)
    hj = lax.broadcasted_iota(jnp.int32, (SSD_HEADS, SSD_INNER), 1)
    expand = (hj // SSD_HEAD_DIM == hh).astype(BF16)
    dt_x = _dot3_lhs(dt_c[:, d * SSD_HEADS:(d + 1) * SSD_HEADS], expand)
    dec_x = _dot3_lhs(jnp.exp(a_last - a_c), expand)
    ein_x = _dot3_lhs(jnp.exp(a_c), expand)
    cdec_x = _dot3_lhs(jnp.broadcast_to(jnp.exp(a_last), (8, SSD_HEADS)), expand)[0:1]

    xsdt = conv_ref[:, 0:SSD_INNER] * dt_x
    xsdt16 = xsdt.astype(BF16)
    xdec16 = (xsdt * dec_x).astype(BF16)
    lane128 = lax.broadcasted_iota(jnp.int32, (CHUNK, 128), 1)
    heads_per_group = SSD_HEADS // SSD_GROUPS
    gw = heads_per_group * SSD_HEAD_DIM
    for g in range(SSD_GROUPS):
        Bg = conv_ref[:, SSD_INNER + g * SSD_STATE:SSD_INNER + (g + 1) * SSD_STATE]
        Cg16 = conv_ref[:, SSD_INNER + 256 + g * SSD_STATE:SSD_INNER + 256 + (g + 1) * SSD_STATE].astype(BF16)
        CB = _dot_nt(Cg16, Bg.astype(BF16))
        S_g = S_ref[:, g * gw:(g + 1) * gw]
        y_off = _dot(Cg16, S_g.astype(BF16)) * ein_x[:, g * gw:(g + 1) * gw]
        for pr in range(heads_per_group // 2):
            col0 = g * gw + pr * 128
            xpair = xsdt16[:, col0:col0 + 128]
            acc = None
            for half in range(2):
                h = g * heads_per_group + pr * 2 + half
                seg = a_c[:, h:h + 1] - a_r[h:h + 1, :]
                Lm = jnp.exp(jnp.where(mask, seg, NEG_INF))
                Mh = (CB * Lm).astype(BF16)
                keep = (lane128 < 64) if half == 0 else (lane128 >= 64)
                part = _dot(Mh, jnp.where(keep, xpair, jnp.zeros_like(xpair)))
                acc = part if acc is None else acc + part
            emit(col0, 128, acc + y_off[:, pr * 128:(pr + 1) * 128])
        S_ref[:, g * gw:(g + 1) * gw] = (cdec_x[:, g * gw:(g + 1) * gw] * S_g
                                         + _dot(Bg.T.astype(BF16), xdec16[:, g * gw:(g + 1) * gw]))

    for h in range(MLSTM_HEADS):
        gi = COL_IG + d * MLSTM_HEADS + h
        gf = COL_FG + d * MLSTM_HEADS + h
        qh16 = conv_ref[:, 1536 + h * 128:1536 + (h + 1) * 128].astype(BF16)
        kh = conv_ref[:, 2048 + h * 128:2048 + (h + 1) * 128] * (MLSTM_QK_DIM ** -0.5)
        kh16 = kh.astype(BF16)
        vh16 = v_ref[:, h * MLSTM_V_DIM:(h + 1) * MLSTM_V_DIM]
        ig_c, ig_r = G[:, gi:gi + 1], GT[gi:gi + 1, :]
        b_c, b_r = cum_c[:, gf:gf + 1], cum_r[gf:gf + 1, :]
        b_last = b_c[last:last + 1, :]
        Dlog = jnp.where(mask, b_c - b_r + ig_r, NEG_INF)
        m_loc = jnp.max(b_last - b_r + ig_r, axis=-1, keepdims=True)
        ek = jnp.exp(b_last - b_c + ig_c - m_loc) * kh
        C_loc = _dot(ek.T.astype(BF16), vh16)
        n_loc = jnp.sum(ek, axis=0, keepdims=True)
        C_in = C_ref[h]
        n_in = n_ref[h:h + 1, :]
        m_in = m_ref[h:h + 1, 0:1]
        m_inter = b_c + m_in
        m_t = jnp.maximum(m_inter, jnp.max(Dlog, axis=-1, keepdims=True))
        P = jnp.exp(Dlog - m_t) * _dot_nt(qh16, kh16)
        w_inter = jnp.exp(m_inter - m_t)
        num = _dot(P.astype(BF16), vh16) + w_inter * _dot(qh16, C_in.astype(BF16))
        qn = jnp.sum(qh16.astype(F32) * n_in, axis=-1, keepdims=True)
        den = jnp.sum(P, axis=-1, keepdims=True) + w_inter * qn
        den = jnp.maximum(jnp.abs(den), jnp.exp(-m_t))
        emit(SSD_INNER + h * MLSTM_V_DIM, MLSTM_V_DIM, num / den)
        m_new = jnp.maximum(b_last + m_in, m_loc)
        fa = jnp.exp(b_last + m_in - m_new)
        fb = jnp.exp(m_loc - m_new)
        C_ref[h] = fa * C_in + fb * C_loc
        n_ref[h:h + 1, :] = fa * n_in + fb * n_loc
        m_ref[h:h + 1, :] = jnp.broadcast_to(m_new, (1, 128))


def _mixer_scan(rec, gates, gates_t, conv_w, conv_b, prow, pcol, extra, ctx_len, reverse):
    B, L, _ = rec.shape
    nc, nct = L // CHUNK, ctx_len // CHUNK
    nb = MIXER_BATCH
    hb = CHUNK // HALO
    wide = SSD_INNER + MLSTM_INNER

    def chunk_of(i):
        return jnp.where(i < nct, nct - 1 - i, nc - 1 - (i - nct)) if reverse else i

    const = lambda r, n: pl.BlockSpec((r, n), lambda b, i: (0, 0))
    chunk = lambda n: pl.BlockSpec((nb, CHUNK, n), lambda b, i: (b, chunk_of(i), 0))
    extra_spec = chunk(wide) if reverse else const(1, SSD_INNER)
    return pl.pallas_call(
        functools.partial(_mixer_kernel, reverse=reverse, nct=nct, nc=nc, nb=nb),
        grid=(B // nb, nc),
        in_specs=[chunk(REC_W),
                  pl.BlockSpec((nb, HALO, REC_W), lambda b, i: (b, jnp.maximum(chunk_of(i) * hb - 1, 0), 0)),
                  pl.BlockSpec((nb, HALO, REC_W),
                               lambda b, i: (b, jnp.minimum((chunk_of(i) + 1) * hb, L // HALO - 1), 0)),
                  chunk(GATE_W),
                  pl.BlockSpec((nb, N_GATE, CHUNK), lambda b, i: (b, 0, chunk_of(i))),
                  const(3, CONV_CH), const(1, CONV_CH), const(2, GATE_W), const(N_GATE, 2), extra_spec],
        out_specs=chunk(wide),
        out_shape=jax.ShapeDtypeStruct((B, L, wide), BF16 if reverse else F32),
        scratch_shapes=[pltpu.VMEM((nb, CHUNK, CONV_CH), F32),
                        pltpu.VMEM((nb, SSD_STATE, SSD_INNER), F32),
                        pltpu.VMEM((nb, MLSTM_HEADS, MLSTM_QK_DIM, MLSTM_V_DIM), F32),
                        pltpu.VMEM((nb, 8, MLSTM_QK_DIM), F32),
                        pltpu.VMEM((nb, 8, 128), F32)],
        compiler_params=_cparams("parallel", "arbitrary"),
        name="mixer_bwd" if reverse else "mixer_fwd",
    )(rec, rec, rec, gates, gates_t, conv_w, conv_b, prow, pcol, extra)


def _group_rms(x, groups):
    w = x.shape[-1] // groups
    parts = []
    for g in range(groups):
        seg = x[:, g * w:(g + 1) * w]
        parts.append(seg * lax.rsqrt(jnp.mean(seg * seg, axis=-1, keepdims=True) + EPS))
    return jnp.concatenate(parts, axis=-1)


def _even_out_kernel(y_ref, zo_ref, h_ref, mod_ref, snw_ref, mnw_ref, w_ref, o_ref):
    y = y_ref[0].astype(F32)
    z = zo_ref[0, :, 0:SSD_INNER].astype(F32)
    o = zo_ref[0, :, SSD_INNER:].astype(F32)
    ys = _group_rms(y[:, 0:SSD_INNER] * _silu(z), SSD_GROUPS) * snw_ref[...]
    hm = _group_rms(y[:, SSD_INNER:], MLSTM_HEADS) * mnw_ref[...] * _sigmoid(o)
    mix = jnp.concatenate([ys, hm], axis=-1).astype(BF16)
    o_ref[0] = h_ref[0] + mod_ref[0, 0][2:3] * _dot(mix, w_ref[...])


def _even_out(y, zo, h, mod, snw, mnw, w_out, ctx_len):
    B, L, d = h.shape
    tm = ROW_TILE
    wide = SSD_INNER + MLSTM_INNER
    row = lambda n: pl.BlockSpec((1, tm, n), lambda b, i: (b, i, 0))
    const = lambda r, n: pl.BlockSpec((r, n), lambda b, i: (0, 0))
    return pl.pallas_call(
        _even_out_kernel,
        grid=(B, L // tm),
        in_specs=[row(wide), row(wide), row(d), _mod_spec(ctx_len // tm),
                  const(1, SSD_INNER), const(1, MLSTM_INNER), const(wide, d)],
        out_specs=row(d),
        out_shape=jax.ShapeDtypeStruct((B, L, d), F32),
        compiler_params=_cparams("parallel", "parallel"),
        name="even_out",
    )(y, zo, h, mod, snw.reshape(1, -1), mnw.reshape(1, -1), w_out)


def _qkv_prep_kernel(p_ref, cs_ref, sn_ref, qnw_ref, knw_ref,
                     qs_ref, ks_ref, vs1_ref, qd_ref, kd_ref, vd1_ref, *, nct):
    is_lat = pl.program_id(1) >= nct
    p = p_ref[0].astype(F32)
    cs, sn = cs_ref[...], sn_ref[...]

    def rope(x):
        w = x.shape[1]
        reps = w // 128
        c = jnp.concatenate([cs] * reps, axis=1) if reps > 1 else cs
        s = jnp.concatenate([sn] * reps, axis=1) if reps > 1 else sn
        lane = lax.broadcasted_iota(jnp.int32, x.shape, 1)
        partner = jnp.where(lane % 2 == 0, pltpu.roll(x, w - 1, 1), pltpu.roll(x, 1, 1))
        return jnp.where(is_lat, x * c + partner * s, x)

    def head_rms(x, nw):
        w = x.shape[1]
        a = lax.broadcasted_iota(jnp.int32, (w, w), 0) // ATT_DH
        b = lax.broadcasted_iota(jnp.int32, (w, w), 1) // ATT_DH
        same = (a == b).astype(BF16)
        ms = _dot3_lhs(x * x, same) * (1.0 / ATT_DH)
        return x * lax.rsqrt(ms + EPS) * nw

    qw, kw = ATT_HEADS * ATT_DH, ATT_KV * ATT_DH
    o = 0
    q_s = rope(p[:, o:o + qw]) * (ATT_DH ** -0.5 * LOG2E); o += qw
    k_s = rope(p[:, o:o + kw]); o += kw
    v_s = p[:, o:o + kw]; o += kw
    q_d = rope(head_rms(p[:, o:o + qw], qnw_ref[...])) * (ATT_DH ** -0.5 * LOG2E); o += qw
    k_d = rope(head_rms(p[:, o:o + kw], knw_ref[...])); o += kw
    v_d = p[:, o:o + kw]

    lane = lax.broadcasted_iota(jnp.int32, (QBLK, 128), 1)
    for h in range(ATT_HEADS):
        qs_ref[0, h] = q_s[:, h * ATT_DH:(h + 1) * ATT_DH].astype(BF16)
        qd_ref[0, h] = q_d[:, h * ATT_DH:(h + 1) * ATT_DH].astype(BF16)
    for kv in range(ATT_KV):
        ks_ref[0, kv] = k_s[:, kv * ATT_DH:(kv + 1) * ATT_DH].astype(BF16)
        kd_ref[0, kv] = k_d[:, kv * ATT_DH:(kv + 1) * ATT_DH].astype(BF16)
    for v, v1_ref in ((v_s, vs1_ref), (v_d, vd1_ref)):
        for kv in range(ATT_KV):
            own = jnp.where((lane // ATT_DH) == kv, v, 0.0)
            low = own if kv == 0 else pltpu.roll(own, ATT_DH, 1)
            v1_ref[0, kv] = jnp.where(lane == ATT_DH, 1.0, low).astype(BF16)


def _qkv_prep(proj, cosf, sinf, qnw, knw, ctx_len):
    B, L, _ = proj.shape
    nct = ctx_len // QBLK
    tab = pl.BlockSpec((QBLK, 128), lambda b, i: (jnp.maximum(i - nct, 0), 0))
    head = lambda n, w: pl.BlockSpec((1, n, QBLK, w), lambda b, i: (b, 0, i, 0))
    shp = lambda n, w: jax.ShapeDtypeStruct((B, n, L, w), BF16)
    return pl.pallas_call(
        functools.partial(_qkv_prep_kernel, nct=nct),
        grid=(B, L // QBLK),
        in_specs=[pl.BlockSpec((1, QBLK, ODD_PROJ), lambda b, i: (b, i, 0)), tab, tab,
                  pl.BlockSpec((1, ATT_HEADS * ATT_DH), lambda b, i: (0, 0)),
                  pl.BlockSpec((1, ATT_KV * ATT_DH), lambda b, i: (0, 0))],
        out_specs=[head(ATT_HEADS, ATT_DH), head(ATT_KV, ATT_DH), head(ATT_KV, 128)] * 2,
        out_shape=[shp(ATT_HEADS, ATT_DH), shp(ATT_KV, ATT_DH), shp(ATT_KV, 128)] * 2,
        compiler_params=_cparams("parallel", "parallel"),
        name="qkv_prep",
    )(proj, cosf, sinf, qnw, knw)


def _group_softmax_out(scores, values, rows, sink=None):
    m = functools.reduce(jnp.maximum, [jnp.max(s, axis=-1, keepdims=True) for s in scores])
    if sink is not None:
        m = jnp.maximum(m, sink)
    probs = [jnp.exp2(s - m).astype(BF16) for s in scores]
    lane = lax.broadcasted_iota(jnp.int32, (rows, 128), 1)
    outs = []
    for pr in range(ATT_G // 2):
        heads = []
        for g in (2 * pr, 2 * pr + 1):
            rs = slice(g * rows, (g + 1) * rows)
            O = functools.reduce(lambda a, b: a + b, [_dot(p[rs], v) for p, v in zip(probs, values)])
            l = O[:, ATT_DH:ATT_DH + 1]
            if sink is not None:
                l = l + jnp.exp2(sink[rs] - m[rs])
            heads.append(O * (1.0 / l))
        outs.append(jnp.where(lane < ATT_DH, heads[0], pltpu.roll(heads[1], ATT_DH, 1)))
    return jnp.concatenate(outs, axis=-1)


def _dense_attn_kernel(q_ref, k_ref, v1_ref, o_ref):
    for kv in range(ATT_KV):
        Q = q_ref[0, kv * ATT_G:(kv + 1) * ATT_G].reshape(ATT_G * QBLK, ATT_DH)
        o_ref[0, :, kv * ATT_G * ATT_DH:(kv + 1) * ATT_G * ATT_DH] = _group_softmax_out(
            [_dot_nt(Q, k_ref[0, kv])], [v1_ref[0, kv]], QBLK).astype(o_ref.dtype)


def _dense_attn(qd, kd, vd1, ctx_len):
    B, _, L, _ = qd.shape
    S = L - ctx_len
    nct = ctx_len // QBLK
    full = lambda w: pl.BlockSpec((1, ATT_KV, L, w), lambda b, j: (b, 0, 0, 0))
    return pl.pallas_call(
        _dense_attn_kernel,
        grid=(B, S // QBLK),
        in_specs=[pl.BlockSpec((1, ATT_HEADS, QBLK, ATT_DH), lambda b, j: (b, 0, j + nct, 0)),
                  full(ATT_DH), full(128)],
        out_specs=pl.BlockSpec((1, QBLK, ATT_HEADS * ATT_DH), lambda b, j: (b, j, 0)),
        out_shape=jax.ShapeDtypeStruct((B, S, ATT_HEADS * ATT_DH), BF16),
        compiler_params=_cparams("parallel", "parallel"),
        name="dense_attn",
    )(qd, kd, vd1)


def _sink_rows(sink_ref, kv, rows):
    return jnp.concatenate(
        [jnp.broadcast_to(sink_ref[kv * ATT_G + g:kv * ATT_G + g + 1, 0:1], (rows, 1)) for g in range(ATT_G)], axis=0)


def _window_attn_kernel(q_ref, k_ref, v1_ref, sink_ref, o_ref, *, ctx_len, seq):
    j = pl.program_id(1)
    span = 3 * QBLK
    start = jnp.clip((j - 1) * QBLK, 0, seq - span)
    kstart = pl.multiple_of(ctx_len + start, QBLK)
    rows = ATT_G * QBLK
    qpos = j * QBLK + lax.broadcasted_iota(jnp.int32, (rows, span), 0) % QBLK
    kpos = start + lax.broadcasted_iota(jnp.int32, (rows, span), 1)
    valid = jnp.abs(kpos - qpos) <= WINDOW
    for kv in range(ATT_KV):
        Q = q_ref[0, kv * ATT_G:(kv + 1) * ATT_G].reshape(rows, ATT_DH)
        Sc = _dot_nt(Q, k_ref[0, kv, 0:ctx_len])
        Sl = jnp.where(valid, _dot_nt(Q, k_ref[0, kv, pl.ds(kstart, span)]), NEG_INF)
        out = _group_softmax_out([Sc, Sl], [v1_ref[0, kv, 0:ctx_len], v1_ref[0, kv, pl.ds(kstart, span)]], QBLK,
                                 sink=_sink_rows(sink_ref, kv, QBLK) * LOG2E)
        o_ref[0, :, kv * ATT_G * ATT_DH:(kv + 1) * ATT_G * ATT_DH] = out.astype(o_ref.dtype)


def _window_attn(qs, ks, vs1, sink_b, ctx_len):
    B, _, L, _ = qs.shape
    S = L - ctx_len
    nct = ctx_len // QBLK
    full = lambda w: pl.BlockSpec((1, ATT_KV, L, w), lambda b, j: (b, 0, 0, 0))
    return pl.pallas_call(
        functools.partial(_window_attn_kernel, ctx_len=ctx_len, seq=S),
        grid=(B, S // QBLK),
        in_specs=[pl.BlockSpec((1, ATT_HEADS, QBLK, ATT_DH), lambda b, j: (b, 0, j + nct, 0)),
                  full(ATT_DH), full(128),
                  pl.BlockSpec((ATT_HEADS, 128), lambda b, j: (0, 0))],
        out_specs=pl.BlockSpec((1, QBLK, ATT_HEADS * ATT_DH), lambda b, j: (b, j, 0)),
        out_shape=jax.ShapeDtypeStruct((B, S, ATT_HEADS * ATT_DH), BF16),
        compiler_params=_cparams("parallel", "parallel"),
        name="window_attn",
    )(qs, ks, vs1, sink_b)


def _ctx_attn_kernel(qs_ref, ks_ref, vs1_ref, qd_ref, kd_ref, vd1_ref, sink_ref, os_ref, od_ref, *, ctx_len):
    T = ctx_len
    for kv in range(ATT_KV):
        cols = slice(kv * ATT_G * ATT_DH, (kv + 1) * ATT_G * ATT_DH)
        Q = qs_ref[0, kv * ATT_G:(kv + 1) * ATT_G].reshape(ATT_G * T, ATT_DH)
        os_ref[0, :, cols] = _group_softmax_out([_dot_nt(Q, ks_ref[0, kv])], [vs1_ref[0, kv]], T,
                                                sink=_sink_rows(sink_ref, kv, T) * LOG2E).astype(os_ref.dtype)
        Q = qd_ref[0, kv * ATT_G:(kv + 1) * ATT_G].reshape(ATT_G * T, ATT_DH)
        od_ref[0, :, cols] = _group_softmax_out([_dot_nt(Q, kd_ref[0, kv])], [vd1_ref[0, kv]], T).astype(od_ref.dtype)


def _ctx_attn(qs, ks, vs1, qd, kd, vd1, sink_b, ctx_len):
    B = qs.shape[0]
    T = ctx_len
    blk = lambda n, w: pl.BlockSpec((1, n, T, w), lambda b: (b, 0, 0, 0))
    out = pl.BlockSpec((1, T, ATT_HEADS * ATT_DH), lambda b: (b, 0, 0))
    return pl.pallas_call(
        functools.partial(_ctx_attn_kernel, ctx_len=T),
        grid=(B,),
        in_specs=[blk(ATT_HEADS, ATT_DH), blk(ATT_KV, ATT_DH), blk(ATT_KV, 128),
                  blk(ATT_HEADS, ATT_DH), blk(ATT_KV, ATT_DH), blk(ATT_KV, 128),
                  pl.BlockSpec((ATT_HEADS, 128), lambda b: (0, 0))],
        out_specs=[out, out],
        out_shape=[jax.ShapeDtypeStruct((B, T, ATT_HEADS * ATT_DH), BF16)] * 2,
        compiler_params=_cparams("parallel"),
        name="ctx_attn",
    )(qs, ks, vs1, qd, kd, vd1, sink_b)


def _odd_out_kernel(ys_ref, yd_ref, h_ref, mod_ref, w_ref, o_ref):
    mix = jnp.concatenate([ys_ref[0], yd_ref[0]], axis=-1)
    o_ref[0] = h_ref[0] + mod_ref[0, 0][2:3] * _dot(mix, w_ref[...])


def _odd_out(ys, yd, h, mod, w_out, ctx_len):
    B, L, d = h.shape
    tm = ROW_TILE
    half = ATT_HEADS * ATT_DH
    row = lambda n: pl.BlockSpec((1, tm, n), lambda b, i: (b, i, 0))
    return pl.pallas_call(
        _odd_out_kernel,
        grid=(B, L // tm),
        in_specs=[row(half), row(half), row(d), _mod_spec(ctx_len // tm),
                  pl.BlockSpec((2 * half, d), lambda b, i: (0, 0))],
        out_specs=row(d),
        out_shape=jax.ShapeDtypeStruct((B, L, d), F32),
        compiler_params=_cparams("parallel", "parallel"),
        name="odd_out",
    )(ys, yd, h, mod, w_out)


def _first_lane_of(cond, lane):
    return jnp.min(jnp.where(cond, lane, ROUTE_W), axis=-1, keepdims=True)


def _router_kernel(h_ref, mod_ref, nw_ref, wr_ref, a_ref, c_ref):
    a = _norm_mod(h_ref[0], nw_ref[...], mod_ref[0, 0], 3, 4)
    a_hi = a.astype(BF16)
    a_ref[0, :, :D_MODEL] = a_hi
    a_lo = (a - a_hi.astype(F32)).astype(BF16)
    w = wr_ref[...]
    w_hi = w.astype(BF16)
    w_lo = (w - w_hi.astype(F32)).astype(BF16)
    logits = _dot(a_hi, w_hi) + (_dot(a_lo, w_hi) + _dot(a_hi, w_lo))
    lane = lax.broadcasted_iota(jnp.int32, logits.shape, 1)
    gl = jnp.where(lane < N_GROUPS, logits, NEG_INF)
    gmax = jnp.max(gl, axis=-1, keepdims=True)
    p_top = 1.0 / jnp.sum(jnp.exp(gl - gmax), axis=-1, keepdims=True)
    g_idx = _first_lane_of(gl == gmax, lane)
    in_group = jnp.logical_and(lane >= EXPERT_LANE0, (lane - EXPERT_LANE0) // EXPERTS_PER_GROUP == g_idx)
    in_group = jnp.logical_and(in_group, lane < EXPERT_LANE0 + N_EXPERTS)
    el = jnp.where(in_group, logits, NEG_INF)
    m1 = jnp.max(el, axis=-1, keepdims=True)
    i1 = _first_lane_of(el == m1, lane)
    el2 = jnp.where(lane == i1, NEG_INF, el)
    m2 = jnp.max(el2, axis=-1, keepdims=True)
    i2 = _first_lane_of(el2 == m2, lane)
    e2 = jnp.exp(m2 - m1)
    w1 = 1.0 / (1.0 + e2) * p_top
    w2 = e2 / (1.0 + e2) * p_top
    j1 = i1 - EXPERT_LANE0 - EXPERTS_PER_GROUP * g_idx
    j2 = i2 - EXPERT_LANE0 - EXPERTS_PER_GROUP * g_idx
    both = lambda j, w: jnp.where(jnp.logical_or(lane == j, lane == j + EXPERTS_PER_GROUP), w, 0.0)
    gate = both(j1, w1) + both(j2, w2)
    gate_hi = gate.astype(BF16).astype(F32)
    tail = jnp.where(lane < EXPERTS_PER_GROUP, gate_hi, gate - gate_hi)
    tail = jnp.where(lane == GIDX_LANE, g_idx.astype(F32), tail)
    a_ref[0, :, D_MODEL:] = tail.astype(BF16)
    counts =jnp.sum((lane == g_idx).astype(jnp.int32), axis=0, keepdims=True)
    c_ref[0, 0] = jnp.broadcast_to(counts, (8, ROUTE_W))


def _router(h, mod, nw, w_route, ctx_len):
    B, L, d = h.shape
    tm = ROW_TILE
    row = lambda n: pl.BlockSpec((1, tm, n), lambda b, i: (b, i, 0))
    return pl.pallas_call(
        _router_kernel,
        grid=(B, L // tm),
        in_specs=[row(d), _mod_spec(ctx_len // tm), pl.BlockSpec((1, d), lambda b, i: (0, 0)),
                  pl.BlockSpec((d, ROUTE_W), lambda b, i: (0, 0))],
        out_specs=[row(d + ROUTE_W), pl.BlockSpec((1, 1, 8, ROUTE_W), lambda b, i: (b, i, 0, 0))],
        out_shape=[jax.ShapeDtypeStruct((B, L, d + ROUTE_W), BF16),
                   jax.ShapeDtypeStruct((B, L // tm, 8, ROUTE_W), jnp.int32)],
        compiler_params=_cparams("parallel", "parallel"),
        name="moe_router",
    )(h, mod, nw.reshape(1, d), w_route)


def _route_plan(counts, n_row_tiles):
    pc = (counts + RUN_ALIGN - 1) // RUN_ALIGN * RUN_ALIGN
    region = (jnp.sum(pc, axis=0) + EXPERT_TILE - 1) // EXPERT_TILE * EXPERT_TILE
    region_end = jnp.cumsum(region)
    off = (region_end - region)[None, :] + jnp.cumsum(pc, axis=0) - pc
    n_used = (region_end[-1] // EXPERT_TILE).reshape(1)
    tile_row0 = jnp.arange(n_row_tiles, dtype=jnp.int32) * EXPERT_TILE
    tile_group = jnp.minimum(jnp.sum(tile_row0[:, None] >= region_end[None, :], axis=1), N_GROUPS - 1)
    return (pc.reshape(-1).astype(jnp.int32), off.reshape(-1).astype(jnp.int32),
            tile_group.astype(jnp.int32), n_used.astype(jnp.int32))


def _run_bases(pc_ref, t):
    bases, base = [], 0
    for g in range(N_GROUPS):
        bases.append(base)
        base = base + pc_ref[t * N_GROUPS + g]
    return bases


def _tile_slots(G, bases):
    tm = G.shape[0]
    lane = lax.broadcasted_iota(jnp.int32, (tm, ROUTE_W), 1)
    g_idx = G[:, GIDX_LANE:GIDX_LANE + 1].astype(jnp.int32)
    onehot = (lane == g_idx).astype(BF16)
    ti = lax.broadcasted_iota(jnp.int32, (tm, tm), 0)
    tj = lax.broadcasted_iota(jnp.int32, (tm, tm), 1)
    earlier = _dot((tj < ti).astype(BF16), onehot)
    slot = jnp.sum(jnp.where(lane == g_idx, earlier, 0.0), axis=-1, keepdims=True).astype(jnp.int32)
    for g in range(N_GROUPS):
        slot = slot + jnp.where(g_idx == g, bases[g], 0)
    return slot


def _run_copies(pc_ref, off_ref, t, buf, pairs, sems, to_sorted, live):
    bases = _run_bases(pc_ref, t)
    out = []
    for g in range(N_GROUPS):
        n = pc_ref[t * N_GROUPS + g]
        dst0 = off_ref[t * N_GROUPS + g]
        for k, size in enumerate(RUN_PIECES):
            po = n & ~(2 * size - 1)
            present = jnp.logical_and(live, (n & size) != 0)
            v0 = pl.multiple_of(bases[g] + po, RUN_ALIGN)
            h0 = pl.multiple_of(dst0 + po, RUN_ALIGN)
            for a, (v_ref, h_ref) in enumerate(pairs):
                v, hb = v_ref.at[buf, pl.ds(v0, size)], h_ref.at[pl.ds(h0, size)]
                src, dst = (v, hb) if to_sorted else (hb, v)
                out.append((present, pltpu.make_async_copy(src, dst, sems.at[buf, a, g, k])))
    return out


def _start(copies):
    for present, cp in copies:
        pl.when(present)(cp.start)


def _wait(copies):
    for present, cp in copies:
        pl.when(present)(cp.wait)


def _dispatch_kernel(pc_ref, off_ref, a_ref, xs_in, xs_ref, sx_ref, sems):
    del xs_in
    n_tiles = pl.num_programs(0) * pl.num_programs(1)
    t = pl.program_id(0) * pl.num_programs(1) + pl.program_id(1)
    buf = t % 2
    G = a_ref[0, :, D_MODEL:].astype(F32)
    slot = _tile_slots(G, _run_bases(pc_ref, t))
    tm = G.shape[0]
    lane = lax.broadcasted_iota(jnp.int32, (tm, SORT_ROWS), 1)
    perm = (lane == slot).astype(F32).T.astype(BF16)
    sx_ref[buf] = _dot(perm, a_ref[0]).astype(BF16)
    pairs = [(sx_ref, xs_ref)]
    _start(_run_copies(pc_ref, off_ref, t, buf, pairs, sems, True, True))
    _wait(_run_copies(pc_ref, off_ref, jnp.maximum(t - 1, 0), 1 - buf, pairs, sems, True, t >= 1))
    _wait(_run_copies(pc_ref, off_ref, t, buf, pairs, sems, True, t == n_tiles - 1))


def _dispatch(a2, pc, off, n_rows):
    B, L, w = a2.shape
    tm = ROW_TILE
    anyspec = pl.BlockSpec(memory_space=pl.ANY)
    return pl.pallas_call(
        _dispatch_kernel,
        grid_spec=pltpu.PrefetchScalarGridSpec(
            num_scalar_prefetch=2, grid=(B, L // tm),
            in_specs=[pl.BlockSpec((1, tm, w), lambda b, i, *_: (b, i, 0)), anyspec],
            out_specs=anyspec,
            scratch_shapes=[pltpu.VMEM((2, SORT_ROWS, w), BF16),
                            pltpu.SemaphoreType.DMA((2, 1, N_GROUPS, len(RUN_PIECES)))]),
        out_shape=jax.ShapeDtypeStruct((n_rows, w), BF16),
        input_output_aliases={3: 0},
        compiler_params=_cparams("arbitrary", "arbitrary"),
        name="moe_dispatch",
    )(pc, off, a2, jnp.zeros((n_rows, w), BF16))


def _group_experts_kernel(tg_ref, nu_ref, x_ref, wg_ref, wu_ref, wd_ref, y_ref):
    del tg_ref
    used = pl.program_id(0) < nu_ref[0]

    @pl.when(jnp.logical_not(used))
    def _():
        y_ref[...] = jnp.zeros_like(y_ref)

    @pl.when(used)
    def _():
        x = x_ref[:, :D_MODEL]
        hid = _silu(_dot(x, wg_ref[0])) * _dot(x, wu_ref[0])
        tail = x_ref[:, D_MODEL:].astype(F32)
        gates = tail[:, 0:EXPERTS_PER_GROUP] + tail[:, EXPERTS_PER_GROUP:2 * EXPERTS_PER_GROUP]
        hid16 = jnp.concatenate(
            [(hid[:, e * EXPERT_FF:(e + 1) * EXPERT_FF] * gates[:, e:e + 1]).astype(BF16)
             for e in range(EXPERTS_PER_GROUP)], axis=-1)
        y_ref[...] = _dot(hid16, wd_ref[0]).astype(y_ref.dtype)


def _group_experts(xs, tile_group, n_used, wg, wu, wd):
    n_rows, w = xs.shape
    d = D_MODEL
    tm = EXPERT_TILE
    ff = EXPERTS_PER_GROUP * EXPERT_FF
    wspec = lambda r, c: pl.BlockSpec((1, r, c), lambda i, tg, nu: (tg[jnp.minimum(i, nu[0] - 1)], 0, 0))
    return pl.pallas_call(
        _group_experts_kernel,
        grid_spec=pltpu.PrefetchScalarGridSpec(
            num_scalar_prefetch=2, grid=(n_rows // tm,),
            in_specs=[pl.BlockSpec((tm, w), lambda i, tg, nu: (jnp.minimum(i, nu[0] - 1), 0)),
                      wspec(d, ff), wspec(d, ff), wspec(ff, d)],
            out_specs=pl.BlockSpec((tm, d), lambda i, tg, nu: (i, 0))),
        out_shape=jax.ShapeDtypeStruct((n_rows, d), BF16),
        compiler_params=_cparams("arbitrary"),
        name="moe_experts",
    )(tile_group, n_used, xs, wg, wu, wd)


def _window_copies(pc_ref, off_ref, t, buf, sy_ref, ys_ref, sems, live):
    out = []
    for g in range(N_GROUPS):
        present = jnp.logical_and(live, pc_ref[t * N_GROUPS + g] > 0)
        h0 = pl.multiple_of(off_ref[t * N_GROUPS + g], RUN_ALIGN)
        out.append((present, pltpu.make_async_copy(ys_ref.at[pl.ds(h0, ROW_TILE)],
                                                   sy_ref.at[buf, pl.ds(g * ROW_TILE, ROW_TILE)], sems.at[buf, g])))
    return out


def _combine_kernel(pc_ref, off_ref, g_ref, h_ref, mod_ref, ys_ref, o_ref, sy_ref, sems):
    n_tiles = pl.num_programs(0) * pl.num_programs(1)
    t = pl.program_id(0) * pl.num_programs(1) + pl.program_id(1)
    buf = t % 2

    @pl.when(t == 0)
    def _():
        sy_ref[...] = jnp.zeros_like(sy_ref)

    _start(_window_copies(pc_ref, off_ref, t, buf, sy_ref, ys_ref, sems, t == 0))
    _start(_window_copies(pc_ref, off_ref, jnp.minimum(t + 1, n_tiles - 1), 1 - buf, sy_ref, ys_ref, sems,
                          t + 1 < n_tiles))
    G = g_ref[0].astype(F32)
    tm = G.shape[0]
    slot = _tile_slots(G, [g * tm for g in range(N_GROUPS)])
    lane = lax.broadcasted_iota(jnp.int32, (tm, N_GROUPS * tm), 1)
    pick = (lane == slot).astype(BF16)
    _wait(_window_copies(pc_ref, off_ref, t, buf, sy_ref, ys_ref, sems, True))
    o_ref[0] = h_ref[0] + mod_ref[0, 0][5:6] * _dot(pick, sy_ref[buf])


def _combine(ys, a2, h, mod, pc, off, ctx_len):
    B, L, d = h.shape
    tm = ROW_TILE
    nct = ctx_len // tm
    row = lambda n: pl.BlockSpec((1, tm, n), lambda b, i, *_: (b, i, 0))
    return pl.pallas_call(
        _combine_kernel,
        grid_spec=pltpu.PrefetchScalarGridSpec(
            num_scalar_prefetch=2, grid=(B, L // tm),
            in_specs=[pl.BlockSpec((1, tm, ROUTE_W), lambda b, i, *_: (b, i, d // ROUTE_W)),
                      row(d),
                      pl.BlockSpec((1, 1, 6, d), lambda b, i, *_: (b, (i >= nct).astype(jnp.int32), 0, 0)),
                      pl.BlockSpec(memory_space=pl.ANY)],
            out_specs=row(d),
            scratch_shapes=[pltpu.VMEM((2, N_GROUPS * tm, d), BF16),
                            pltpu.SemaphoreType.DMA((2, N_GROUPS))]),
        out_shape=jax.ShapeDtypeStruct((B, L, d), F32),
        compiler_params=_cparams("arbitrary", "arbitrary"),
        name="moe_combine",
    )(pc, off, a2, h, mod, ys)


def _final_norm_kernel(h_ref, nw_ref, o_ref):
    x = h_ref[0]
    o_ref[0] = x * lax.rsqrt(jnp.mean(x * x, axis=-1, keepdims=True) + EPS) * nw_ref[...]


def _final_norm(h, nw, ctx_len):
    B, L, d = h.shape
    tm = ROW_TILE
    nct = ctx_len // tm
    return pl.pallas_call(
        _final_norm_kernel,
        grid=(B, (L - ctx_len) // tm),
        in_specs=[pl.BlockSpec((1, tm, d), lambda b, i: (b, i + nct, 0)),
                  pl.BlockSpec((1, d), lambda b, i: (0, 0))],
        out_specs=pl.BlockSpec((1, tm, d), lambda b, i: (b, i, 0)),
        out_shape=jax.ShapeDtypeStruct((B, L - ctx_len, d), F32),
        compiler_params=_cparams("parallel", "parallel"),
        name="final_norm",
    )(h, nw.reshape(1, d))


def _rope_tables(seq):
    rows = seq // GRID_W
    row = jnp.repeat(jnp.arange(rows), GRID_W).astype(F32)
    col = (jnp.arange(rows * GRID_W) % GRID_W).astype(F32)
    axis_dim = ATT_DH // 2
    inv = ROPE_THETA ** (-jnp.arange(0, axis_dim, 2, dtype=F32) / axis_dim)
    ang = jnp.concatenate([row[:, None] * inv, col[:, None] * inv], axis=-1)
    cosf = jnp.repeat(jnp.cos(ang), 2, axis=-1)
    sinf = jnp.repeat(jnp.sin(ang), 2, axis=-1) * jnp.tile(jnp.array([-1.0, 1.0], F32), ATT_DH // 2)
    return jnp.tile(cosf, (1, 2)), jnp.tile(sinf, (1, 2))


def kernel(x, c, ctx, c_ctx, ada_w, ada_b, norm1_w, norm2_w, ev_w_in, ev_conv_w, ev_conv_b, ev_dt_bias, ev_a_log, ev_d_skip, ev_ssd_norm_w, ev_ig_bias, ev_fg_bias, ev_mlstm_norm_w, ev_w_out, od_w_in, od_sink, od_q_norm_w, od_k_norm_w, od_w_out, moe_w_group, moe_w_expert, moe_w_gate, moe_w_up, moe_w_down, final_norm_w):
    B, S, d = x.shape
    T = ctx.shape[1]
    depth = ada_w.shape[0]
    assert d == D_MODEL and T % ROW_TILE == 0 and S % ROW_TILE == 0 and S % GRID_W == 0 and S >= 3 * QBLK

    h = jnp.concatenate([ctx, x], axis=1)

    rows = -(-(B + 1) // 8) * 8
    cond = jnp.zeros((rows, d), F32).at[:B].set(c).at[B].set(c_ctx)
    ada = _adaln(cond, ada_w, ada_b).reshape(depth, rows, 6, d)
    mods = jnp.stack([jnp.broadcast_to(ada[:, B:B + 1], (depth, B, 6, d)), ada[:, :B]], axis=2)

    cosf, sinf = _rope_tables(S)

    for layer in range(depth):
        li = layer // 2
        mod = mods[layer]
        if layer % 2 == 0:
            w_in = ev_w_in[li]
            conv, z, v, o, gate_cols = (w_in[:, :CONV_CH], w_in[:, CONV_CH:CONV_CH + SSD_INNER],
                                        w_in[:, CONV_CH + SSD_INNER:CONV_CH + 2 * SSD_INNER],
                                        w_in[:, CONV_CH + 2 * SSD_INNER:CONV_CH + 3 * SSD_INNER],
                                        w_in[:, CONV_CH + 3 * SSD_INNER:])
            w_rec = jnp.concatenate([v, conv], axis=1).astype(BF16)
            w_zo = jnp.concatenate([z, o], axis=1).astype(BF16)
            w_g = jnp.pad(gate_cols, ((0, 0), (0, GATE_W - N_GATE))).astype(BF16)
            rec, zo, gates = _norm_mod_matmul(h, mod, norm1_w[layer], [w_rec, w_zo, w_g], [BF16, BF16, F32], 0, 1, T)
            gates_t = jnp.swapaxes(gates[:, :, :N_GATE], 1, 2)
            bias = jnp.concatenate([ev_dt_bias[li].reshape(-1), ev_ig_bias[li].reshape(-1), ev_fg_bias[li].reshape(-1)])
            alog = jnp.pad(ev_a_log[li].reshape(-1), (0, N_GATE - 2 * SSD_HEADS))
            prow = jnp.pad(jnp.stack([bias, alog]), ((0, 0), (0, GATE_W - N_GATE)))
            pcol = jnp.stack([bias, alog], axis=1)
            dskip = jnp.repeat(ev_d_skip[li], SSD_HEAD_DIM).reshape(1, SSD_INNER)
            cb = ev_conv_b[li].reshape(1, CONV_CH)
            yf = _mixer_scan(rec, gates, gates_t, ev_conv_w[li], cb, prow, pcol, dskip, T, False)
            y = _mixer_scan(rec, gates, gates_t, ev_conv_w[li], cb, prow, pcol, yf, T, True)
            h = _even_out(y, zo, h, mod, ev_ssd_norm_w[li], ev_mlstm_norm_w[li], ev_w_out[li].astype(BF16), T)
        else:
            (proj,) = _norm_mod_matmul(h, mod, norm1_w[layer], [od_w_in[li].astype(BF16)], [BF16], 0, 1, T)
            qnw = jnp.tile(od_q_norm_w[li], ATT_HEADS).reshape(1, -1)
            knw = jnp.tile(od_k_norm_w[li], ATT_KV).reshape(1, -1)
            qs, ks, vs1, qd, kd, vd1 = _qkv_prep(proj, cosf, sinf, qnw, knw, T)
            sink_b = jnp.broadcast_to(od_sink[li].reshape(ATT_HEADS, 1), (ATT_HEADS, 128))
            ys_lat = _window_attn(qs, ks, vs1, sink_b, T)
            yd_lat = _dense_attn(qd, kd, vd1, T)
            ys_ctx, yd_ctx = _ctx_attn(qs, ks, vs1, qd, kd, vd1, sink_b, T)
            ys = jnp.concatenate([ys_ctx, ys_lat], axis=1)
            yd = jnp.concatenate([yd_ctx, yd_lat], axis=1)
            h = _odd_out(ys, yd, h, mod, od_w_out[li].astype(BF16), T)

        w_route = jnp.pad(jnp.concatenate([moe_w_group[layer], moe_w_expert[layer]], axis=1),
                          ((0, 0), (0, ROUTE_W - N_GROUPS - N_EXPERTS)))
        a2, counts = _router(h, mod, norm2_w[layer], w_route, T)
        n_tiles = B * ((T + S) // ROW_TILE)
        n_rows = -(-(n_tiles * (ROW_TILE + N_GROUPS * RUN_ALIGN) + N_GROUPS * EXPERT_TILE + ROW_TILE)
                   // EXPERT_TILE) * EXPERT_TILE
        pc, off, tile_group, n_used = _route_plan(counts[:, :, 0, :N_GROUPS].reshape(n_tiles, N_GROUPS),
                                                  n_rows // EXPERT_TILE)
        xs = _dispatch(a2, pc, off, n_rows)
        ff = EXPERTS_PER_GROUP * EXPERT_FF
        wg = moe_w_gate[layer].transpose(0, 2, 1, 3).reshape(N_GROUPS, d, ff).astype(BF16)
        wu = moe_w_up[layer].transpose(0, 2, 1, 3).reshape(N_GROUPS, d, ff).astype(BF16)
        wd = moe_w_down[layer].reshape(N_GROUPS, ff, d).astype(BF16)
        ys = _group_experts(xs, tile_group, n_used, wg, wu, wd)
        h = _combine(ys, a2, h, mod, pc, off, T)

    return _final_norm(h, final_norm_w, T)
```

```python
import functools
import math

import jax
import jax.numpy as jnp
from jax import lax
from jax.experimental import pallas as pl
from jax.experimental.pallas import tpu as pltpu

F32 = jnp.float32
BF16 = jnp.bfloat16
EPS = 1e-6
NEG_INF = float("-inf")
LOG2E = 1.4426950408889634

D_MODEL = 1024
GRID_W = 64
ROPE_THETA = 10000.0

SSD_HEADS = 16
SSD_HEAD_DIM = 64
SSD_INNER = 1024
SSD_GROUPS = 2
SSD_STATE = 128
MLSTM_HEADS = 4
MLSTM_QK_DIM = 128
MLSTM_V_DIM = 256
MLSTM_QK = 512
MLSTM_INNER = 1024
CHUNK = 128
MIXER_BATCH = 2
HALO = 16
CONV_CH = 2560
REC_W = MLSTM_INNER + CONV_CH
N_GATE = 48
GATE_W = 128
COL_DT, COL_IG, COL_FG = 0, 32, 40

ATT_DH = 64
ATT_HEADS = 8
ATT_KV = 2
ATT_G = ATT_HEADS // ATT_KV
WINDOW = 128
QBLK = 128
ODD_PROJ = 1536

N_GROUPS = 4
EXPERTS_PER_GROUP = 4
N_EXPERTS = N_GROUPS * EXPERTS_PER_GROUP
EXPERT_FF = 256
ROUTE_W = 128
EXPERT_LANE0 = N_GROUPS

ROW_TILE = 256
GIDX_LANE = 8
RUN_ALIGN = 16
RUN_PIECES = (256, 128, 64, 32, 16)
SORT_ROWS = -(-(ROW_TILE + N_GROUPS * (RUN_ALIGN - 1)) // 128) * 128
EXPERT_TILE = 512
VMEM_LIMIT = 56 * 1024 * 1024


def _cparams(*sem):
    return pltpu.CompilerParams(dimension_semantics=sem, vmem_limit_bytes=VMEM_LIMIT)


def _sigmoid(x):
    return 0.5 * jnp.tanh(0.5 * x) + 0.5


def _silu(x):
    return x * _sigmoid(x)


def _softplus(x):
    return jnp.maximum(x, 0.0) + jnp.log(1.0 + jnp.exp(-jnp.abs(x)))


def _dot(a, b):
    return jnp.dot(a, b, preferred_element_type=F32)


def _dot_nt(a, b):
    return lax.dot_general(a, b, (((1,), (1,)), ((), ())), preferred_element_type=F32)


def _split3(a):
    hi = a.astype(BF16)
    r1 = a - hi.astype(F32)
    mid = r1.astype(BF16)
    lo = (r1 - mid.astype(F32)).astype(BF16)
    return hi, mid, lo


def _dot3_rhs(exact, a):
    hi, mid, lo = _split3(a)
    return _dot(exact, hi) + _dot(exact, mid) + _dot(exact, lo)


def _dot3_lhs(a, exact):
    hi, mid, lo = _split3(a)
    return _dot(hi, exact) + _dot(mid, exact) + _dot(lo, exact)


def _adaln_kernel(cond_ref, w_ref, b_ref, o_ref):
    s = _silu(cond_ref[...])
    o_ref[0] = _dot(s.astype(BF16), w_ref[0].astype(BF16)) + b_ref[0]


def _adaln(cond, ada_w, ada_b):
    depth, d, n = ada_w.shape
    rows = cond.shape[0]
    tn = 1536
    return pl.pallas_call(
        _adaln_kernel,
        grid=(depth, n // tn),
        in_specs=[pl.BlockSpec((rows, d), lambda l, j: (0, 0)),
                  pl.BlockSpec((1, d, tn), lambda l, j: (l, 0, j)),
                  pl.BlockSpec((1, 1, tn), lambda l, j: (l, 0, j))],
        out_specs=pl.BlockSpec((1, rows, tn), lambda l, j: (l, 0, j)),
        out_shape=jax.ShapeDtypeStruct((depth, rows, n), F32),
        compiler_params=_cparams("parallel", "parallel"),
        name="adaln",
    )(cond, ada_w, ada_b.reshape(depth, 1, n))


def _norm_mod(x, nw, mod, shift_row, scale_row):
    var = jnp.mean(x * x, axis=-1, keepdims=True)
    y = x * lax.rsqrt(var + EPS) * nw
    return y * (1.0 + mod[scale_row:scale_row + 1]) + mod[shift_row:shift_row + 1]


def _nmm_kernel(h_ref, mod_ref, nw_ref, *rest, n_out, shift_row, scale_row, tn):
    w_refs, o_refs = rest[:n_out], rest[n_out:]
    a16 = _norm_mod(h_ref[0], nw_ref[...], mod_ref[0, 0], shift_row, scale_row).astype(BF16)
    for w_ref, o_ref in zip(w_refs, o_refs):
        n = w_ref.shape[1]
        for j in range(0, n, tn):
            w = min(tn, n - j)
            o_ref[0, :, j:j + w] = _dot(a16, w_ref[:, j:j + w]).astype(o_ref.dtype)


def _mod_spec(nct_tiles):
    return pl.BlockSpec((1, 1, 6, D_MODEL), lambda b, i: (b, (i >= nct_tiles).astype(jnp.int32), 0, 0))


def _norm_mod_matmul(h, mod, nw, weights, out_dtypes, shift_row, scale_row, ctx_len):
    B, L, d = h.shape
    tm = ROW_TILE
    in_specs = [pl.BlockSpec((1, tm, d), lambda b, i: (b, i, 0)),
                _mod_spec(ctx_len // tm),
                pl.BlockSpec((1, d), lambda b, i: (0, 0))]
    out_specs, out_shape = [], []
    for w, dt in zip(weights, out_dtypes):
        n = w.shape[1]
        in_specs.append(pl.BlockSpec((d, n), lambda b, i: (0, 0)))
        out_specs.append(pl.BlockSpec((1, tm, n), lambda b, i: (b, i, 0)))
        out_shape.append(jax.ShapeDtypeStruct((B, L, n), dt))
    return pl.pallas_call(
        functools.partial(_nmm_kernel, n_out=len(weights), shift_row=shift_row, scale_row=scale_row, tn=512),
        grid=(B, L // tm),
        in_specs=in_specs, out_specs=out_specs, out_shape=out_shape,
        compiler_params=_cparams("parallel", "parallel"),
        name="norm_mod_matmul",
    )(h, mod, nw.reshape(1, d), *weights)


def _mixer_kernel(rec_ref, prev_ref, next_ref, g_ref, gt_ref, cw_ref, cb_ref, prow_ref, pcol_ref, extra_ref,
                  y_ref, conv_ref, S_ref, C_ref, n_ref, m_ref, *, reverse, nct, nc, nb):
    i = pl.program_id(1)
    c = jnp.where(i < nct, nct - 1 - i, nc - 1 - (i - nct)) if reverse else i
    seq_start = jnp.logical_or(c == 0, c == nct)
    seq_end = jnp.logical_or(c == nct - 1, c == nc - 1)

    @pl.when(i == 0)
    def _():
        S_ref[...] = jnp.zeros_like(S_ref)
        C_ref[...] = jnp.zeros_like(C_ref)
        n_ref[...] = jnp.zeros_like(n_ref)
        m_ref[...] = jnp.zeros_like(m_ref)

    for bb in range(nb):
        xc = rec_ref[bb, :, MLSTM_INNER:].astype(F32)
        before = prev_ref[bb].astype(F32)[HALO - 1:HALO, MLSTM_INNER:]
        after = next_ref[bb].astype(F32)[0:1, MLSTM_INNER:]
        before = jnp.where(seq_start, 0.0, before)
        after = jnp.where(seq_end, 0.0, after)
        rid = lax.broadcasted_iota(jnp.int32, (CHUNK, 1), 0)
        x_prev = jnp.where(rid == 0, before, pltpu.roll(xc, 1, 0))
        x_next = jnp.where(rid == CHUNK - 1, after, pltpu.roll(xc, CHUNK - 1, 0))
        cw = cw_ref[...]
        conv = _silu(cw[0:1] * x_prev + cw[1:2] * xc + cw[2:3] * x_next + cb_ref[...])

        if reverse:
            def emit(col0, width, val, bb=bb):
                y_ref[bb, :, col0:col0 + width] = (extra_ref[bb, :, col0:col0 + width] + val).astype(y_ref.dtype)
        else:
            def emit(col0, width, val, bb=bb, conv=conv):
                if col0 < SSD_INNER:
                    val = val + conv[:, col0:col0 + width] * extra_ref[:, col0:col0 + width]
                y_ref[bb, :, col0:col0 + width] = val

        _mixer_chunk(rec_ref.at[bb], g_ref.at[bb], gt_ref.at[bb], prow_ref, pcol_ref, conv,
                     S_ref.at[bb], C_ref.at[bb], n_ref.at[bb], m_ref.at[bb], emit, reverse=reverse)


def _mixer_chunk(v_ref, g_ref, gt_ref, prow_ref, pcol_ref, conv_ref, S_ref, C_ref, n_ref, m_ref, emit, *, reverse):
    d = 1 if reverse else 0
    li = lax.broadcasted_iota(jnp.int32, (CHUNK, CHUNK), 0)
    si = lax.broadcasted_iota(jnp.int32, (CHUNK, CHUNK), 1)
    mask = (si >= li) if reverse else (si <= li)
    tri = mask.astype(BF16)
    tri_t = ((li >= si) if reverse else (li <= si)).astype(BF16)
    last = 0 if reverse else CHUNK - 1

    G = g_ref[...] + prow_ref[0:1]
    GT = gt_ref[...] + pcol_ref[:, 0:1]
    lane = lax.broadcasted_iota(jnp.int32, (CHUNK, GATE_W), 1)
    row = lax.broadcasted_iota(jnp.int32, (N_GATE, CHUNK), 0)
    dt_c = _softplus(jnp.where(lane < COL_IG, G, -G))
    dt_r = _softplus(jnp.where(row < COL_IG, GT, -GT))
    nega_c = -jnp.exp(prow_ref[1:2])
    nega_r = -jnp.exp(pcol_ref[:, 1:2])
    pre_c = jnp.where(lane < COL_IG, dt_c * nega_c, jnp.where(lane >= COL_FG, -dt_c, 0.0))
    pre_c = jnp.where(lane < N_GATE, pre_c, 0.0)
    pre_r = jnp.where(row < COL_IG, dt_r * nega_r, jnp.where(row >= COL_FG, -dt_r, 0.0))
    cum_c = _dot3_rhs(tri, pre_c)
    cum_r = _dot3_lhs(pre_r, tri_t)

    a_c = cum_c[:, d * SSD_HEADS:(d + 1) * SSD_HEADS]
    a_r = cum_r[d * SSD_HEADS:(d + 1) * SSD_HEADS, :]
    a_last = a_c[last:last + 1, :]
    hh = lax.broadcasted_iota(jnp.int32, (SSD_HEADS, SSD_INNER), 0)
    hj = lax.broadcasted_iota(jnp.int32, (SSD_HEADS, SSD_INNER), 1)
    expand = (hj // SSD_HEAD_DIM == hh).astype(BF16)
    dt_x = _dot3_lhs(dt_c[:, d * SSD_HEADS:(d + 1) * SSD_HEADS], expand)
    dec_x = _dot3_lhs(jnp.exp(a_last - a_c), expand)
    ein_x = _dot3_lhs(jnp.exp(a_c), expand)
    cdec_x = _dot3_lhs(jnp.broadcast_to(jnp.exp(a_last), (8, SSD_HEADS)), expand)[0:1]

    xsdt = conv_ref[:, 0:SSD_INNER] * dt_x
    xsdt16 = xsdt.astype(BF16)
    xdec16 = (xsdt * dec_x).astype(BF16)
    lane128 = lax.broadcasted_iota(jnp.int32, (CHUNK, 128), 1)
    heads_per_group = SSD_HEADS // SSD_GROUPS
    gw = heads_per_group * SSD_HEAD_DIM
    for g in range(SSD_GROUPS):
        Bg = conv_ref[:, SSD_INNER + g * SSD_STATE:SSD_INNER + (g + 1) * SSD_STATE]
        Cg16 = conv_ref[:, SSD_INNER + 256 + g * SSD_STATE:SSD_INNER + 256 + (g + 1) * SSD_STATE].astype(BF16)
        CB = _dot_nt(Cg16, Bg.astype(BF16))
        S_g = S_ref[:, g * gw:(g + 1) * gw]
        y_off = _dot(Cg16, S_g.astype(BF16)) * ein_x[:, g * gw:(g + 1) * gw]
        for pr in range(heads_per_group // 2):
            col0 = g * gw + pr * 128
            xpair = xsdt16[:, col0:col0 + 128]
            acc = None
            for half in range(2):
                h = g * heads_per_group + pr * 2 + half
                seg = a_c[:, h:h + 1] - a_r[h:h + 1, :]
                Lm = jnp.exp(jnp.where(mask, seg, NEG_INF))
                Mh = (CB * Lm).astype(BF16)
                keep = (lane128 < 64) if half == 0 else (lane128 >= 64)
                part = _dot(Mh, jnp.where(keep, xpair, jnp.zeros_like(xpair)))
                acc = part if acc is None else acc + part
            emit(col0, 128, acc + y_off[:, pr * 128:(pr + 1) * 128])
        S_ref[:, g * gw:(g + 1) * gw] = (cdec_x[:, g * gw:(g + 1) * gw] * S_g
                                         + _dot(Bg.T.astype(BF16), xdec16[:, g * gw:(g + 1) * gw]))

    for h in range(MLSTM_HEADS):
        gi = COL_IG + d * MLSTM_HEADS + h
        gf = COL_FG + d * MLSTM_HEADS + h
        qh16 = conv_ref[:, 1536 + h * 128:1536 + (h + 1) * 128].astype(BF16)
        kh = conv_ref[:, 2048 + h * 128:2048 + (h + 1) * 128] * (MLSTM_QK_DIM ** -0.5)
        kh16 = kh.astype(BF16)
        vh16 = v_ref[:, h * MLSTM_V_DIM:(h + 1) * MLSTM_V_DIM]
        ig_c, ig_r = G[:, gi:gi + 1], GT[gi:gi + 1, :]
        b_c, b_r = cum_c[:, gf:gf + 1], cum_r[gf:gf + 1, :]
        b_last = b_c[last:last + 1, :]
        Dlog = jnp.where(mask, b_c - b_r + ig_r, NEG_INF)
        m_loc = jnp.max(b_last - b_r + ig_r, axis=-1, keepdims=True)
        ek = jnp.exp(b_last - b_c + ig_c - m_loc) * kh
        C_loc = _dot(ek.T.astype(BF16), vh16)
        n_loc = jnp.sum(ek, axis=0, keepdims=True)
        C_in = C_ref[h]
        n_in = n_ref[h:h + 1, :]
        m_in = m_ref[h:h + 1, 0:1]
        m_inter = b_c + m_in
        m_t = jnp.maximum(m_inter, jnp.max(Dlog, axis=-1, keepdims=True))
        P = jnp.exp(Dlog - m_t) * _dot_nt(qh16, kh16)
        w_inter = jnp.exp(m_inter - m_t)
        num = _dot(P.astype(BF16), vh16) + w_inter * _dot(qh16, C_in.astype(BF16))
        qn = jnp.sum(qh16.astype(F32) * n_in, axis=-1, keepdims=True)
        den = jnp.sum(P, axis=-1, keepdims=True) + w_inter * qn
        den = jnp.maximum(jnp.abs(den), jnp.exp(-m_t))
        emit(SSD_INNER + h * MLSTM_V_DIM, MLSTM_V_DIM, num / den)
        m_new = jnp.maximum(b_last + m_in, m_loc)
        fa = jnp.exp(b_last + m_in - m_new)
        fb = jnp.exp(m_loc - m_new)
        C_ref[h] = fa * C_in + fb * C_loc
        n_ref[h:h + 1, :] = fa * n_in + fb * n_loc
        m_ref[h:h + 1, :] = jnp.broadcast_to(m_new, (1, 128))


def _mixer_scan(rec, gates, gates_t, conv_w, conv_b, prow, pcol, extra, ctx_len, reverse):
    B, L, _ = rec.shape
    nc, nct = L // CHUNK, ctx_len // CHUNK
    nb = MIXER_BATCH
    hb = CHUNK // HALO
    wide = SSD_INNER + MLSTM_INNER

    def chunk_of(i):
        return jnp.where(i < nct, nct - 1 - i, nc - 1 - (i - nct)) if reverse else i

    const = lambda r, n: pl.BlockSpec((r, n), lambda b, i: (0, 0))
    chunk = lambda n: pl.BlockSpec((nb, CHUNK, n), lambda b, i: (b, chunk_of(i), 0))
    extra_spec = chunk(wide) if reverse else const(1, SSD_INNER)
    return pl.pallas_call(
        functools.partial(_mixer_kernel, reverse=reverse, nct=nct, nc=nc, nb=nb),
        grid=(B // nb, nc),
        in_specs=[chunk(REC_W),
                  pl.BlockSpec((nb, HALO, REC_W), lambda b, i: (b, jnp.maximum(chunk_of(i) * hb - 1, 0), 0)),
                  pl.BlockSpec((nb, HALO, REC_W),
                               lambda b, i: (b, jnp.minimum((chunk_of(i) + 1) * hb, L // HALO - 1), 0)),
                  chunk(GATE_W),
                  pl.BlockSpec((nb, N_GATE, CHUNK), lambda b, i: (b, 0, chunk_of(i))),
                  const(3, CONV_CH), const(1, CONV_CH), const(2, GATE_W), const(N_GATE, 2), extra_spec],
        out_specs=chunk(wide),
        out_shape=jax.ShapeDtypeStruct((B, L, wide), BF16 if reverse else F32),
        scratch_shapes=[pltpu.VMEM((nb, CHUNK, CONV_CH), F32),
                        pltpu.VMEM((nb, SSD_STATE, SSD_INNER), F32),
                        pltpu.VMEM((nb, MLSTM_HEADS, MLSTM_QK_DIM, MLSTM_V_DIM), F32),
                        pltpu.VMEM((nb, 8, MLSTM_QK_DIM), F32),
                        pltpu.VMEM((nb, 8, 128), F32)],
        compiler_params=_cparams("parallel", "arbitrary"),
        name="mixer_bwd" if reverse else "mixer_fwd",
    )(rec, rec, rec, gates, gates_t, conv_w, conv_b, prow, pcol, extra)


def _group_rms(x, groups):
    w = x.shape[-1] // groups
    parts = []
    for g in range(groups):
        seg = x[:, g * w:(g + 1) * w]
        parts.append(seg * lax.rsqrt(jnp.mean(seg * seg, axis=-1, keepdims=True) + EPS))
    return jnp.concatenate(parts, axis=-1)


def _even_out_kernel(y_ref, zo_ref, h_ref, mod_ref, snw_ref, mnw_ref, w_ref, o_ref):
    y = y_ref[0].astype(F32)
    z = zo_ref[0, :, 0:SSD_INNER].astype(F32)
    o = zo_ref[0, :, SSD_INNER:].astype(F32)
    ys = _group_rms(y[:, 0:SSD_INNER] * _silu(z), SSD_GROUPS) * snw_ref[...]
    hm = _group_rms(y[:, SSD_INNER:], MLSTM_HEADS) * mnw_ref[...] * _sigmoid(o)
    mix = jnp.concatenate([ys, hm], axis=-1).astype(BF16)
    o_ref[0] = h_ref[0] + mod_ref[0, 0][2:3] * _dot(mix, w_ref[...])


def _even_out(y, zo, h, mod, snw, mnw, w_out, ctx_len):
    B, L, d = h.shape
    tm = ROW_TILE
    wide = SSD_INNER + MLSTM_INNER
    row = lambda n: pl.BlockSpec((1, tm, n), lambda b, i: (b, i, 0))
    const = lambda r, n: pl.BlockSpec((r, n), lambda b, i: (0, 0))
    return pl.pallas_call(
        _even_out_kernel,
        grid=(B, L // tm),
        in_specs=[row(wide), row(wide), row(d), _mod_spec(ctx_len // tm),
                  const(1, SSD_INNER), const(1, MLSTM_INNER), const(wide, d)],
        out_specs=row(d),
        out_shape=jax.ShapeDtypeStruct((B, L, d), F32),
        compiler_params=_cparams("parallel", "parallel"),
        name="even_out",
    )(y, zo, h, mod, snw.reshape(1, -1), mnw.reshape(1, -1), w_out)


def _qkv_prep_kernel(p_ref, cs_ref, sn_ref, qnw_ref, knw_ref,
                     qs_ref, ks_ref, vs1_ref, qd_ref, kd_ref, vd1_ref, *, nct):
    is_lat = pl.program_id(1) >= nct
    p = p_ref[0].astype(F32)
    cs, sn = cs_ref[...], sn_ref[...]

    def rope(x):
        w = x.shape[1]
        reps = w // 128
        c = jnp.concatenate([cs] * reps, axis=1) if reps > 1 else cs
        s = jnp.concatenate([sn] * reps, axis=1) if reps > 1 else sn
        lane = lax.broadcasted_iota(jnp.int32, x.shape, 1)
        partner = jnp.where(lane % 2 == 0, pltpu.roll(x, w - 1, 1), pltpu.roll(x, 1, 1))
        return jnp.where(is_lat, x * c + partner * s, x)

    def head_rms(x, nw):
        w = x.shape[1]
        a = lax.broadcasted_iota(jnp.int32, (w, w), 0) // ATT_DH
        b = lax.broadcasted_iota(jnp.int32, (w, w), 1) // ATT_DH
        same = (a == b).astype(BF16)
        ms = _dot3_lhs(x * x, same) * (1.0 / ATT_DH)
        return x * lax.rsqrt(ms + EPS) * nw

    qw, kw = ATT_HEADS * ATT_DH, ATT_KV * ATT_DH
    o = 0
    q_s = rope(p[:, o:o + qw]) * (ATT_DH ** -0.5 * LOG2E); o += qw
    k_s = rope(p[:, o:o + kw]); o += kw
    v_s = p[:, o:o + kw]; o += kw
    q_d = rope(head_rms(p[:, o:o + qw], qnw_ref[...])) * (ATT_DH ** -0.5 * LOG2E); o += qw
    k_d = rope(head_rms(p[:, o:o + kw], knw_ref[...])); o += kw
    v_d = p[:, o:o + kw]

    lane = lax.broadcasted_iota(jnp.int32, (QBLK, 128), 1)
    for h in range(ATT_HEADS):
        qs_ref[0, h] = q_s[:, h * ATT_DH:(h + 1) * ATT_DH].astype(BF16)
        qd_ref[0, h] = q_d[:, h * ATT_DH:(h + 1) * ATT_DH].astype(BF16)
    for kv in range(ATT_KV):
        ks_ref[0, kv] = k_s[:, kv * ATT_DH:(kv + 1) * ATT_DH].astype(BF16)
        kd_ref[0, kv] = k_d[:, kv * ATT_DH:(kv + 1) * ATT_DH].astype(BF16)
    for v, v1_ref in ((v_s, vs1_ref), (v_d, vd1_ref)):
        for kv in range(ATT_KV):
            own = jnp.where((lane // ATT_DH) == kv, v, 0.0)
            low = own if kv == 0 else pltpu.roll(own, ATT_DH, 1)
            v1_ref[0, kv] = jnp.where(lane == ATT_DH, 1.0, low).astype(BF16)


def _qkv_prep(proj, cosf, sinf, qnw, knw, ctx_len):
    B, L, _ = proj.shape
    nct = ctx_len // QBLK
    tab = pl.BlockSpec((QBLK, 128), lambda b, i: (jnp.maximum(i - nct, 0), 0))
    head = lambda n, w: pl.BlockSpec((1, n, QBLK, w), lambda b, i: (b, 0, i, 0))
    shp = lambda n, w: jax.ShapeDtypeStruct((B, n, L, w), BF16)
    return pl.pallas_call(
        functools.partial(_qkv_prep_kernel, nct=nct),
        grid=(B, L // QBLK),
        in_specs=[pl.BlockSpec((1, QBLK, ODD_PROJ), lambda b, i: (b, i, 0)), tab, tab,
                  pl.BlockSpec((1, ATT_HEADS * ATT_DH), lambda b, i: (0, 0)),
                  pl.BlockSpec((1, ATT_KV * ATT_DH), lambda b, i: (0, 0))],
        out_specs=[head(ATT_HEADS, ATT_DH), head(ATT_KV, ATT_DH), head(ATT_KV, 128)] * 2,
        out_shape=[shp(ATT_HEADS, ATT_DH), shp(ATT_KV, ATT_DH), shp(ATT_KV, 128)] * 2,
        compiler_params=_cparams("parallel", "parallel"),
        name="qkv_prep",
    )(proj, cosf, sinf, qnw, knw)


def _group_softmax_out(scores, values, rows, sink=None):
    m = functools.reduce(jnp.maximum, [jnp.max(s, axis=-1, keepdims=True) for s in scores])
    if sink is not None:
        m = jnp.maximum(m, sink)
    probs = [jnp.exp2(s - m).astype(BF16) for s in scores]
    lane = lax.broadcasted_iota(jnp.int32, (rows, 128), 1)
    outs = []
    for pr in range(ATT_G // 2):
        heads = []
        for g in (2 * pr, 2 * pr + 1):
            rs = slice(g * rows, (g + 1) * rows)
            O = functools.reduce(lambda a, b: a + b, [_dot(p[rs], v) for p, v in zip(probs, values)])
            l = O[:, ATT_DH:ATT_DH + 1]
            if sink is not None:
                l = l + jnp.exp2(sink[rs] - m[rs])
            heads.append(O * (1.0 / l))
        outs.append(jnp.where(lane < ATT_DH, heads[0], pltpu.roll(heads[1], ATT_DH, 1)))
    return jnp.concatenate(outs, axis=-1)


def _dense_attn_kernel(q_ref, k_ref, v1_ref, o_ref):
    for kv in range(ATT_KV):
        Q = q_ref[0, kv * ATT_G:(kv + 1) * ATT_G].reshape(ATT_G * QBLK, ATT_DH)
        o_ref[0, :, kv * ATT_G * ATT_DH:(kv + 1) * ATT_G * ATT_DH] = _group_softmax_out(
            [_dot_nt(Q, k_ref[0, kv])], [v1_ref[0, kv]], QBLK).astype(o_ref.dtype)


def _dense_attn(qd, kd, vd1, ctx_len):
    B, _, L, _ = qd.shape
    S = L - ctx_len
    nct = ctx_len // QBLK
    full = lambda w: pl.BlockSpec((1, ATT_KV, L, w), lambda b, j: (b, 0, 0, 0))
    return pl.pallas_call(
        _dense_attn_kernel,
        grid=(B, S // QBLK),
        in_specs=[pl.BlockSpec((1, ATT_HEADS, QBLK, ATT_DH), lambda b, j: (b, 0, j + nct, 0)),
                  full(ATT_DH), full(128)],
        out_specs=pl.BlockSpec((1, QBLK, ATT_HEADS * ATT_DH), lambda b, j: (b, j, 0)),
        out_shape=jax.ShapeDtypeStruct((B, S, ATT_HEADS * ATT_DH), BF16),
        compiler_params=_cparams("parallel", "parallel"),
        name="dense_attn",
    )(qd, kd, vd1)


def _sink_rows(sink_ref, kv, rows):
    return jnp.concatenate(
        [jnp.broadcast_to(sink_ref[kv * ATT_G + g:kv * ATT_G + g + 1, 0:1], (rows, 1)) for g in range(ATT_G)], axis=0)


def _window_attn_kernel(q_ref, k_ref, v1_ref, sink_ref, o_ref, *, ctx_len, seq):
    j = pl.program_id(1)
    span = 3 * QBLK
    start = jnp.clip((j - 1) * QBLK, 0, seq - span)
    kstart = pl.multiple_of(ctx_len + start, QBLK)
    rows = ATT_G * QBLK
    qpos = j * QBLK + lax.broadcasted_iota(jnp.int32, (rows, span), 0) % QBLK
    kpos = start + lax.broadcasted_iota(jnp.int32, (rows, span), 1)
    valid = jnp.abs(kpos - qpos) <= WINDOW
    for kv in range(ATT_KV):
        Q = q_ref[0, kv * ATT_G:(kv + 1) * ATT_G].reshape(rows, ATT_DH)
        Sc = _dot_nt(Q, k_ref[0, kv, 0:ctx_len])
        Sl = jnp.where(valid, _dot_nt(Q, k_ref[0, kv, pl.ds(kstart, span)]), NEG_INF)
        out = _group_softmax_out([Sc, Sl], [v1_ref[0, kv, 0:ctx_len], v1_ref[0, kv, pl.ds(kstart, span)]], QBLK,
                                 sink=_sink_rows(sink_ref, kv, QBLK) * LOG2E)
        o_ref[0, :, kv * ATT_G * ATT_DH:(kv + 1) * ATT_G * ATT_DH] = out.astype(o_ref.dtype)


def _window_attn(qs, ks, vs1, sink_b, ctx_len):
    B, _, L, _ = qs.shape
    S = L - ctx_len
    nct = ctx_len // QBLK
    full = lambda w: pl.BlockSpec((1, ATT_KV, L, w), lambda b, j: (b, 0, 0, 0))
    return pl.pallas_call(
        functools.partial(_window_attn_kernel, ctx_len=ctx_len, seq=S),
        grid=(B, S // QBLK),
        in_specs=[pl.BlockSpec((1, ATT_HEADS, QBLK, ATT_DH), lambda b, j: (b, 0, j + nct, 0)),
                  full(ATT_DH), full(128),
                  pl.BlockSpec((ATT_HEADS, 128), lambda b, j: (0, 0))],
        out_specs=pl.BlockSpec((1, QBLK, ATT_HEADS * ATT_DH), lambda b, j: (b, j, 0)),
        out_shape=jax.ShapeDtypeStruct((B, S, ATT_HEADS * ATT_DH), BF16),
        compiler_params=_cparams("parallel", "parallel"),
        name="window_attn",
    )(qs, ks, vs1, sink_b)


def _ctx_attn_kernel(qs_ref, ks_ref, vs1_ref, qd_ref, kd_ref, vd1_ref, sink_ref, os_ref, od_ref, *, ctx_len):
    T = ctx_len
    for kv in range(ATT_KV):
        cols = slice(kv * ATT_G * ATT_DH, (kv + 1) * ATT_G * ATT_DH)
        Q = qs_ref[0, kv * ATT_G:(kv + 1) * ATT_G].reshape(ATT_G * T, ATT_DH)
        os_ref[0, :, cols] = _group_softmax_out([_dot_nt(Q, ks_ref[0, kv])], [vs1_ref[0, kv]], T,
                                                sink=_sink_rows(sink_ref, kv, T) * LOG2E).astype(os_ref.dtype)
        Q = qd_ref[0, kv * ATT_G:(kv + 1) * ATT_G].reshape(ATT_G * T, ATT_DH)
        od_ref[0, :, cols] = _group_softmax_out([_dot_nt(Q, kd_ref[0, kv])], [vd1_ref[0, kv]], T).astype(od_ref.dtype)


def _ctx_attn(qs, ks, vs1, qd, kd, vd1, sink_b, ctx_len):
    B = qs.shape[0]
    T = ctx_len
    blk = lambda n, w: pl.BlockSpec((1, n, T, w), lambda b: (b, 0, 0, 0))
    out = pl.BlockSpec((1, T, ATT_HEADS * ATT_DH), lambda b: (b, 0, 0))
    return pl.pallas_call(
        functools.partial(_ctx_attn_kernel, ctx_len=T),
        grid=(B,),
        in_specs=[blk(ATT_HEADS, ATT_DH), blk(ATT_KV, ATT_DH), blk(ATT_KV, 128),
                  blk(ATT_HEADS, ATT_DH), blk(ATT_KV, ATT_DH), blk(ATT_KV, 128),
                  pl.BlockSpec((ATT_HEADS, 128), lambda b: (0, 0))],
        out_specs=[out, out],
        out_shape=[jax.ShapeDtypeStruct((B, T, ATT_HEADS * ATT_DH), BF16)] * 2,
        compiler_params=_cparams("parallel"),
        name="ctx_attn",
    )(qs, ks, vs1, qd, kd, vd1, sink_b)


def _odd_out_kernel(ysc_ref, ydc_ref, ysl_ref, ydl_ref, h_ref, mod_ref, w_ref, o_ref, *, nct):
    is_ctx = pl.program_id(1) < nct
    ys = jnp.where(is_ctx, ysc_ref[0], ysl_ref[0])
    yd = jnp.where(is_ctx, ydc_ref[0], ydl_ref[0])
    mix = jnp.concatenate([ys, yd], axis=-1)
    o_ref[0] = h_ref[0] + mod_ref[0, 0][2:3] * _dot(mix, w_ref[...])


def _odd_out(ys_ctx, yd_ctx, ys_lat, yd_lat, h, mod, w_out, ctx_len):
    B, L, d = h.shape
    tm = ROW_TILE
    nct = ctx_len // tm
    half = ATT_HEADS * ATT_DH
    row = lambda n: pl.BlockSpec((1, tm, n), lambda b, i: (b, i, 0))
    ctx = pl.BlockSpec((1, tm, half), lambda b, i: (b, jnp.minimum(i, nct - 1), 0))
    lat = pl.BlockSpec((1, tm, half), lambda b, i: (b, jnp.maximum(i - nct, 0), 0))
    return pl.pallas_call(
        functools.partial(_odd_out_kernel, nct=nct),
        grid=(B, L // tm),
        in_specs=[ctx, ctx, lat, lat, row(d), _mod_spec(nct), pl.BlockSpec((2 * half, d), lambda b, i: (0, 0))],
        out_specs=row(d),
        out_shape=jax.ShapeDtypeStruct((B, L, d), F32),
        compiler_params=_cparams("parallel", "parallel"),
        name="odd_out",
    )(ys_ctx, yd_ctx, ys_lat, yd_lat, h, mod, w_out)


def _first_lane_of(cond, lane):
    return jnp.min(jnp.where(cond, lane, ROUTE_W), axis=-1, keepdims=True)


def _router_kernel(h_ref, mod_ref, nw_ref, wr_ref, a_ref, c_ref):
    a = _norm_mod(h_ref[0], nw_ref[...], mod_ref[0, 0], 3, 4)
    a_hi = a.astype(BF16)
    a_ref[0, :, :D_MODEL] = a_hi
    a_lo = (a - a_hi.astype(F32)).astype(BF16)
    w = wr_ref[...]
    w_hi = w.astype(BF16)
    w_lo = (w - w_hi.astype(F32)).astype(BF16)
    logits = _dot(a_hi, w_hi) + (_dot(a_lo, w_hi) + _dot(a_hi, w_lo))
    lane = lax.broadcasted_iota(jnp.int32, logits.shape, 1)
    gl = jnp.where(lane < N_GROUPS, logits, NEG_INF)
    gmax = jnp.max(gl, axis=-1, keepdims=True)
    p_top = 1.0 / jnp.sum(jnp.exp(gl - gmax), axis=-1, keepdims=True)
    g_idx = _first_lane_of(gl == gmax, lane)
    in_group = jnp.logical_and(lane >= EXPERT_LANE0, (lane - EXPERT_LANE0) // EXPERTS_PER_GROUP == g_idx)
    in_group = jnp.logical_and(in_group, lane < EXPERT_LANE0 + N_EXPERTS)
    el = jnp.where(in_group, logits, NEG_INF)
    m1 = jnp.max(el, axis=-1, keepdims=True)
    i1 = _first_lane_of(el == m1, lane)
    el2 = jnp.where(lane == i1, NEG_INF, el)
    m2 = jnp.max(el2, axis=-1, keepdims=True)
    i2 = _first_lane_of(el2 == m2, lane)
    e2 = jnp.exp(m2 - m1)
    w1 = 1.0 / (1.0 + e2) * p_top
    w2 = e2 / (1.0 + e2) * p_top
    j1 = i1 - EXPERT_LANE0 - EXPERTS_PER_GROUP * g_idx
    j2 = i2 - EXPERT_LANE0 - EXPERTS_PER_GROUP * g_idx
    both = lambda j, w: jnp.where(jnp.logical_or(lane == j, lane == j + EXPERTS_PER_GROUP), w, 0.0)
    gate = both(j1, w1) + both(j2, w2)
    gate_hi = gate.astype(BF16).astype(F32)
    tail = jnp.where(lane < EXPERTS_PER_GROUP, gate_hi, gate - gate_hi)
    tail = jnp.where(lane == GIDX_LANE, g_idx.astype(F32), tail)
    a_ref[0, :, D_MODEL:] = tail.astype(BF16)
    counts =jnp.sum((lane == g_idx).astype(jnp.int32), axis=0, keepdims=True)
    c_ref[0, 0] = jnp.broadcast_to(counts, (8, ROUTE_W))


def _router(h, mod, nw, w_route, ctx_len):
    B, L, d = h.shape
    tm = ROW_TILE
    row = lambda n: pl.BlockSpec((1, tm, n), lambda b, i: (b, i, 0))
    return pl.pallas_call(
        _router_kernel,
        grid=(B, L // tm),
        in_specs=[row(d), _mod_spec(ctx_len // tm), pl.BlockSpec((1, d), lambda b, i: (0, 0)),
                  pl.BlockSpec((d, ROUTE_W), lambda b, i: (0, 0))],
        out_specs=[row(d + ROUTE_W), pl.BlockSpec((1, 1, 8, ROUTE_W), lambda b, i: (b, i, 0, 0))],
        out_shape=[jax.ShapeDtypeStruct((B, L, d + ROUTE_W), BF16),
                   jax.ShapeDtypeStruct((B, L // tm, 8, ROUTE_W), jnp.int32)],
        compiler_params=_cparams("parallel", "parallel"),
        name="moe_router",
    )(h, mod, nw.reshape(1, d), w_route)


def _route_plan(counts, n_row_tiles):
    pc = (counts + RUN_ALIGN - 1) // RUN_ALIGN * RUN_ALIGN
    region = (jnp.sum(pc, axis=0) + EXPERT_TILE - 1) // EXPERT_TILE * EXPERT_TILE
    region_end = jnp.cumsum(region)
    off = (region_end - region)[None, :] + jnp.cumsum(pc, axis=0) - pc
    n_used = (region_end[-1] // EXPERT_TILE).reshape(1)
    tile_row0 = jnp.arange(n_row_tiles, dtype=jnp.int32) * EXPERT_TILE
    tile_group = jnp.minimum(jnp.sum(tile_row0[:, None] >= region_end[None, :], axis=1), N_GROUPS - 1)
    return (pc.reshape(-1).astype(jnp.int32), off.reshape(-1).astype(jnp.int32),
            tile_group.astype(jnp.int32), n_used.astype(jnp.int32))


def _run_bases(pc_ref, t):
    bases, base = [], 0
    for g in range(N_GROUPS):
        bases.append(base)
        base = base + pc_ref[t * N_GROUPS + g]
    return bases


def _tile_slots(G, bases):
    tm = G.shape[0]
    lane = lax.broadcasted_iota(jnp.int32, (tm, ROUTE_W), 1)
    g_idx = G[:, GIDX_LANE:GIDX_LANE + 1].astype(jnp.int32)
    onehot = (lane == g_idx).astype(BF16)
    ti = lax.broadcasted_iota(jnp.int32, (tm, tm), 0)
    tj = lax.broadcasted_iota(jnp.int32, (tm, tm), 1)
    earlier = _dot((tj < ti).astype(BF16), onehot)
    slot = jnp.sum(jnp.where(lane == g_idx, earlier, 0.0), axis=-1, keepdims=True).astype(jnp.int32)
    for g in range(N_GROUPS):
        slot = slot + jnp.where(g_idx == g, bases[g], 0)
    return slot


def _run_copies(pc_ref, off_ref, t, buf, pairs, sems, to_sorted, live):
    bases = _run_bases(pc_ref, t)
    out = []
    for g in range(N_GROUPS):
        n = pc_ref[t * N_GROUPS + g]
        dst0 = off_ref[t * N_GROUPS + g]
        for k, size in enumerate(RUN_PIECES):
            po = n & ~(2 * size - 1)
            present = jnp.logical_and(live, (n & size) != 0)
            v0 = pl.multiple_of(bases[g] + po, RUN_ALIGN)
            h0 = pl.multiple_of(dst0 + po, RUN_ALIGN)
            for a, (v_ref, h_ref) in enumerate(pairs):
                v, hb = v_ref.at[buf, pl.ds(v0, size)], h_ref.at[pl.ds(h0, size)]
                src, dst = (v, hb) if to_sorted else (hb, v)
                out.append((present, pltpu.make_async_copy(src, dst, sems.at[buf, a, g, k])))
    return out


def _start(copies):
    for present, cp in copies:
        pl.when(present)(cp.start)


def _wait(copies):
    for present, cp in copies:
        pl.when(present)(cp.wait)


def _dispatch_kernel(pc_ref, off_ref, a_ref, xs_in, xs_ref, sx_ref, sems):
    del xs_in
    n_tiles = pl.num_programs(0) * pl.num_programs(1)
    t = pl.program_id(0) * pl.num_programs(1) + pl.program_id(1)
    buf = t % 2
    G = a_ref[0, :, D_MODEL:].astype(F32)
    slot = _tile_slots(G, _run_bases(pc_ref, t))
    tm = G.shape[0]
    lane = lax.broadcasted_iota(jnp.int32, (tm, SORT_ROWS), 1)
    perm = (lane == slot).astype(F32).T.astype(BF16)
    sx_ref[buf] = _dot(perm, a_ref[0]).astype(BF16)
    pairs = [(sx_ref, xs_ref)]
    _start(_run_copies(pc_ref, off_ref, t, buf, pairs, sems, True, True))
    _wait(_run_copies(pc_ref, off_ref, jnp.maximum(t - 1, 0), 1 - buf, pairs, sems, True, t >= 1))
    _wait(_run_copies(pc_ref, off_ref, t, buf, pairs, sems, True, t == n_tiles - 1))


def _dispatch(a2, pc, off, n_rows):
    B, L, w = a2.shape
    tm = ROW_TILE
    anyspec = pl.BlockSpec(memory_space=pl.ANY)
    return pl.pallas_call(
        _dispatch_kernel,
        grid_spec=pltpu.PrefetchScalarGridSpec(
            num_scalar_prefetch=2, grid=(B, L // tm),
            in_specs=[pl.BlockSpec((1, tm, w), lambda b, i, *_: (b, i, 0)), anyspec],
            out_specs=anyspec,
            scratch_shapes=[pltpu.VMEM((2, SORT_ROWS, w), BF16),
                            pltpu.SemaphoreType.DMA((2, 1, N_GROUPS, len(RUN_PIECES)))]),
        out_shape=jax.ShapeDtypeStruct((n_rows, w), BF16),
        input_output_aliases={3: 0},
        compiler_params=_cparams("arbitrary", "arbitrary"),
        name="moe_dispatch",
    )(pc, off, a2, jnp.zeros((n_rows, w), BF16))


def _group_experts_kernel(tg_ref, nu_ref, x_ref, wg_ref, wu_ref, wd_ref, y_ref):
    del tg_ref
    used = pl.program_id(0) < nu_ref[0]

    @pl.when(jnp.logical_not(used))
    def _():
        y_ref[...] = jnp.zeros_like(y_ref)

    @pl.when(used)
    def _():
        x = x_ref[:, :D_MODEL]
        hid = _silu(_dot(x, wg_ref[0])) * _dot(x, wu_ref[0])
        tail = x_ref[:, D_MODEL:].astype(F32)
        gates = tail[:, 0:EXPERTS_PER_GROUP] + tail[:, EXPERTS_PER_GROUP:2 * EXPERTS_PER_GROUP]
        hid16 = jnp.concatenate(
            [(hid[:, e * EXPERT_FF:(e + 1) * EXPERT_FF] * gates[:, e:e + 1]).astype(BF16)
             for e in range(EXPERTS_PER_GROUP)], axis=-1)
        y_ref[...] = _dot(hid16, wd_ref[0]).astype(y_ref.dtype)


def _group_experts(xs, tile_group, n_used, wg, wu, wd):
    n_rows, w = xs.shape
    d = D_MODEL
    tm = EXPERT_TILE
    ff = EXPERTS_PER_GROUP * EXPERT_FF
    wspec = lambda r, c: pl.BlockSpec((1, r, c), lambda i, tg, nu: (tg[jnp.minimum(i, nu[0] - 1)], 0, 0))
    return pl.pallas_call(
        _group_experts_kernel,
        grid_spec=pltpu.PrefetchScalarGridSpec(
            num_scalar_prefetch=2, grid=(n_rows // tm,),
            in_specs=[pl.BlockSpec((tm, w), lambda i, tg, nu: (jnp.minimum(i, nu[0] - 1), 0)),
                      wspec(d, ff), wspec(d, ff), wspec(ff, d)],
            out_specs=pl.BlockSpec((tm, d), lambda i, tg, nu: (i, 0))),
        out_shape=jax.ShapeDtypeStruct((n_rows, d), BF16),
        compiler_params=_cparams("arbitrary"),
        name="moe_experts",
    )(tile_group, n_used, xs, wg, wu, wd)


def _window_copies(pc_ref, off_ref, t, buf, sy_ref, ys_ref, sems, live):
    out = []
    for g in range(N_GROUPS):
        present = jnp.logical_and(live, pc_ref[t * N_GROUPS + g] > 0)
        h0 = pl.multiple_of(off_ref[t * N_GROUPS + g], RUN_ALIGN)
        out.append((present, pltpu.make_async_copy(ys_ref.at[pl.ds(h0, ROW_TILE)],
                                                   sy_ref.at[buf, pl.ds(g * ROW_TILE, ROW_TILE)], sems.at[buf, g])))
    return out


def _combine_kernel(pc_ref, off_ref, g_ref, h_ref, mod_ref, fw_ref, ys_ref, o_ref, sy_ref, sems, *, final):
    n_tiles = pl.num_programs(0) * pl.num_programs(1)
    t = pl.program_id(0) * pl.num_programs(1) + pl.program_id(1)
    buf = t % 2

    @pl.when(t == 0)
    def _():
        sy_ref[...] = jnp.zeros_like(sy_ref)

    _start(_window_copies(pc_ref, off_ref, t, buf, sy_ref, ys_ref, sems, t == 0))
    _start(_window_copies(pc_ref, off_ref, jnp.minimum(t + 1, n_tiles - 1), 1 - buf, sy_ref, ys_ref, sems,
                          t + 1 < n_tiles))
    G = g_ref[0].astype(F32)
    tm = G.shape[0]
    slot = _tile_slots(G, [g * tm for g in range(N_GROUPS)])
    lane = lax.broadcasted_iota(jnp.int32, (tm, N_GROUPS * tm), 1)
    pick = (lane == slot).astype(BF16)
    _wait(_window_copies(pc_ref, off_ref, t, buf, sy_ref, ys_ref, sems, True))
    x = h_ref[0] + mod_ref[0, 0][5:6] * _dot(pick, sy_ref[buf])
    if final:
        x = x * lax.rsqrt(jnp.mean(x * x, axis=-1, keepdims=True) + EPS) * fw_ref[...]
    o_ref[0] = x


def _combine(ys, a2, h, mod, pc, off, final_w, ctx_len, final):
    B, L, d = h.shape
    tm = ROW_TILE
    nct = ctx_len // tm
    row = lambda n: pl.BlockSpec((1, tm, n), lambda b, i, *_: (b, i, 0))
    if final:
        out_spec = pl.BlockSpec((1, tm, d), lambda b, i, *_: (b, jnp.maximum(i - nct, 0), 0))
        out_shape = jax.ShapeDtypeStruct((B, L - ctx_len, d), F32)
    else:
        out_spec, out_shape = row(d), jax.ShapeDtypeStruct((B, L, d), F32)
    return pl.pallas_call(
        functools.partial(_combine_kernel, final=final),
        grid_spec=pltpu.PrefetchScalarGridSpec(
            num_scalar_prefetch=2, grid=(B, L // tm),
            in_specs=[pl.BlockSpec((1, tm, ROUTE_W), lambda b, i, *_: (b, i, d // ROUTE_W)),
                      row(d),
                      pl.BlockSpec((1, 1, 6, d), lambda b, i, *_: (b, (i >= nct).astype(jnp.int32), 0, 0)),
                      pl.BlockSpec((1, d), lambda b, i, *_: (0, 0)),
                      pl.BlockSpec(memory_space=pl.ANY)],
            out_specs=out_spec,
            scratch_shapes=[pltpu.VMEM((2, N_GROUPS * tm, d), BF16),
                            pltpu.SemaphoreType.DMA((2, N_GROUPS))]),
        out_shape=out_shape,
        compiler_params=_cparams("arbitrary", "arbitrary"),
        name="moe_combine",
    )(pc, off, a2, h, mod, final_w.reshape(1, d), ys)


def _rope_tables(seq):
    rows = seq // GRID_W
    row = jnp.repeat(jnp.arange(rows), GRID_W).astype(F32)
    col = (jnp.arange(rows * GRID_W) % GRID_W).astype(F32)
    axis_dim = ATT_DH // 2
    inv = ROPE_THETA ** (-jnp.arange(0, axis_dim, 2, dtype=F32) / axis_dim)
    ang = jnp.concatenate([row[:, None] * inv, col[:, None] * inv], axis=-1)
    cosf = jnp.repeat(jnp.cos(ang), 2, axis=-1)
    sinf = jnp.repeat(jnp.sin(ang), 2, axis=-1) * jnp.tile(jnp.array([-1.0, 1.0], F32), ATT_DH // 2)
    return jnp.tile(cosf, (1, 2)), jnp.tile(sinf, (1, 2))


def kernel(x, c, ctx, c_ctx, ada_w, ada_b, norm1_w, norm2_w, ev_w_in, ev_conv_w, ev_conv_b, ev_dt_bias, ev_a_log, ev_d_skip, ev_ssd_norm_w, ev_ig_bias, ev_fg_bias, ev_mlstm_norm_w, ev_w_out, od_w_in, od_sink, od_q_norm_w, od_k_norm_w, od_w_out, moe_w_group, moe_w_expert, moe_w_gate, moe_w_up, moe_w_down, final_norm_w):
    B, S, d = x.shape
    T = ctx.shape[1]
    depth = ada_w.shape[0]
    assert d == D_MODEL and T % ROW_TILE == 0 and S % ROW_TILE == 0 and S % GRID_W == 0 and S >= 3 * QBLK

    h = jnp.concatenate([ctx, x], axis=1)

    rows = -(-(B + 1) // 8) * 8
    cond = jnp.zeros((rows, d), F32).at[:B].set(c).at[B].set(c_ctx)
    ada = _adaln(cond, ada_w, ada_b).reshape(depth, rows, 6, d)
    mods = jnp.stack([jnp.broadcast_to(ada[:, B:B + 1], (depth, B, 6, d)), ada[:, :B]], axis=2)

    cosf, sinf = _rope_tables(S)

    for layer in range(depth):
        li = layer // 2
        mod = mods[layer]
        if layer % 2 == 0:
            w_in = ev_w_in[li]
            conv, z, v, o, gate_cols = (w_in[:, :CONV_CH], w_in[:, CONV_CH:CONV_CH + SSD_INNER],
                                        w_in[:, CONV_CH + SSD_INNER:CONV_CH + 2 * SSD_INNER],
                                        w_in[:, CONV_CH + 2 * SSD_INNER:CONV_CH + 3 * SSD_INNER],
                                        w_in[:, CONV_CH + 3 * SSD_INNER:])
            w_rec = jnp.concatenate([v, conv], axis=1).astype(BF16)
            w_zo = jnp.concatenate([z, o], axis=1).astype(BF16)
            w_g = jnp.pad(gate_cols, ((0, 0), (0, GATE_W - N_GATE))).astype(BF16)
            rec, zo, gates = _norm_mod_matmul(h, mod, norm1_w[layer], [w_rec, w_zo, w_g], [BF16, BF16, F32], 0, 1, T)
            gates_t = jnp.swapaxes(gates[:, :, :N_GATE], 1, 2)
            bias = jnp.concatenate([ev_dt_bias[li].reshape(-1), ev_ig_bias[li].reshape(-1), ev_fg_bias[li].reshape(-1)])
            alog = jnp.pad(ev_a_log[li].reshape(-1), (0, N_GATE - 2 * SSD_HEADS))
            prow = jnp.pad(jnp.stack([bias, alog]), ((0, 0), (0, GATE_W - N_GATE)))
            pcol = jnp.stack([bias, alog], axis=1)
            dskip = jnp.repeat(ev_d_skip[li], SSD_HEAD_DIM).reshape(1, SSD_INNER)
            cb = ev_conv_b[li].reshape(1, CONV_CH)
            yf = _mixer_scan(rec, gates, gates_t, ev_conv_w[li], cb, prow, pcol, dskip, T, False)
            y = _mixer_scan(rec, gates, gates_t, ev_conv_w[li], cb, prow, pcol, yf, T, True)
            h = _even_out(y, zo, h, mod, ev_ssd_norm_w[li], ev_mlstm_norm_w[li], ev_w_out[li].astype(BF16), T)
        else:
            (proj,) = _norm_mod_matmul(h, mod, norm1_w[layer], [od_w_in[li].astype(BF16)], [BF16], 0, 1, T)
            qnw = jnp.tile(od_q_norm_w[li], ATT_HEADS).reshape(1, -1)
            knw = jnp.tile(od_k_norm_w[li], ATT_KV).reshape(1, -1)
            qs, ks, vs1, qd, kd, vd1 = _qkv_prep(proj, cosf, sinf, qnw, knw, T)
            sink_b = jnp.broadcast_to(od_sink[li].reshape(ATT_HEADS, 1), (ATT_HEADS, 128))
            ys_lat = _window_attn(qs, ks, vs1, sink_b, T)
            yd_lat = _dense_attn(qd, kd, vd1, T)
            ys_ctx, yd_ctx = _ctx_attn(qs, ks, vs1, qd, kd, vd1, sink_b, T)
            h = _odd_out(ys_ctx, yd_ctx, ys_lat, yd_lat, h, mod, od_w_out[li].astype(BF16), T)

        w_route = jnp.pad(jnp.concatenate([moe_w_group[layer], moe_w_expert[layer]], axis=1),
                          ((0, 0), (0, ROUTE_W - N_GROUPS - N_EXPERTS)))
        a2, counts = _router(h, mod, norm2_w[layer], w_route, T)
        n_tiles = B * ((T + S) // ROW_TILE)
        n_rows = -(-(n_tiles * (ROW_TILE + N_GROUPS * RUN_ALIGN) + N_GROUPS * EXPERT_TILE + ROW_TILE)
                   // EXPERT_TILE) * EXPERT_TILE
        pc, off, tile_group, n_used = _route_plan(counts[:, :, 0, :N_GROUPS].reshape(n_tiles, N_GROUPS),
                                                  n_rows // EXPERT_TILE)
        xs = _dispatch(a2, pc, off, n_rows)
        ff = EXPERTS_PER_GROUP * EXPERT_FF
        wg = moe_w_gate[layer].transpose(0, 2, 1, 3).reshape(N_GROUPS, d, ff).astype(BF16)
        wu = moe_w_up[layer].transpose(0, 2, 1, 3).reshape(N_GROUPS, d, ff).astype(BF16)
        wd = moe_w_down[layer].reshape(N_GROUPS, ff, d).astype(BF16)
        ys = _group_experts(xs, tile_group, n_used, wg, wu, wd)
        h = _combine(ys, a2, h, mod, pc, off, final_norm_w, T, final=layer == depth - 1)

    return h
```

```python
import functools
import math

import jax
import jax.numpy as jnp
from jax import lax
from jax.experimental import pallas as pl
from jax.experimental.pallas import tpu as pltpu

F32 = jnp.float32
BF16 = jnp.bfloat16
EPS = 1e-6
NEG_INF = float("-inf")
LOG2E = 1.4426950408889634

D_MODEL = 1024
GRID_W = 64
ROPE_THETA = 10000.0

SSD_HEADS = 16
SSD_HEAD_DIM = 64
SSD_INNER = 1024
SSD_GROUPS = 2
SSD_STATE = 128
MLSTM_HEADS = 4
MLSTM_QK_DIM = 128
MLSTM_V_DIM = 256
MLSTM_QK = 512
MLSTM_INNER = 1024
CHUNK = 128
MIXER_BATCH = 2
HALO = 16
CONV_CH = 2560
REC_W = MLSTM_INNER + CONV_CH
N_GATE = 48
GATE_W = 128
COL_DT, COL_IG, COL_FG = 0, 32, 40

ATT_DH = 64
ATT_HEADS = 8
ATT_KV = 2
ATT_G = ATT_HEADS // ATT_KV
WINDOW = 128
QBLK = 128
ODD_PROJ = 1536

N_GROUPS = 4
EXPERTS_PER_GROUP = 4
N_EXPERTS = N_GROUPS * EXPERTS_PER_GROUP
EXPERT_FF = 256
ROUTE_W = 128
EXPERT_LANE0 = N_GROUPS

ROW_TILE = 256
GIDX_LANE = 8
RUN_ALIGN = 16
RUN_PIECES = (256, 128, 64, 32, 16)
SORT_ROWS = -(-(ROW_TILE + N_GROUPS * (RUN_ALIGN - 1)) // 128) * 128
EXPERT_TILE = 512
VMEM_LIMIT = 56 * 1024 * 1024


def _cparams(*sem):
    return pltpu.CompilerParams(dimension_semantics=sem, vmem_limit_bytes=VMEM_LIMIT)


def _sigmoid(x):
    return 0.5 * jnp.tanh(0.5 * x) + 0.5


def _silu(x):
    return x * _sigmoid(x)


def _softplus(x):
    return jnp.maximum(x, 0.0) + jnp.log(1.0 + jnp.exp(-jnp.abs(x)))


def _dot(a, b):
    return jnp.dot(a, b, preferred_element_type=F32)


def _dot_nt(a, b):
    return lax.dot_general(a, b, (((1,), (1,)), ((), ())), preferred_element_type=F32)


def _split3(a):
    hi = a.astype(BF16)
    r1 = a - hi.astype(F32)
    mid = r1.astype(BF16)
    lo = (r1 - mid.astype(F32)).astype(BF16)
    return hi, mid, lo


def _dot3_rhs(exact, a):
    hi, mid, lo = _split3(a)
    return _dot(exact, hi) + _dot(exact, mid) + _dot(exact, lo)


def _dot3_lhs(a, exact):
    hi, mid, lo = _split3(a)
    return _dot(hi, exact) + _dot(mid, exact) + _dot(lo, exact)


def _adaln_kernel(cond_ref, w_ref, b_ref, o_ref):
    s = _silu(cond_ref[...])
    o_ref[0] = _dot(s.astype(BF16), w_ref[0].astype(BF16)) + b_ref[0]


def _adaln(cond, ada_w, ada_b):
    depth, d, n = ada_w.shape
    rows = cond.shape[0]
    tn = 1536
    return pl.pallas_call(
        _adaln_kernel,
        grid=(depth, n // tn),
        in_specs=[pl.BlockSpec((rows, d), lambda l, j: (0, 0)),
                  pl.BlockSpec((1, d, tn), lambda l, j: (l, 0, j)),
                  pl.BlockSpec((1, 1, tn), lambda l, j: (l, 0, j))],
        out_specs=pl.BlockSpec((1, rows, tn), lambda l, j: (l, 0, j)),
        out_shape=jax.ShapeDtypeStruct((depth, rows, n), F32),
        compiler_params=_cparams("parallel", "parallel"),
        name="adaln",
    )(cond, ada_w, ada_b.reshape(depth, 1, n))


def _norm_mod(x, nw, mod, shift_row, scale_row):
    var = jnp.mean(x * x, axis=-1, keepdims=True)
    y = x * lax.rsqrt(var + EPS) * nw
    return y * (1.0 + mod[scale_row:scale_row + 1]) + mod[shift_row:shift_row + 1]


def _nmm_kernel(h_ref, mod_ref, nw_ref, *rest, n_out, shift_row, scale_row, tn):
    w_refs, o_refs = rest[:n_out], rest[n_out:]
    a16 = _norm_mod(h_ref[0], nw_ref[...], mod_ref[0, 0], shift_row, scale_row).astype(BF16)
    for w_ref, o_ref in zip(w_refs, o_refs):
        n = w_ref.shape[1]
        for j in range(0, n, tn):
            w = min(tn, n - j)
            o_ref[0, :, j:j + w] = _dot(a16, w_ref[:, j:j + w]).astype(o_ref.dtype)


def _mod_spec(nct_tiles):
    return pl.BlockSpec((1, 1, 6, D_MODEL), lambda b, i: (b, (i >= nct_tiles).astype(jnp.int32), 0, 0))


def _norm_mod_matmul(h, mod, nw, weights, out_dtypes, shift_row, scale_row, ctx_len):
    B, L, d = h.shape
    tm = ROW_TILE
    in_specs = [pl.BlockSpec((1, tm, d), lambda b, i: (b, i, 0)),
                _mod_spec(ctx_len // tm),
                pl.BlockSpec((1, d), lambda b, i: (0, 0))]
    out_specs, out_shape = [], []
    for w, dt in zip(weights, out_dtypes):
        n = w.shape[1]
        in_specs.append(pl.BlockSpec((d, n), lambda b, i: (0, 0)))
        out_specs.append(pl.BlockSpec((1, tm, n), lambda b, i: (b, i, 0)))
        out_shape.append(jax.ShapeDtypeStruct((B, L, n), dt))
    return pl.pallas_call(
        functools.partial(_nmm_kernel, n_out=len(weights), shift_row=shift_row, scale_row=scale_row, tn=512),
        grid=(B, L // tm),
        in_specs=in_specs, out_specs=out_specs, out_shape=out_shape,
        compiler_params=_cparams("parallel", "parallel"),
        name="norm_mod_matmul",
    )(h, mod, nw.reshape(1, d), *weights)


def _mixer_kernel(rec_ref, prev_ref, next_ref, g_ref, gt_ref, cw_ref, cb_ref, prow_ref, pcol_ref, extra_ref,
                  y_ref, conv_ref, S_ref, C_ref, n_ref, m_ref, *, reverse, nct, nc, nb):
    i = pl.program_id(1)
    c = jnp.where(i < nct, nct - 1 - i, nc - 1 - (i - nct)) if reverse else i
    seq_start = jnp.logical_or(c == 0, c == nct)
    seq_end = jnp.logical_or(c == nct - 1, c == nc - 1)

    @pl.when(i == 0)
    def _():
        S_ref[...] = jnp.zeros_like(S_ref)
        C_ref[...] = jnp.zeros_like(C_ref)
        n_ref[...] = jnp.zeros_like(n_ref)
        m_ref[...] = jnp.zeros_like(m_ref)

    for bb in range(nb):
        xc = rec_ref[bb, :, MLSTM_INNER:].astype(F32)
        before = prev_ref[bb].astype(F32)[HALO - 1:HALO, MLSTM_INNER:]
        after = next_ref[bb].astype(F32)[0:1, MLSTM_INNER:]
        before = jnp.where(seq_start, 0.0, before)
        after = jnp.where(seq_end, 0.0, after)
        rid = lax.broadcasted_iota(jnp.int32, (CHUNK, 1), 0)
        x_prev = jnp.where(rid == 0, before, pltpu.roll(xc, 1, 0))
        x_next = jnp.where(rid == CHUNK - 1, after, pltpu.roll(xc, CHUNK - 1, 0))
        cw = cw_ref[...]
        conv = _silu(cw[0:1] * x_prev + cw[1:2] * xc + cw[2:3] * x_next + cb_ref[...])

        if reverse:
            def emit(col0, width, val, bb=bb):
                y_ref[bb, :, col0:col0 + width] = (extra_ref[bb, :, col0:col0 + width] + val).astype(y_ref.dtype)
        else:
            def emit(col0, width, val, bb=bb, conv=conv):
                if col0 < SSD_INNER:
                    val = val + conv[:, col0:col0 + width] * extra_ref[:, col0:col0 + width]
                y_ref[bb, :, col0:col0 + width] = val

        _mixer_chunk(rec_ref.at[bb], g_ref.at[bb], gt_ref.at[bb], prow_ref, pcol_ref, conv,
                     S_ref.at[bb], C_ref.at[bb], n_ref.at[bb], m_ref.at[bb], emit, reverse=reverse)


def _mixer_chunk(v_ref, g_ref, gt_ref, prow_ref, pcol_ref, conv_ref, S_ref, C_ref, n_ref, m_ref, emit, *, reverse):
    d = 1 if reverse else 0
    li = lax.broadcasted_iota(jnp.int32, (CHUNK, CHUNK), 0)
    si = lax.broadcasted_iota(jnp.int32, (CHUNK, CHUNK), 1)
    mask = (si >= li) if reverse else (si <= li)
    tri = mask.astype(BF16)
    tri_t = ((li >= si) if reverse else (li <= si)).astype(BF16)
    last = 0 if reverse else CHUNK - 1

    G = g_ref[...] + prow_ref[0:1]
    GT = gt_ref[...] + pcol_ref[:, 0:1]
    lane = lax.broadcasted_iota(jnp.int32, (CHUNK, GATE_W), 1)
    row = lax.broadcasted_iota(jnp.int32, (N_GATE, CHUNK), 0)
    dt_c = _softplus(jnp.where(lane < COL_IG, G, -G))
    dt_r = _softplus(jnp.where(row < COL_IG, GT, -GT))
    nega_c = -jnp.exp(prow_ref[1:2])
    nega_r = -jnp.exp(pcol_ref[:, 1:2])
    pre_c = jnp.where(lane < COL_IG, dt_c * nega_c, jnp.where(lane >= COL_FG, -dt_c, 0.0))
    pre_c = jnp.where(lane < N_GATE, pre_c, 0.0)
    pre_r = jnp.where(row < COL_IG, dt_r * nega_r, jnp.where(row >= COL_FG, -dt_r, 0.0))
    cum_c = _dot3_rhs(tri, pre_c)
    cum_r = _dot3_lhs(pre_r, tri_t)

    a_c = cum_c[:, d * SSD_HEADS:(d + 1) * SSD_HEADS]
    a_r = cum_r[d * SSD_HEADS:(d + 1) * SSD_HEADS, :]
    a_last = a_c[last:last + 1, :]
    hh = lax.broadcasted_iota(jnp.int32, (SSD_HEADS, SSD_INNER), 0)
    hj = lax.broadcasted_iota(jnp.int32, (SSD_HEADS, SSD_INNER), 1)
    expand = (hj // SSD_HEAD_DIM == hh).astype(BF16)
    dt_x = _dot3_lhs(dt_c[:, d * SSD_HEADS:(d + 1) * SSD_HEADS], expand)
    dec_x = _dot3_lhs(jnp.exp(a_last - a_c), expand)
    ein_x = _dot3_lhs(jnp.exp(a_c), expand)
    cdec_x = _dot3_lhs(jnp.broadcast_to(jnp.exp(a_last), (8, SSD_HEADS)), expand)[0:1]

    xsdt = conv_ref[:, 0:SSD_INNER] * dt_x
    xsdt16 = xsdt.astype(BF16)
    xdec16 = (xsdt * dec_x).astype(BF16)
    lane128 = lax.broadcasted_iota(jnp.int32, (CHUNK, 128), 1)
    heads_per_group = SSD_HEADS // SSD_GROUPS
    gw = heads_per_group * SSD_HEAD_DIM
    for g in range(SSD_GROUPS):
        Bg = conv_ref[:, SSD_INNER + g * SSD_STATE:SSD_INNER + (g + 1) * SSD_STATE]
        Cg16 = conv_ref[:, SSD_INNER + 256 + g * SSD_STATE:SSD_INNER + 256 + (g + 1) * SSD_STATE].astype(BF16)
        CB = _dot_nt(Cg16, Bg.astype(BF16))
        S_g = S_ref[:, g * gw:(g + 1) * gw]
        y_off = _dot(Cg16, S_g.astype(BF16)) * ein_x[:, g * gw:(g + 1) * gw]
        for pr in range(heads_per_group // 2):
            col0 = g * gw + pr * 128
            xpair = xsdt16[:, col0:col0 + 128]
            acc = None
            for half in range(2):
                h = g * heads_per_group + pr * 2 + half
                seg = a_c[:, h:h + 1] - a_r[h:h + 1, :]
                Lm = jnp.exp(jnp.where(mask, seg, NEG_INF))
                Mh = (CB * Lm).astype(BF16)
                keep = (lane128 < 64) if half == 0 else (lane128 >= 64)
                part = _dot(Mh, jnp.where(keep, xpair, jnp.zeros_like(xpair)))
                acc = part if acc is None else acc + part
            emit(col0, 128, acc + y_off[:, pr * 128:(pr + 1) * 128])
        S_ref[:, g * gw:(g + 1) * gw] = (cdec_x[:, g * gw:(g + 1) * gw] * S_g
                                         + _dot(Bg.T.astype(BF16), xdec16[:, g * gw:(g + 1) * gw]))

    for h in range(MLSTM_HEADS):
        gi = COL_IG + d * MLSTM_HEADS + h
        gf = COL_FG + d * MLSTM_HEADS + h
        qh16 = conv_ref[:, 1536 + h * 128:1536 + (h + 1) * 128].astype(BF16)
        kh = conv_ref[:, 2048 + h * 128:2048 + (h + 1) * 128] * (MLSTM_QK_DIM ** -0.5)
        kh16 = kh.astype(BF16)
        vh16 = v_ref[:, h * MLSTM_V_DIM:(h + 1) * MLSTM_V_DIM]
        ig_c, ig_r = G[:, gi:gi + 1], GT[gi:gi + 1, :]
        b_c, b_r = cum_c[:, gf:gf + 1], cum_r[gf:gf + 1, :]
        b_last = b_c[last:last + 1, :]
        Dlog = jnp.where(mask, b_c - b_r + ig_r, NEG_INF)
        m_loc = jnp.max(b_last - b_r + ig_r, axis=-1, keepdims=True)
        ek = jnp.exp(b_last - b_c + ig_c - m_loc) * kh
        C_loc = _dot(ek.T.astype(BF16), vh16)
        n_loc = jnp.sum(ek, axis=0, keepdims=True)
        C_in = C_ref[h]
        n_in = n_ref[h:h + 1, :]
        m_in = m_ref[h:h + 1, 0:1]
        m_inter = b_c + m_in
        m_t = jnp.maximum(m_inter, jnp.max(Dlog, axis=-1, keepdims=True))
        P = jnp.exp(Dlog - m_t) * _dot_nt(qh16, kh16)
        w_inter = jnp.exp(m_inter - m_t)
        num = _dot(P.astype(BF16), vh16) + w_inter * _dot(qh16, C_in.astype(BF16))
        qn = jnp.sum(qh16.astype(F32) * n_in, axis=-1, keepdims=True)
        den = jnp.sum(P, axis=-1, keepdims=True) + w_inter * qn
        den = jnp.maximum(jnp.abs(den), jnp.exp(-m_t))
        emit(SSD_INNER + h * MLSTM_V_DIM, MLSTM_V_DIM, num / den)
        m_new = jnp.maximum(b_last + m_in, m_loc)
        fa = jnp.exp(b_last + m_in - m_new)
        fb = jnp.exp(m_loc - m_new)
        C_ref[h] = fa * C_in + fb * C_loc
        n_ref[h:h + 1, :] = fa * n_in + fb * n_loc
        m_ref[h:h + 1, :] = jnp.broadcast_to(m_new, (1, 128))


def _mixer_scan(rec, gates, gates_t, conv_w, conv_b, prow, pcol, extra, ctx_len, reverse):
    B, L, _ = rec.shape
    nc, nct = L // CHUNK, ctx_len // CHUNK
    nb = MIXER_BATCH
    hb = CHUNK // HALO
    wide = SSD_INNER + MLSTM_INNER

    def chunk_of(i):
        return jnp.where(i < nct, nct - 1 - i, nc - 1 - (i - nct)) if reverse else i

    const = lambda r, n: pl.BlockSpec((r, n), lambda b, i: (0, 0))
    chunk = lambda n: pl.BlockSpec((nb, CHUNK, n), lambda b, i: (b, chunk_of(i), 0))
    extra_spec = chunk(wide) if reverse else const(1, SSD_INNER)
    return pl.pallas_call(
        functools.partial(_mixer_kernel, reverse=reverse, nct=nct, nc=nc, nb=nb),
        grid=(B // nb, nc),
        in_specs=[chunk(REC_W),
                  pl.BlockSpec((nb, HALO, REC_W), lambda b, i: (b, jnp.maximum(chunk_of(i) * hb - 1, 0), 0)),
                  pl.BlockSpec((nb, HALO, REC_W),
                               lambda b, i: (b, jnp.minimum((chunk_of(i) + 1) * hb, L // HALO - 1), 0)),
                  chunk(GATE_W),
                  pl.BlockSpec((nb, N_GATE, CHUNK), lambda b, i: (b, 0, chunk_of(i))),
                  const(3, CONV_CH), const(1, CONV_CH), const(2, GATE_W), const(N_GATE, 2), extra_spec],
        out_specs=chunk(wide),
        out_shape=jax.ShapeDtypeStruct((B, L, wide), BF16 if reverse else F32),
        scratch_shapes=[pltpu.VMEM((nb, CHUNK, CONV_CH), F32),
                        pltpu.VMEM((nb, SSD_STATE, SSD_INNER), F32),
                        pltpu.VMEM((nb, MLSTM_HEADS, MLSTM_QK_DIM, MLSTM_V_DIM), F32),
                        pltpu.VMEM((nb, 8, MLSTM_QK_DIM), F32),
                        pltpu.VMEM((nb, 8, 128), F32)],
        compiler_params=_cparams("parallel", "arbitrary"),
        name="mixer_bwd" if reverse else "mixer_fwd",
    )(rec, rec, rec, gates, gates_t, conv_w, conv_b, prow, pcol, extra)


def _group_rms(x, groups):
    w = x.shape[-1] // groups
    parts = []
    for g in range(groups):
        seg = x[:, g * w:(g + 1) * w]
        parts.append(seg * lax.rsqrt(jnp.mean(seg * seg, axis=-1, keepdims=True) + EPS))
    return jnp.concatenate(parts, axis=-1)


def _even_out_kernel(y_ref, zo_ref, h_ref, mod_ref, snw_ref, mnw_ref, w_ref, n2_ref, wr_ref, o_ref, a_ref, c_ref):
    y = y_ref[0].astype(F32)
    z = zo_ref[0, :, 0:SSD_INNER].astype(F32)
    o = zo_ref[0, :, SSD_INNER:].astype(F32)
    ys = _group_rms(y[:, 0:SSD_INNER] * _silu(z), SSD_GROUPS) * snw_ref[...]
    hm = _group_rms(y[:, SSD_INNER:], MLSTM_HEADS) * mnw_ref[...] * _sigmoid(o)
    mix = jnp.concatenate([ys, hm], axis=-1).astype(BF16)
    mod = mod_ref[0, 0]
    x = h_ref[0] + mod[2:3] * _dot(mix, w_ref[...])
    o_ref[0] = x
    _route_rows(x, mod, n2_ref, wr_ref, a_ref, c_ref)


def _even_out(y, zo, h, mod, snw, mnw, w_out, norm2_w, w_route, ctx_len):
    B, L, d = h.shape
    tm = ROW_TILE
    wide = SSD_INNER + MLSTM_INNER
    row = lambda n: pl.BlockSpec((1, tm, n), lambda b, i: (b, i, 0))
    const = lambda r, n: pl.BlockSpec((r, n), lambda b, i: (0, 0))
    r_in, r_out, r_shape = _route_specs(B, L, d, tm)
    return pl.pallas_call(
        _even_out_kernel,
        grid=(B, L // tm),
        in_specs=[row(wide), row(wide), row(d), _mod_spec(ctx_len // tm),
                  const(1, SSD_INNER), const(1, MLSTM_INNER), const(wide, d)] + r_in,
        out_specs=[row(d)] + r_out,
        out_shape=[jax.ShapeDtypeStruct((B, L, d), F32)] + r_shape,
        compiler_params=_cparams("parallel", "parallel"),
        name="even_out",
    )(y, zo, h, mod, snw.reshape(1, -1), mnw.reshape(1, -1), w_out, norm2_w.reshape(1, d), w_route)


def _qkv_kernel(h_ref, mod_ref, nw_ref, w_ref, cs_ref, sn_ref, qnw_ref, knw_ref,
                qs_ref, ks_ref, vs1_ref, qd_ref, kd_ref, vd1_ref, *, nct):
    is_lat = pl.program_id(1) >= nct
    a16 = _norm_mod(h_ref[0], nw_ref[...], mod_ref[0, 0], 0, 1).astype(BF16)
    p = _dot(a16, w_ref[...])
    cs, sn = cs_ref[...], sn_ref[...]

    def rope(x):
        w = x.shape[1]
        reps = w // 128
        c = jnp.concatenate([cs] * reps, axis=1) if reps > 1 else cs
        s = jnp.concatenate([sn] * reps, axis=1) if reps > 1 else sn
        lane = lax.broadcasted_iota(jnp.int32, x.shape, 1)
        partner = jnp.where(lane % 2 == 0, pltpu.roll(x, w - 1, 1), pltpu.roll(x, 1, 1))
        return jnp.where(is_lat, x * c + partner * s, x)

    def head_rms(x, nw):
        w = x.shape[1]
        a = lax.broadcasted_iota(jnp.int32, (w, w), 0) // ATT_DH
        b = lax.broadcasted_iota(jnp.int32, (w, w), 1) // ATT_DH
        same = (a == b).astype(BF16)
        ms = _dot3_lhs(x * x, same) * (1.0 / ATT_DH)
        return x * lax.rsqrt(ms + EPS) * nw

    qw, kw = ATT_HEADS * ATT_DH, ATT_KV * ATT_DH
    o = 0
    q_s = rope(p[:, o:o + qw]) * (ATT_DH ** -0.5 * LOG2E); o += qw
    k_s = rope(p[:, o:o + kw]); o += kw
    v_s = p[:, o:o + kw]; o += kw
    q_d = rope(head_rms(p[:, o:o + qw], qnw_ref[...])) * (ATT_DH ** -0.5 * LOG2E); o += qw
    k_d = rope(head_rms(p[:, o:o + kw], knw_ref[...])); o += kw
    v_d = p[:, o:o + kw]

    lane = lax.broadcasted_iota(jnp.int32, (p.shape[0], 128), 1)
    for h in range(ATT_HEADS):
        qs_ref[0, h] = q_s[:, h * ATT_DH:(h + 1) * ATT_DH].astype(BF16)
        qd_ref[0, h] = q_d[:, h * ATT_DH:(h + 1) * ATT_DH].astype(BF16)
    for kv in range(ATT_KV):
        ks_ref[0, kv] = k_s[:, kv * ATT_DH:(kv + 1) * ATT_DH].astype(BF16)
        kd_ref[0, kv] = k_d[:, kv * ATT_DH:(kv + 1) * ATT_DH].astype(BF16)
    for v, v1_ref in ((v_s, vs1_ref), (v_d, vd1_ref)):
        for kv in range(ATT_KV):
            own = jnp.where((lane // ATT_DH) == kv, v, 0.0)
            low = own if kv == 0 else pltpu.roll(own, ATT_DH, 1)
            v1_ref[0, kv] = jnp.where(lane == ATT_DH, 1.0, low).astype(BF16)


def _qkv(h, mod, nw, w_in, cosf, sinf, qnw, knw, ctx_len):
    B, L, d = h.shape
    tm = ROW_TILE
    nct = ctx_len // tm
    const = lambda r, n: pl.BlockSpec((r, n), lambda b, i: (0, 0))
    tab = pl.BlockSpec((tm, 128), lambda b, i: (jnp.maximum(i - nct, 0), 0))
    head = lambda n, w: pl.BlockSpec((1, n, tm, w), lambda b, i: (b, 0, i, 0))
    shp = lambda n, w: jax.ShapeDtypeStruct((B, n, L, w), BF16)
    return pl.pallas_call(
        functools.partial(_qkv_kernel, nct=nct),
        grid=(B, L // tm),
        in_specs=[pl.BlockSpec((1, tm, d), lambda b, i: (b, i, 0)), _mod_spec(nct), const(1, d), const(d, ODD_PROJ),
                  tab, tab, const(1, ATT_HEADS * ATT_DH), const(1, ATT_KV * ATT_DH)],
        out_specs=[head(ATT_HEADS, ATT_DH), head(ATT_KV, ATT_DH), head(ATT_KV, 128)] * 2,
        out_shape=[shp(ATT_HEADS, ATT_DH), shp(ATT_KV, ATT_DH), shp(ATT_KV, 128)] * 2,
        compiler_params=_cparams("parallel", "parallel"),
        name="qkv",
    )(h, mod, nw.reshape(1, d), w_in, cosf, sinf, qnw, knw)


def _group_softmax_out(scores, values, rows, sink=None):
    m = functools.reduce(jnp.maximum, [jnp.max(s, axis=-1, keepdims=True) for s in scores])
    if sink is not None:
        m = jnp.maximum(m, sink)
    probs = [jnp.exp2(s - m).astype(BF16) for s in scores]
    lane = lax.broadcasted_iota(jnp.int32, (rows, 128), 1)
    outs = []
    for pr in range(ATT_G // 2):
        heads = []
        for g in (2 * pr, 2 * pr + 1):
            rs = slice(g * rows, (g + 1) * rows)
            O = functools.reduce(lambda a, b: a + b, [_dot(p[rs], v) for p, v in zip(probs, values)])
            l = O[:, ATT_DH:ATT_DH + 1]
            if sink is not None:
                l = l + jnp.exp2(sink[rs] - m[rs])
            heads.append(O * (1.0 / l))
        outs.append(jnp.where(lane < ATT_DH, heads[0], pltpu.roll(heads[1], ATT_DH, 1)))
    return jnp.concatenate(outs, axis=-1)


def _dense_attn_kernel(q_ref, k_ref, v1_ref, o_ref):
    for kv in range(ATT_KV):
        Q = q_ref[0, kv * ATT_G:(kv + 1) * ATT_G].reshape(ATT_G * QBLK, ATT_DH)
        o_ref[0, :, kv * ATT_G * ATT_DH:(kv + 1) * ATT_G * ATT_DH] = _group_softmax_out(
            [_dot_nt(Q, k_ref[0, kv])], [v1_ref[0, kv]], QBLK).astype(o_ref.dtype)


def _dense_attn(qd, kd, vd1, ctx_len):
    B, _, L, _ = qd.shape
    S = L - ctx_len
    nct = ctx_len // QBLK
    full = lambda w: pl.BlockSpec((1, ATT_KV, L, w), lambda b, j: (b, 0, 0, 0))
    return pl.pallas_call(
        _dense_attn_kernel,
        grid=(B, S // QBLK),
        in_specs=[pl.BlockSpec((1, ATT_HEADS, QBLK, ATT_DH), lambda b, j: (b, 0, j + nct, 0)),
                  full(ATT_DH), full(128)],
        out_specs=pl.BlockSpec((1, QBLK, ATT_HEADS * ATT_DH), lambda b, j: (b, j, 0)),
        out_shape=jax.ShapeDtypeStruct((B, S, ATT_HEADS * ATT_DH), BF16),
        compiler_params=_cparams("parallel", "parallel"),
        name="dense_attn",
    )(qd, kd, vd1)


def _sink_rows(sink_ref, kv, rows):
    return jnp.concatenate(
        [jnp.broadcast_to(sink_ref[kv * ATT_G + g:kv * ATT_G + g + 1, 0:1], (rows, 1)) for g in range(ATT_G)], axis=0)


def _window_attn_kernel(q_ref, k_ref, v1_ref, sink_ref, o_ref, *, ctx_len, seq):
    j = pl.program_id(1)
    span = 3 * QBLK
    start = jnp.clip((j - 1) * QBLK, 0, seq - span)
    kstart = pl.multiple_of(ctx_len + start, QBLK)
    rows = ATT_G * QBLK
    qpos = j * QBLK + lax.broadcasted_iota(jnp.int32, (rows, span), 0) % QBLK
    kpos = start + lax.broadcasted_iota(jnp.int32, (rows, span), 1)
    valid = jnp.abs(kpos - qpos) <= WINDOW
    for kv in range(ATT_KV):
        Q = q_ref[0, kv * ATT_G:(kv + 1) * ATT_G].reshape(rows, ATT_DH)
        Sc = _dot_nt(Q, k_ref[0, kv, 0:ctx_len])
        Sl = jnp.where(valid, _dot_nt(Q, k_ref[0, kv, pl.ds(kstart, span)]), NEG_INF)
        out = _group_softmax_out([Sc, Sl], [v1_ref[0, kv, 0:ctx_len], v1_ref[0, kv, pl.ds(kstart, span)]], QBLK,
                                 sink=_sink_rows(sink_ref, kv, QBLK) * LOG2E)
        o_ref[0, :, kv * ATT_G * ATT_DH:(kv + 1) * ATT_G * ATT_DH] = out.astype(o_ref.dtype)


def _window_attn(qs, ks, vs1, sink_b, ctx_len):
    B, _, L, _ = qs.shape
    S = L - ctx_len
    nct = ctx_len // QBLK
    full = lambda w: pl.BlockSpec((1, ATT_KV, L, w), lambda b, j: (b, 0, 0, 0))
    return pl.pallas_call(
        functools.partial(_window_attn_kernel, ctx_len=ctx_len, seq=S),
        grid=(B, S // QBLK),
        in_specs=[pl.BlockSpec((1, ATT_HEADS, QBLK, ATT_DH), lambda b, j: (b, 0, j + nct, 0)),
                  full(ATT_DH), full(128),
                  pl.BlockSpec((ATT_HEADS, 128), lambda b, j: (0, 0))],
        out_specs=pl.BlockSpec((1, QBLK, ATT_HEADS * ATT_DH), lambda b, j: (b, j, 0)),
        out_shape=jax.ShapeDtypeStruct((B, S, ATT_HEADS * ATT_DH), BF16),
        compiler_params=_cparams("parallel", "parallel"),
        name="window_attn",
    )(qs, ks, vs1, sink_b)


def _ctx_attn_kernel(qs_ref, ks_ref, vs1_ref, qd_ref, kd_ref, vd1_ref, sink_ref, os_ref, od_ref, *, ctx_len):
    T = ctx_len
    for kv in range(ATT_KV):
        cols = slice(kv * ATT_G * ATT_DH, (kv + 1) * ATT_G * ATT_DH)
        Q = qs_ref[0, kv * ATT_G:(kv + 1) * ATT_G].reshape(ATT_G * T, ATT_DH)
        os_ref[0, :, cols] = _group_softmax_out([_dot_nt(Q, ks_ref[0, kv])], [vs1_ref[0, kv]], T,
                                                sink=_sink_rows(sink_ref, kv, T) * LOG2E).astype(os_ref.dtype)
        Q = qd_ref[0, kv * ATT_G:(kv + 1) * ATT_G].reshape(ATT_G * T, ATT_DH)
        od_ref[0, :, cols] = _group_softmax_out([_dot_nt(Q, kd_ref[0, kv])], [vd1_ref[0, kv]], T).astype(od_ref.dtype)


def _ctx_attn(qs, ks, vs1, qd, kd, vd1, sink_b, ctx_len):
    B = qs.shape[0]
    T = ctx_len
    blk = lambda n, w: pl.BlockSpec((1, n, T, w), lambda b: (b, 0, 0, 0))
    out = pl.BlockSpec((1, T, ATT_HEADS * ATT_DH), lambda b: (b, 0, 0))
    return pl.pallas_call(
        functools.partial(_ctx_attn_kernel, ctx_len=T),
        grid=(B,),
        in_specs=[blk(ATT_HEADS, ATT_DH), blk(ATT_KV, ATT_DH), blk(ATT_KV, 128),
                  blk(ATT_HEADS, ATT_DH), blk(ATT_KV, ATT_DH), blk(ATT_KV, 128),
                  pl.BlockSpec((ATT_HEADS, 128), lambda b: (0, 0))],
        out_specs=[out, out],
        out_shape=[jax.ShapeDtypeStruct((B, T, ATT_HEADS * ATT_DH), BF16)] * 2,
        compiler_params=_cparams("parallel"),
        name="ctx_attn",
    )(qs, ks, vs1, qd, kd, vd1, sink_b)


def _odd_out_kernel(ysc_ref, ydc_ref, ysl_ref, ydl_ref, h_ref, mod_ref, w_ref, n2_ref, wr_ref,
                    o_ref, a_ref, c_ref, *, nct):
    is_ctx = pl.program_id(1) < nct
    ys = jnp.where(is_ctx, ysc_ref[0], ysl_ref[0])
    yd = jnp.where(is_ctx, ydc_ref[0], ydl_ref[0])
    mix = jnp.concatenate([ys, yd], axis=-1)
    mod = mod_ref[0, 0]
    x = h_ref[0] + mod[2:3] * _dot(mix, w_ref[...])
    o_ref[0] = x
    _route_rows(x, mod, n2_ref, wr_ref, a_ref, c_ref)


def _odd_out(ys_ctx, yd_ctx, ys_lat, yd_lat, h, mod, w_out, norm2_w, w_route, ctx_len):
    B, L, d = h.shape
    tm = ROW_TILE
    nct = ctx_len // tm
    half = ATT_HEADS * ATT_DH
    row = lambda n: pl.BlockSpec((1, tm, n), lambda b, i: (b, i, 0))
    ctx = pl.BlockSpec((1, tm, half), lambda b, i: (b, jnp.minimum(i, nct - 1), 0))
    lat = pl.BlockSpec((1, tm, half), lambda b, i: (b, jnp.maximum(i - nct, 0), 0))
    r_in, r_out, r_shape = _route_specs(B, L, d, tm)
    return pl.pallas_call(
        functools.partial(_odd_out_kernel, nct=nct),
        grid=(B, L // tm),
        in_specs=[ctx, ctx, lat, lat, row(d), _mod_spec(nct),
                  pl.BlockSpec((2 * half, d), lambda b, i: (0, 0))] + r_in,
        out_specs=[row(d)] + r_out,
        out_shape=[jax.ShapeDtypeStruct((B, L, d), F32)] + r_shape,
        compiler_params=_cparams("parallel", "parallel"),
        name="odd_out",
    )(ys_ctx, yd_ctx, ys_lat, yd_lat, h, mod, w_out, norm2_w.reshape(1, d), w_route)


def _first_lane_of(cond, lane):
    return jnp.min(jnp.where(cond, lane, ROUTE_W), axis=-1, keepdims=True)


def _route_rows(x, mod, nw_ref, wr_ref, a_ref, c_ref):
    a = _norm_mod(x, nw_ref[...], mod, 3, 4)
    a_hi = a.astype(BF16)
    a_ref[0, :, :D_MODEL] = a_hi
    a_lo = (a - a_hi.astype(F32)).astype(BF16)
    w = wr_ref[...]
    w_hi = w.astype(BF16)
    w_lo = (w - w_hi.astype(F32)).astype(BF16)
    logits = _dot(a_hi, w_hi) + (_dot(a_lo, w_hi) + _dot(a_hi, w_lo))
    lane = lax.broadcasted_iota(jnp.int32, logits.shape, 1)
    gl = jnp.where(lane < N_GROUPS, logits, NEG_INF)
    gmax = jnp.max(gl, axis=-1, keepdims=True)
    p_top = 1.0 / jnp.sum(jnp.exp(gl - gmax), axis=-1, keepdims=True)
    g_idx = _first_lane_of(gl == gmax, lane)
    in_group = jnp.logical_and(lane >= EXPERT_LANE0, (lane - EXPERT_LANE0) // EXPERTS_PER_GROUP == g_idx)
    in_group = jnp.logical_and(in_group, lane < EXPERT_LANE0 + N_EXPERTS)
    el = jnp.where(in_group, logits, NEG_INF)
    m1 = jnp.max(el, axis=-1, keepdims=True)
    i1 = _first_lane_of(el == m1, lane)
    el2 = jnp.where(lane == i1, NEG_INF, el)
    m2 = jnp.max(el2, axis=-1, keepdims=True)
    i2 = _first_lane_of(el2 == m2, lane)
    e2 = jnp.exp(m2 - m1)
    w1 = 1.0 / (1.0 + e2) * p_top
    w2 = e2 / (1.0 + e2) * p_top
    j1 = i1 - EXPERT_LANE0 - EXPERTS_PER_GROUP * g_idx
    j2 = i2 - EXPERT_LANE0 - EXPERTS_PER_GROUP * g_idx
    both = lambda j, w: jnp.where(jnp.logical_or(lane == j, lane == j + EXPERTS_PER_GROUP), w, 0.0)
    gate = both(j1, w1) + both(j2, w2)
    gate_hi = gate.astype(BF16).astype(F32)
    tail = jnp.where(lane < EXPERTS_PER_GROUP, gate_hi, gate - gate_hi)
    tail = jnp.where(lane == GIDX_LANE, g_idx.astype(F32), tail)
    a_ref[0, :, D_MODEL:] = tail.astype(BF16)
    counts =jnp.sum((lane == g_idx).astype(jnp.int32), axis=0, keepdims=True)
    c_ref[0, 0] = jnp.broadcast_to(counts, (8, ROUTE_W))


def _route_specs(B, L, d, tm):
    in_specs = [pl.BlockSpec((1, d), lambda b, i: (0, 0)), pl.BlockSpec((d, ROUTE_W), lambda b, i: (0, 0))]
    out_specs = [pl.BlockSpec((1, tm, d + ROUTE_W), lambda b, i: (b, i, 0)),
                 pl.BlockSpec((1, 1, 8, ROUTE_W), lambda b, i: (b, i, 0, 0))]
    out_shape = [jax.ShapeDtypeStruct((B, L, d + ROUTE_W), BF16),
                 jax.ShapeDtypeStruct((B, L // tm, 8, ROUTE_W), jnp.int32)]
    return in_specs, out_specs, out_shape


def _route_plan(counts, n_row_tiles):
    pc = (counts + RUN_ALIGN - 1) // RUN_ALIGN * RUN_ALIGN
    region = (jnp.sum(pc, axis=0) + EXPERT_TILE - 1) // EXPERT_TILE * EXPERT_TILE
    region_end = jnp.cumsum(region)
    off = (region_end - region)[None, :] + jnp.cumsum(pc, axis=0) - pc
    n_used = (region_end[-1] // EXPERT_TILE).reshape(1)
    tile_row0 = jnp.arange(n_row_tiles, dtype=jnp.int32) * EXPERT_TILE
    tile_group = jnp.minimum(jnp.sum(tile_row0[:, None] >= region_end[None, :], axis=1), N_GROUPS - 1)
    return (pc.reshape(-1).astype(jnp.int32), off.reshape(-1).astype(jnp.int32),
            tile_group.astype(jnp.int32), n_used.astype(jnp.int32))


def _run_bases(pc_ref, t):
    bases, base = [], 0
    for g in range(N_GROUPS):
        bases.append(base)
        base = base + pc_ref[t * N_GROUPS + g]
    return bases


def _tile_slots(G, bases):
    tm = G.shape[0]
    lane = lax.broadcasted_iota(jnp.int32, (tm, ROUTE_W), 1)
    g_idx = G[:, GIDX_LANE:GIDX_LANE + 1].astype(jnp.int32)
    onehot = (lane == g_idx).astype(BF16)
    ti = lax.broadcasted_iota(jnp.int32, (tm, tm), 0)
    tj = lax.broadcasted_iota(jnp.int32, (tm, tm), 1)
    earlier = _dot((tj < ti).astype(BF16), onehot)
    slot = jnp.sum(jnp.where(lane == g_idx, earlier, 0.0), axis=-1, keepdims=True).astype(jnp.int32)
    for g in range(N_GROUPS):
        slot = slot + jnp.where(g_idx == g, bases[g], 0)
    return slot


def _run_copies(pc_ref, off_ref, t, buf, pairs, sems, to_sorted, live):
    bases = _run_bases(pc_ref, t)
    out = []
    for g in range(N_GROUPS):
        n = pc_ref[t * N_GROUPS + g]
        dst0 = off_ref[t * N_GROUPS + g]
        for k, size in enumerate(RUN_PIECES):
            po = n & ~(2 * size - 1)
            present = jnp.logical_and(live, (n & size) != 0)
            v0 = pl.multiple_of(bases[g] + po, RUN_ALIGN)
            h0 = pl.multiple_of(dst0 + po, RUN_ALIGN)
            for a, (v_ref, h_ref) in enumerate(pairs):
                v, hb = v_ref.at[buf, pl.ds(v0, size)], h_ref.at[pl.ds(h0, size)]
                src, dst = (v, hb) if to_sorted else (hb, v)
                out.append((present, pltpu.make_async_copy(src, dst, sems.at[buf, a, g, k])))
    return out


def _start(copies):
    for present, cp in copies:
        pl.when(present)(cp.start)


def _wait(copies):
    for present, cp in copies:
        pl.when(present)(cp.wait)


def _dispatch_kernel(pc_ref, off_ref, a_ref, xs_in, xs_ref, sx_ref, sems):
    del xs_in
    n_tiles = pl.num_programs(0) * pl.num_programs(1)
    t = pl.program_id(0) * pl.num_programs(1) + pl.program_id(1)
    buf = t % 2
    G = a_ref[0, :, D_MODEL:].astype(F32)
    slot = _tile_slots(G, _run_bases(pc_ref, t))
    tm = G.shape[0]
    lane = lax.broadcasted_iota(jnp.int32, (tm, SORT_ROWS), 1)
    perm = (lane == slot).astype(F32).T.astype(BF16)
    sx_ref[buf] = _dot(perm, a_ref[0]).astype(BF16)
    pairs = [(sx_ref, xs_ref)]
    _start(_run_copies(pc_ref, off_ref, t, buf, pairs, sems, True, True))
    _wait(_run_copies(pc_ref, off_ref, jnp.maximum(t - 1, 0), 1 - buf, pairs, sems, True, t >= 1))
    _wait(_run_copies(pc_ref, off_ref, t, buf, pairs, sems, True, t == n_tiles - 1))


def _dispatch(a2, pc, off, n_rows):
    B, L, w = a2.shape
    tm = ROW_TILE
    anyspec = pl.BlockSpec(memory_space=pl.ANY)
    return pl.pallas_call(
        _dispatch_kernel,
        grid_spec=pltpu.PrefetchScalarGridSpec(
            num_scalar_prefetch=2, grid=(B, L // tm),
            in_specs=[pl.BlockSpec((1, tm, w), lambda b, i, *_: (b, i, 0)), anyspec],
            out_specs=anyspec,
            scratch_shapes=[pltpu.VMEM((2, SORT_ROWS, w), BF16),
                            pltpu.SemaphoreType.DMA((2, 1, N_GROUPS, len(RUN_PIECES)))]),
        out_shape=jax.ShapeDtypeStruct((n_rows, w), BF16),
        input_output_aliases={3: 0},
        compiler_params=_cparams("arbitrary", "arbitrary"),
        name="moe_dispatch",
    )(pc, off, a2, jnp.zeros((n_rows, w), BF16))


def _group_experts_kernel(tg_ref, nu_ref, x_ref, wg_ref, wu_ref, wd_ref, y_ref):
    del tg_ref
    used = pl.program_id(0) < nu_ref[0]

    @pl.when(jnp.logical_not(used))
    def _():
        y_ref[...] = jnp.zeros_like(y_ref)

    @pl.when(used)
    def _():
        x = x_ref[:, :D_MODEL]
        hid = _silu(_dot(x, wg_ref[0])) * _dot(x, wu_ref[0])
        tail = x_ref[:, D_MODEL:].astype(F32)
        gates = tail[:, 0:EXPERTS_PER_GROUP] + tail[:, EXPERTS_PER_GROUP:2 * EXPERTS_PER_GROUP]
        hid16 = jnp.concatenate(
            [(hid[:, e * EXPERT_FF:(e + 1) * EXPERT_FF] * gates[:, e:e + 1]).astype(BF16)
             for e in range(EXPERTS_PER_GROUP)], axis=-1)
        y_ref[...] = _dot(hid16, wd_ref[0]).astype(y_ref.dtype)


def _group_experts(xs, tile_group, n_used, wg, wu, wd):
    n_rows, w = xs.shape
    d = D_MODEL
    tm = EXPERT_TILE
    ff = EXPERTS_PER_GROUP * EXPERT_FF
    wspec = lambda r, c: pl.BlockSpec((1, r, c), lambda i, tg, nu: (tg[jnp.minimum(i, nu[0] - 1)], 0, 0))
    return pl.pallas_call(
        _group_experts_kernel,
        grid_spec=pltpu.PrefetchScalarGridSpec(
            num_scalar_prefetch=2, grid=(n_rows // tm,),
            in_specs=[pl.BlockSpec((tm, w), lambda i, tg, nu: (jnp.minimum(i, nu[0] - 1), 0)),
                      wspec(d, ff), wspec(d, ff), wspec(ff, d)],
            out_specs=pl.BlockSpec((tm, d), lambda i, tg, nu: (i, 0))),
        out_shape=jax.ShapeDtypeStruct((n_rows, d), BF16),
        compiler_params=_cparams("arbitrary"),
        name="moe_experts",
    )(tile_group, n_used, xs, wg, wu, wd)


def _window_copies(pc_ref, off_ref, t, buf, sy_ref, ys_ref, sems, live):
    out = []
    for g in range(N_GROUPS):
        present = jnp.logical_and(live, pc_ref[t * N_GROUPS + g] > 0)
        h0 = pl.multiple_of(off_ref[t * N_GROUPS + g], RUN_ALIGN)
        out.append((present, pltpu.make_async_copy(ys_ref.at[pl.ds(h0, ROW_TILE)],
                                                   sy_ref.at[buf, pl.ds(g * ROW_TILE, ROW_TILE)], sems.at[buf, g])))
    return out


def _combine_kernel(pc_ref, off_ref, g_ref, h_ref, mod_ref, fw_ref, ys_ref, o_ref, sy_ref, sems, *, final):
    n_tiles = pl.num_programs(0) * pl.num_programs(1)
    t = pl.program_id(0) * pl.num_programs(1) + pl.program_id(1)
    buf = t % 2

    @pl.when(t == 0)
    def _():
        sy_ref[...] = jnp.zeros_like(sy_ref)

    _start(_window_copies(pc_ref, off_ref, t, buf, sy_ref, ys_ref, sems, t == 0))
    _start(_window_copies(pc_ref, off_ref, jnp.minimum(t + 1, n_tiles - 1), 1 - buf, sy_ref, ys_ref, sems,
                          t + 1 < n_tiles))
    G = g_ref[0].astype(F32)
    tm = G.shape[0]
    slot = _tile_slots(G, [g * tm for g in range(N_GROUPS)])
    lane = lax.broadcasted_iota(jnp.int32, (tm, N_GROUPS * tm), 1)
    pick = (lane == slot).astype(BF16)
    _wait(_window_copies(pc_ref, off_ref, t, buf, sy_ref, ys_ref, sems, True))
    x = h_ref[0] + mod_ref[0, 0][5:6] * _dot(pick, sy_ref[buf])
    if final:
        x = x * lax.rsqrt(jnp.mean(x * x, axis=-1, keepdims=True) + EPS) * fw_ref[...]
    o_ref[0] = x


def _combine(ys, a2, h, mod, pc, off, final_w, ctx_len, final):
    B, L, d = h.shape
    tm = ROW_TILE
    nct = ctx_len // tm
    row = lambda n: pl.BlockSpec((1, tm, n), lambda b, i, *_: (b, i, 0))
    if final:
        out_spec = pl.BlockSpec((1, tm, d), lambda b, i, *_: (b, jnp.maximum(i - nct, 0), 0))
        out_shape = jax.ShapeDtypeStruct((B, L - ctx_len, d), F32)
    else:
        out_spec, out_shape = row(d), jax.ShapeDtypeStruct((B, L, d), F32)
    return pl.pallas_call(
        functools.partial(_combine_kernel, final=final),
        grid_spec=pltpu.PrefetchScalarGridSpec(
            num_scalar_prefetch=2, grid=(B, L // tm),
            in_specs=[pl.BlockSpec((1, tm, ROUTE_W), lambda b, i, *_: (b, i, d // ROUTE_W)),
                      row(d),
                      pl.BlockSpec((1, 1, 6, d), lambda b, i, *_: (b, (i >= nct).astype(jnp.int32), 0, 0)),
                      pl.BlockSpec((1, d), lambda b, i, *_: (0, 0)),
                      pl.BlockSpec(memory_space=pl.ANY)],
            out_specs=out_spec,
            scratch_shapes=[pltpu.VMEM((2, N_GROUPS * tm, d), BF16),
                            pltpu.SemaphoreType.DMA((2, N_GROUPS))]),
        out_shape=out_shape,
        compiler_params=_cparams("arbitrary", "arbitrary"),
        name="moe_combine",
    )(pc, off, a2, h, mod, final_w.reshape(1, d), ys)


def _rope_tables(seq):
    rows = seq // GRID_W
    row = jnp.repeat(jnp.arange(rows), GRID_W).astype(F32)
    col = (jnp.arange(rows * GRID_W) % GRID_W).astype(F32)
    axis_dim = ATT_DH // 2
    inv = ROPE_THETA ** (-jnp.arange(0, axis_dim, 2, dtype=F32) / axis_dim)
    ang = jnp.concatenate([row[:, None] * inv, col[:, None] * inv], axis=-1)
    cosf = jnp.repeat(jnp.cos(ang), 2, axis=-1)
    sinf = jnp.repeat(jnp.sin(ang), 2, axis=-1) * jnp.tile(jnp.array([-1.0, 1.0], F32), ATT_DH // 2)
    return jnp.tile(cosf, (1, 2)), jnp.tile(sinf, (1, 2))


def kernel(x, c, ctx, c_ctx, ada_w, ada_b, norm1_w, norm2_w, ev_w_in, ev_conv_w, ev_conv_b, ev_dt_bias, ev_a_log, ev_d_skip, ev_ssd_norm_w, ev_ig_bias, ev_fg_bias, ev_mlstm_norm_w, ev_w_out, od_w_in, od_sink, od_q_norm_w, od_k_norm_w, od_w_out, moe_w_group, moe_w_expert, moe_w_gate, moe_w_up, moe_w_down, final_norm_w):
    B, S, d = x.shape
    T = ctx.shape[1]
    depth = ada_w.shape[0]
    assert d == D_MODEL and T % ROW_TILE == 0 and S % ROW_TILE == 0 and S % GRID_W == 0 and S >= 3 * QBLK

    h = jnp.concatenate([ctx, x], axis=1)

    rows = -(-(B + 1) // 8) * 8
    cond = jnp.zeros((rows, d), F32).at[:B].set(c).at[B].set(c_ctx)
    ada = _adaln(cond, ada_w, ada_b).reshape(depth, rows, 6, d)
    mods = jnp.stack([jnp.broadcast_to(ada[:, B:B + 1], (depth, B, 6, d)), ada[:, :B]], axis=2)

    cosf, sinf = _rope_tables(S)

    for layer in range(depth):
        li = layer // 2
        mod = mods[layer]
        w_route = jnp.pad(jnp.concatenate([moe_w_group[layer], moe_w_expert[layer]], axis=1),
                          ((0, 0), (0, ROUTE_W - N_GROUPS - N_EXPERTS)))
        if layer % 2 == 0:
            w_in = ev_w_in[li]
            conv, z, v, o, gate_cols = (w_in[:, :CONV_CH], w_in[:, CONV_CH:CONV_CH + SSD_INNER],
                                        w_in[:, CONV_CH + SSD_INNER:CONV_CH + 2 * SSD_INNER],
                                        w_in[:, CONV_CH + 2 * SSD_INNER:CONV_CH + 3 * SSD_INNER],
                                        w_in[:, CONV_CH + 3 * SSD_INNER:])
            w_rec = jnp.concatenate([v, conv], axis=1).astype(BF16)
            w_zo = jnp.concatenate([z, o], axis=1).astype(BF16)
            w_g = jnp.pad(gate_cols, ((0, 0), (0, GATE_W - N_GATE))).astype(BF16)
            rec, zo, gates = _norm_mod_matmul(h, mod, norm1_w[layer], [w_rec, w_zo, w_g], [BF16, BF16, F32], 0, 1, T)
            gates_t = jnp.swapaxes(gates[:, :, :N_GATE], 1, 2)
            bias = jnp.concatenate([ev_dt_bias[li].reshape(-1), ev_ig_bias[li].reshape(-1), ev_fg_bias[li].reshape(-1)])
            alog = jnp.pad(ev_a_log[li].reshape(-1), (0, N_GATE - 2 * SSD_HEADS))
            prow = jnp.pad(jnp.stack([bias, alog]), ((0, 0), (0, GATE_W - N_GATE)))
            pcol = jnp.stack([bias, alog], axis=1)
            dskip = jnp.repeat(ev_d_skip[li], SSD_HEAD_DIM).reshape(1, SSD_INNER)
            cb = ev_conv_b[li].reshape(1, CONV_CH)
            yf = _mixer_scan(rec, gates, gates_t, ev_conv_w[li], cb, prow, pcol, dskip, T, False)
            y = _mixer_scan(rec, gates, gates_t, ev_conv_w[li], cb, prow, pcol, yf, T, True)
            h, a2, counts = _even_out(y, zo, h, mod, ev_ssd_norm_w[li], ev_mlstm_norm_w[li],
                                      ev_w_out[li].astype(BF16), norm2_w[layer], w_route, T)
        else:
            qnw = jnp.tile(od_q_norm_w[li], ATT_HEADS).reshape(1, -1)
            knw = jnp.tile(od_k_norm_w[li], ATT_KV).reshape(1, -1)
            qs, ks, vs1, qd, kd, vd1 = _qkv(h, mod, norm1_w[layer], od_w_in[li].astype(BF16), cosf, sinf, qnw, knw, T)
            sink_b = jnp.broadcast_to(od_sink[li].reshape(ATT_HEADS, 1), (ATT_HEADS, 128))
            ys_lat = _window_attn(qs, ks, vs1, sink_b, T)
            yd_lat = _dense_attn(qd, kd, vd1, T)
            ys_ctx, yd_ctx = _ctx_attn(qs, ks, vs1, qd, kd, vd1, sink_b, T)
            h, a2, counts = _odd_out(ys_ctx, yd_ctx, ys_lat, yd_lat, h, mod, od_w_out[li].astype(BF16),
                                     norm2_w[layer], w_route, T)

        n_tiles = B * ((T + S) // ROW_TILE)
        n_rows = -(-(n_tiles * (ROW_TILE + N_GROUPS * RUN_ALIGN) + N_GROUPS * EXPERT_TILE + ROW_TILE)
                   // EXPERT_TILE) * EXPERT_TILE
        pc, off, tile_group, n_used = _route_plan(counts[:, :, 0, :N_GROUPS].reshape(n_tiles, N_GROUPS),
                                                  n_rows // EXPERT_TILE)
        xs = _dispatch(a2, pc, off, n_rows)
        ff = EXPERTS_PER_GROUP * EXPERT_FF
        wg = moe_w_gate[layer].transpose(0, 2, 1, 3).reshape(N_GROUPS, d, ff).astype(BF16)
        wu = moe_w_up[layer].transpose(0, 2, 1, 3).reshape(N_GROUPS, d, ff).astype(BF16)
        wd = moe_w_down[layer].reshape(N_GROUPS, ff, d).astype(BF16)
        ys = _group_experts(xs, tile_group, n_used, wg, wu, wd)
        h = _combine(ys, a2, h, mod, pc, off, final_norm_w, T, final=layer == depth - 1)

    return h
```

```python
import functools
import math

import jax
import jax.numpy as jnp
from jax import lax
from jax.experimental import pallas as pl
from jax.experimental.pallas import tpu as pltpu

F32 = jnp.float32
BF16 = jnp.bfloat16
EPS = 1e-6
NEG_INF = float("-inf")
LOG2E = 1.4426950408889634

D_MODEL = 1024
GRID_W = 64
ROPE_THETA = 10000.0

SSD_HEADS = 16
SSD_HEAD_DIM = 64
SSD_INNER = 1024
SSD_GROUPS = 2
SSD_STATE = 128
MLSTM_HEADS = 4
MLSTM_QK_DIM = 128
MLSTM_V_DIM = 256
MLSTM_QK = 512
MLSTM_INNER = 1024
CHUNK = 128
MIXER_BATCH = 2
HALO = 16
CONV_CH = 2560
REC_W = MLSTM_INNER + CONV_CH
N_GATE = 48
GATE_W = 128
COL_DT, COL_IG, COL_FG = 0, 32, 40

ATT_DH = 64
ATT_HEADS = 8
ATT_KV = 2
ATT_G = ATT_HEADS // ATT_KV
WINDOW = 128
WIN_QBLK = 128
WIN_SPAN = WIN_QBLK + 2 * WINDOW
DENSE_QBLK = 256
ODD_PROJ = 1536

N_GROUPS = 4
EXPERTS_PER_GROUP = 4
N_EXPERTS = N_GROUPS * EXPERTS_PER_GROUP
EXPERT_FF = 256
ROUTE_W = 128
EXPERT_LANE0 = N_GROUPS

ROW_TILE = 256
GIDX_LANE = 8
RUN_ALIGN = 16
RUN_PIECES = (256, 128, 64, 32, 16)
SORT_ROWS = -(-(ROW_TILE + N_GROUPS * (RUN_ALIGN - 1)) // 128) * 128
EXPERT_TILE = 512
VMEM_LIMIT = 56 * 1024 * 1024


def _cparams(*sem):
    return pltpu.CompilerParams(dimension_semantics=sem, vmem_limit_bytes=VMEM_LIMIT)


def _sigmoid(x):
    return 0.5 * jnp.tanh(0.5 * x) + 0.5


def _silu(x):
    return x * _sigmoid(x)


def _softplus(x):
    return jnp.maximum(x, 0.0) + jnp.log(1.0 + jnp.exp(-jnp.abs(x)))


def _dot(a, b):
    return jnp.dot(a, b, preferred_element_type=F32)


def _dot_nt(a, b):
    return lax.dot_general(a, b, (((1,), (1,)), ((), ())), preferred_element_type=F32)


def _split3(a):
    hi = a.astype(BF16)
    r1 = a - hi.astype(F32)
    mid = r1.astype(BF16)
    lo = (r1 - mid.astype(F32)).astype(BF16)
    return hi, mid, lo


def _dot3_rhs(exact, a):
    hi, mid, lo = _split3(a)
    return _dot(exact, hi) + _dot(exact, mid) + _dot(exact, lo)


def _dot3_lhs(a, exact):
    hi, mid, lo = _split3(a)
    return _dot(hi, exact) + _dot(mid, exact) + _dot(lo, exact)


def _adaln_kernel(cond_ref, w_ref, b_ref, o_ref):
    s = _silu(cond_ref[...])
    o_ref[0] = _dot(s.astype(BF16), w_ref[0].astype(BF16)) + b_ref[0]


def _adaln(cond, ada_w, ada_b):
    depth, d, n = ada_w.shape
    rows = cond.shape[0]
    tn = 1536
    return pl.pallas_call(
        _adaln_kernel,
        grid=(depth, n // tn),
        in_specs=[pl.BlockSpec((rows, d), lambda l, j: (0, 0)),
                  pl.BlockSpec((1, d, tn), lambda l, j: (l, 0, j)),
                  pl.BlockSpec((1, 1, tn), lambda l, j: (l, 0, j))],
        out_specs=pl.BlockSpec((1, rows, tn), lambda l, j: (l, 0, j)),
        out_shape=jax.ShapeDtypeStruct((depth, rows, n), F32),
        compiler_params=_cparams("parallel", "parallel"),
        name="adaln",
    )(cond, ada_w, ada_b.reshape(depth, 1, n))


def _norm_mod(x, nw, mod, shift_row, scale_row):
    var = jnp.mean(x * x, axis=-1, keepdims=True)
    y = x * lax.rsqrt(var + EPS) * nw
    return y * (1.0 + mod[scale_row:scale_row + 1]) + mod[shift_row:shift_row + 1]


def _nmm_kernel(h_ref, mod_ref, nw_ref, *rest, n_out, shift_row, scale_row, tn):
    w_refs, o_refs = rest[:n_out], rest[n_out:]
    a16 = _norm_mod(h_ref[0], nw_ref[...], mod_ref[0, 0], shift_row, scale_row).astype(BF16)
    for w_ref, o_ref in zip(w_refs, o_refs):
        n = w_ref.shape[1]
        for j in range(0, n, tn):
            w = min(tn, n - j)
            o_ref[0, :, j:j + w] = _dot(a16, w_ref[:, j:j + w]).astype(o_ref.dtype)


def _mod_spec(nct_tiles):
    return pl.BlockSpec((1, 1, 6, D_MODEL), lambda b, i: (b, (i >= nct_tiles).astype(jnp.int32), 0, 0))


def _norm_mod_matmul(h, mod, nw, weights, out_dtypes, shift_row, scale_row, ctx_len):
    B, L, d = h.shape
    tm = ROW_TILE
    in_specs = [pl.BlockSpec((1, tm, d), lambda b, i: (b, i, 0)),
                _mod_spec(ctx_len // tm),
                pl.BlockSpec((1, d), lambda b, i: (0, 0))]
    out_specs, out_shape = [], []
    for w, dt in zip(weights, out_dtypes):
        n = w.shape[1]
        in_specs.append(pl.BlockSpec((d, n), lambda b, i: (0, 0)))
        out_specs.append(pl.BlockSpec((1, tm, n), lambda b, i: (b, i, 0)))
        out_shape.append(jax.ShapeDtypeStruct((B, L, n), dt))
    return pl.pallas_call(
        functools.partial(_nmm_kernel, n_out=len(weights), shift_row=shift_row, scale_row=scale_row, tn=512),
        grid=(B, L // tm),
        in_specs=in_specs, out_specs=out_specs, out_shape=out_shape,
        compiler_params=_cparams("parallel", "parallel"),
        name="norm_mod_matmul",
    )(h, mod, nw.reshape(1, d), *weights)


def _mixer_kernel(rec_ref, prev_ref, next_ref, g_ref, gt_ref, cw_ref, cb_ref, prow_ref, pcol_ref, extra_ref,
                  y_ref, S_ref, C_ref, n_ref, m_ref, *, reverse, nct, nc, nb):
    i = pl.program_id(1)
    c = jnp.where(i < nct, nct - 1 - i, nc - 1 - (i - nct)) if reverse else i
    seq_start = jnp.logical_or(c == 0, c == nct)
    seq_end = jnp.logical_or(c == nct - 1, c == nc - 1)

    @pl.when(i == 0)
    def _():
        S_ref[...] = jnp.zeros_like(S_ref)
        C_ref[...] = jnp.zeros_like(C_ref)
        n_ref[...] = jnp.zeros_like(n_ref)
        m_ref[...] = jnp.zeros_like(m_ref)

    for bb in range(nb):
        xc = rec_ref[bb, :, MLSTM_INNER:].astype(F32)
        before = prev_ref[bb].astype(F32)[HALO - 1:HALO, MLSTM_INNER:]
        after = next_ref[bb].astype(F32)[0:1, MLSTM_INNER:]
        before = jnp.where(seq_start, 0.0, before)
        after = jnp.where(seq_end, 0.0, after)
        rid = lax.broadcasted_iota(jnp.int32, (CHUNK, 1), 0)
        x_prev = jnp.where(rid == 0, before, pltpu.roll(xc, 1, 0))
        x_next = jnp.where(rid == CHUNK - 1, after, pltpu.roll(xc, CHUNK - 1, 0))
        cw = cw_ref[...]
        conv = _silu(cw[0:1] * x_prev + cw[1:2] * xc + cw[2:3] * x_next + cb_ref[...])

        if reverse:
            def emit(col0, width, val, bb=bb):
                y_ref[bb, :, col0:col0 + width] = (extra_ref[bb, :, col0:col0 + width] + val).astype(y_ref.dtype)
        else:
            def emit(col0, width, val, bb=bb, conv=conv):
                if col0 < SSD_INNER:
                    val = val + conv[:, col0:col0 + width] * extra_ref[:, col0:col0 + width]
                y_ref[bb, :, col0:col0 + width] = val

        _mixer_chunk(rec_ref.at[bb], g_ref.at[bb], gt_ref.at[bb], prow_ref, pcol_ref, conv,
                     S_ref.at[bb], C_ref.at[bb], n_ref.at[bb], m_ref.at[bb], emit, reverse=reverse)


def _mixer_chunk(v_ref, g_ref, gt_ref, prow_ref, pcol_ref, conv_ref, S_ref, C_ref, n_ref, m_ref, emit, *, reverse):
    d = 1 if reverse else 0
    li = lax.broadcasted_iota(jnp.int32, (CHUNK, CHUNK), 0)
    si = lax.broadcasted_iota(jnp.int32, (CHUNK, CHUNK), 1)
    mask = (si >= li) if reverse else (si <= li)
    tri = mask.astype(BF16)
    tri_t = ((li >= si) if reverse else (li <= si)).astype(BF16)
    last = 0 if reverse else CHUNK - 1

    G = g_ref[...] + prow_ref[0:1]
    GT = gt_ref[...] + pcol_ref[:, 0:1]
    lane = lax.broadcasted_iota(jnp.int32, (CHUNK, GATE_W), 1)
    row = lax.broadcasted_iota(jnp.int32, (N_GATE, CHUNK), 0)
    dt_c = _softplus(jnp.where(lane < COL_IG, G, -G))
    dt_r = _softplus(jnp.where(row < COL_IG, GT, -GT))
    nega_c = -jnp.exp(prow_ref[1:2])
    nega_r = -jnp.exp(pcol_ref[:, 1:2])
    pre_c = jnp.where(lane < COL_IG, dt_c * nega_c, jnp.where(lane >= COL_FG, -dt_c, 0.0))
    pre_c = jnp.where(lane < N_GATE, pre_c, 0.0)
    pre_r = jnp.where(row < COL_IG, dt_r * nega_r, jnp.where(row >= COL_FG, -dt_r, 0.0))
    cum_c = _dot3_rhs(tri, pre_c)
    cum_r = _dot3_lhs(pre_r, tri_t)

    a_c = cum_c[:, d * SSD_HEADS:(d + 1) * SSD_HEADS]
    a_r = cum_r[d * SSD_HEADS:(d + 1) * SSD_HEADS, :]
    a_last = a_c[last:last + 1, :]
    hh = lax.broadcasted_iota(jnp.int32, (SSD_HEADS, SSD_INNER), 0)
    hj = lax.broadcasted_iota(jnp.int32, (SSD_HEADS, SSD_INNER), 1)
    expand = (hj // SSD_HEAD_DIM == hh).astype(BF16)
    dt_x = _dot3_lhs(dt_c[:, d * SSD_HEADS:(d + 1) * SSD_HEADS], expand)
    dec_x = _dot3_lhs(jnp.exp(a_last - a_c), expand)
    ein_x = _dot3_lhs(jnp.exp(a_c), expand)
    cdec_x = _dot3_lhs(jnp.broadcast_to(jnp.exp(a_last), (8, SSD_HEADS)), expand)[0:1]

    xsdt = conv_ref[:, 0:SSD_INNER] * dt_x
    xsdt16 = xsdt.astype(BF16)
    xdec16 = (xsdt * dec_x).astype(BF16)
    lane128 = lax.broadcasted_iota(jnp.int32, (CHUNK, 128), 1)
    heads_per_group = SSD_HEADS // SSD_GROUPS
    gw = heads_per_group * SSD_HEAD_DIM
    for g in range(SSD_GROUPS):
        Bg = conv_ref[:, SSD_INNER + g * SSD_STATE:SSD_INNER + (g + 1) * SSD_STATE]
        Cg16 = conv_ref[:, SSD_INNER + 256 + g * SSD_STATE:SSD_INNER + 256 + (g + 1) * SSD_STATE].astype(BF16)
        CB = _dot_nt(Cg16, Bg.astype(BF16))
        S_g = S_ref[:, g * gw:(g + 1) * gw]
        y_off = _dot(Cg16, S_g.astype(BF16)) * ein_x[:, g * gw:(g + 1) * gw]
        for pr in range(heads_per_group // 2):
            col0 = g * gw + pr * 128
            xpair = xsdt16[:, col0:col0 + 128]
            acc = None
            for half in range(2):
                h = g * heads_per_group + pr * 2 + half
                seg = a_c[:, h:h + 1] - a_r[h:h + 1, :]
                Lm = jnp.exp(jnp.where(mask, seg, NEG_INF))
                Mh = (CB * Lm).astype(BF16)
                keep = (lane128 < 64) if half == 0 else (lane128 >= 64)
                part = _dot(Mh, jnp.where(keep, xpair, jnp.zeros_like(xpair)))
                acc = part if acc is None else acc + part
            emit(col0, 128, acc + y_off[:, pr * 128:(pr + 1) * 128])
        S_ref[:, g * gw:(g + 1) * gw] = (cdec_x[:, g * gw:(g + 1) * gw] * S_g
                                         + _dot(Bg.T.astype(BF16), xdec16[:, g * gw:(g + 1) * gw]))

    for h in range(MLSTM_HEADS):
        gi = COL_IG + d * MLSTM_HEADS + h
        gf = COL_FG + d * MLSTM_HEADS + h
        qh16 = conv_ref[:, 1536 + h * 128:1536 + (h + 1) * 128].astype(BF16)
        kh = conv_ref[:, 2048 + h * 128:2048 + (h + 1) * 128] * (MLSTM_QK_DIM ** -0.5)
        kh16 = kh.astype(BF16)
        vh16 = v_ref[:, h * MLSTM_V_DIM:(h + 1) * MLSTM_V_DIM]
        ig_c, ig_r = G[:, gi:gi + 1], GT[gi:gi + 1, :]
        b_c, b_r = cum_c[:, gf:gf + 1], cum_r[gf:gf + 1, :]
        b_last = b_c[last:last + 1, :]
        Dlog = jnp.where(mask, b_c - b_r + ig_r, NEG_INF)
        m_loc = jnp.max(b_last - b_r + ig_r, axis=-1, keepdims=True)
        ek = jnp.exp(b_last - b_c + ig_c - m_loc) * kh
        C_loc = _dot(ek.T.astype(BF16), vh16)
        n_loc = jnp.sum(ek, axis=0, keepdims=True)
        C_in = C_ref[h]
        n_in = n_ref[h:h + 1, :]
        m_in = m_ref[h:h + 1, 0:1]
        m_inter = b_c + m_in
        m_t = jnp.maximum(m_inter, jnp.max(Dlog, axis=-1, keepdims=True))
        P = jnp.exp(Dlog - m_t) * _dot_nt(qh16, kh16)
        w_inter = jnp.exp(m_inter - m_t)
        num = _dot(P.astype(BF16), vh16) + w_inter * _dot(qh16, C_in.astype(BF16))
        qn = jnp.sum(qh16.astype(F32) * n_in, axis=-1, keepdims=True)
        den = jnp.sum(P, axis=-1, keepdims=True) + w_inter * qn
        den = jnp.maximum(jnp.abs(den), jnp.exp(-m_t))
        emit(SSD_INNER + h * MLSTM_V_DIM, MLSTM_V_DIM, num / den)
        m_new = jnp.maximum(b_last + m_in, m_loc)
        fa = jnp.exp(b_last + m_in - m_new)
        fb = jnp.exp(m_loc - m_new)
        C_ref[h] = fa * C_in + fb * C_loc
        n_ref[h:h + 1, :] = fa * n_in + fb * n_loc
        m_ref[h:h + 1, :] = jnp.broadcast_to(m_new, (1, 128))


def _mixer_scan(rec, gates, gates_t, conv_w, conv_b, prow, pcol, extra, ctx_len, reverse):
    B, L, _ = rec.shape
    nc, nct = L // CHUNK, ctx_len // CHUNK
    nb = MIXER_BATCH
    hb = CHUNK // HALO
    wide = SSD_INNER + MLSTM_INNER

    def chunk_of(i):
        return jnp.where(i < nct, nct - 1 - i, nc - 1 - (i - nct)) if reverse else i

    const = lambda r, n: pl.BlockSpec((r, n), lambda b, i: (0, 0))
    chunk = lambda n: pl.BlockSpec((nb, CHUNK, n), lambda b, i: (b, chunk_of(i), 0))
    extra_spec = chunk(wide) if reverse else const(1, SSD_INNER)
    return pl.pallas_call(
        functools.partial(_mixer_kernel, reverse=reverse, nct=nct, nc=nc, nb=nb),
        grid=(B // nb, nc),
        in_specs=[chunk(REC_W),
                  pl.BlockSpec((nb, HALO, REC_W), lambda b, i: (b, jnp.maximum(chunk_of(i) * hb - 1, 0), 0)),
                  pl.BlockSpec((nb, HALO, REC_W),
                               lambda b, i: (b, jnp.minimum((chunk_of(i) + 1) * hb, L // HALO - 1), 0)),
                  chunk(GATE_W),
                  pl.BlockSpec((nb, N_GATE, CHUNK), lambda b, i: (b, 0, chunk_of(i))),
                  const(3, CONV_CH), const(1, CONV_CH), const(2, GATE_W), const(N_GATE, 2), extra_spec],
        out_specs=chunk(wide),
        out_shape=jax.ShapeDtypeStruct((B, L, wide), BF16 if reverse else F32),
        scratch_shapes=[pltpu.VMEM((nb, SSD_STATE, SSD_INNER), F32),
                        pltpu.VMEM((nb, MLSTM_HEADS, MLSTM_QK_DIM, MLSTM_V_DIM), F32),
                        pltpu.VMEM((nb, 8, MLSTM_QK_DIM), F32),
                        pltpu.VMEM((nb, 8, 128), F32)],
        compiler_params=_cparams("parallel", "arbitrary"),
        name="mixer_bwd" if reverse else "mixer_fwd",
    )(rec, rec, rec, gates, gates_t, conv_w, conv_b, prow, pcol, extra)


def _group_rms(x, groups):
    w = x.shape[-1] // groups
    parts = []
    for g in range(groups):
        seg = x[:, g * w:(g + 1) * w]
        parts.append(seg * lax.rsqrt(jnp.mean(seg * seg, axis=-1, keepdims=True) + EPS))
    return jnp.concatenate(parts, axis=-1)


def _even_out_kernel(y_ref, zo_ref, h_ref, mod_ref, snw_ref, mnw_ref, w_ref, n2_ref, wr_ref, o_ref, a_ref, c_ref):
    y = y_ref[0].astype(F32)
    z = zo_ref[0, :, 0:SSD_INNER].astype(F32)
    o = zo_ref[0, :, SSD_INNER:].astype(F32)
    ys = _group_rms(y[:, 0:SSD_INNER] * _silu(z), SSD_GROUPS) * snw_ref[...]
    hm = _group_rms(y[:, SSD_INNER:], MLSTM_HEADS) * mnw_ref[...] * _sigmoid(o)
    mix = jnp.concatenate([ys, hm], axis=-1).astype(BF16)
    mod = mod_ref[0, 0]
    x = h_ref[0] + mod[2:3] * _dot(mix, w_ref[...])
    o_ref[0] = x
    _route_rows(x, mod, n2_ref, wr_ref, a_ref, c_ref)


def _even_out(y, zo, h, mod, snw, mnw, w_out, norm2_w, w_route, ctx_len):
    B, L, d = h.shape
    tm = ROW_TILE
    wide = SSD_INNER + MLSTM_INNER
    row = lambda n: pl.BlockSpec((1, tm, n), lambda b, i: (b, i, 0))
    const = lambda r, n: pl.BlockSpec((r, n), lambda b, i: (0, 0))
    r_in, r_out, r_shape = _route_specs(B, L, d, tm)
    return pl.pallas_call(
        _even_out_kernel,
        grid=(B, L // tm),
        in_specs=[row(wide), row(wide), row(d), _mod_spec(ctx_len // tm),
                  const(1, SSD_INNER), const(1, MLSTM_INNER), const(wide, d)] + r_in,
        out_specs=[row(d)] + r_out,
        out_shape=[jax.ShapeDtypeStruct((B, L, d), F32)] + r_shape,
        compiler_params=_cparams("parallel", "parallel"),
        name="even_out",
    )(y, zo, h, mod, snw.reshape(1, -1), mnw.reshape(1, -1), w_out, norm2_w.reshape(1, d), w_route)


def _qkv_kernel(h_ref, mod_ref, nw_ref, w_ref, cs_ref, sn_ref, qnw_ref, knw_ref,
                qs_ref, ks_ref, vs1_ref, qd_ref, kd_ref, vd1_ref, *, nct):
    is_lat = pl.program_id(1) >= nct
    a16 = _norm_mod(h_ref[0], nw_ref[...], mod_ref[0, 0], 0, 1).astype(BF16)
    p = _dot(a16, w_ref[...])
    cs, sn = cs_ref[...], sn_ref[...]

    def rope(x):
        w = x.shape[1]
        reps = w // 128
        c = jnp.concatenate([cs] * reps, axis=1) if reps > 1 else cs
        s = jnp.concatenate([sn] * reps, axis=1) if reps > 1 else sn
        lane = lax.broadcasted_iota(jnp.int32, x.shape, 1)
        partner = jnp.where(lane % 2 == 0, pltpu.roll(x, w - 1, 1), pltpu.roll(x, 1, 1))
        return jnp.where(is_lat, x * c + partner * s, x)

    def head_rms(x, nw):
        w = x.shape[1]
        a = lax.broadcasted_iota(jnp.int32, (w, w), 0) // ATT_DH
        b = lax.broadcasted_iota(jnp.int32, (w, w), 1) // ATT_DH
        same = (a == b).astype(BF16)
        ms = _dot3_lhs(x * x, same) * (1.0 / ATT_DH)
        return x * lax.rsqrt(ms + EPS) * nw

    qw, kw = ATT_HEADS * ATT_DH, ATT_KV * ATT_DH
    o = 0
    q_s = rope(p[:, o:o + qw]) * (ATT_DH ** -0.5 * LOG2E); o += qw
    k_s = rope(p[:, o:o + kw]); o += kw
    v_s = p[:, o:o + kw]; o += kw
    q_d = rope(head_rms(p[:, o:o + qw], qnw_ref[...])) * (ATT_DH ** -0.5 * LOG2E); o += qw
    k_d = rope(head_rms(p[:, o:o + kw], knw_ref[...])); o += kw
    v_d = p[:, o:o + kw]

    lane = lax.broadcasted_iota(jnp.int32, (p.shape[0], 128), 1)
    for h in range(ATT_HEADS):
        qs_ref[0, h] = q_s[:, h * ATT_DH:(h + 1) * ATT_DH].astype(BF16)
        qd_ref[0, h] = q_d[:, h * ATT_DH:(h + 1) * ATT_DH].astype(BF16)
    for kv in range(ATT_KV):
        ks_ref[0, kv] = k_s[:, kv * ATT_DH:(kv + 1) * ATT_DH].astype(BF16)
        kd_ref[0, kv] = k_d[:, kv * ATT_DH:(kv + 1) * ATT_DH].astype(BF16)
    for v, v1_ref in ((v_s, vs1_ref), (v_d, vd1_ref)):
        for kv in range(ATT_KV):
            own = jnp.where((lane // ATT_DH) == kv, v, 0.0)
            low = own if kv == 0 else pltpu.roll(own, ATT_DH, 1)
            v1_ref[0, kv] = jnp.where(lane == ATT_DH, 1.0, low).astype(BF16)


def _qkv(h, mod, nw, w_in, cosf, sinf, qnw, knw, ctx_len):
    B, L, d = h.shape
    tm = ROW_TILE
    nct = ctx_len // tm
    const = lambda r, n: pl.BlockSpec((r, n), lambda b, i: (0, 0))
    tab = pl.BlockSpec((tm, 128), lambda b, i: (jnp.maximum(i - nct, 0), 0))
    head = lambda n, w: pl.BlockSpec((1, n, tm, w), lambda b, i: (b, 0, i, 0))
    shp = lambda n, w: jax.ShapeDtypeStruct((B, n, L, w), BF16)
    return pl.pallas_call(
        functools.partial(_qkv_kernel, nct=nct),
        grid=(B, L // tm),
        in_specs=[pl.BlockSpec((1, tm, d), lambda b, i: (b, i, 0)), _mod_spec(nct), const(1, d), const(d, ODD_PROJ),
                  tab, tab, const(1, ATT_HEADS * ATT_DH), const(1, ATT_KV * ATT_DH)],
        out_specs=[head(ATT_HEADS, ATT_DH), head(ATT_KV, ATT_DH), head(ATT_KV, 128)] * 2,
        out_shape=[shp(ATT_HEADS, ATT_DH), shp(ATT_KV, ATT_DH), shp(ATT_KV, 128)] * 2,
        compiler_params=_cparams("parallel", "parallel"),
        name="qkv",
    )(h, mod, nw.reshape(1, d), w_in, cosf, sinf, qnw, knw)


def _group_softmax_out(scores, values, rows, sink=None):
    m = functools.reduce(jnp.maximum, [jnp.max(s, axis=-1, keepdims=True) for s in scores])
    if sink is not None:
        m = jnp.maximum(m, sink)
    probs = [jnp.exp2(s - m).astype(BF16) for s in scores]
    lane = lax.broadcasted_iota(jnp.int32, (rows, 128), 1)
    outs = []
    for pr in range(ATT_G // 2):
        heads = []
        for g in (2 * pr, 2 * pr + 1):
            rs = slice(g * rows, (g + 1) * rows)
            O = functools.reduce(lambda a, b: a + b, [_dot(p[rs], v) for p, v in zip(probs, values)])
            l = O[:, ATT_DH:ATT_DH + 1]
            if sink is not None:
                l = l + jnp.exp2(sink[rs] - m[rs])
            heads.append(O * (1.0 / l))
        outs.append(jnp.where(lane < ATT_DH, heads[0], pltpu.roll(heads[1], ATT_DH, 1)))
    return jnp.concatenate(outs, axis=-1)


def _dense_attn_kernel(q_ref, k_ref, v1_ref, o_ref):
    for kv in range(ATT_KV):
        Q = q_ref[0, kv * ATT_G:(kv + 1) * ATT_G].reshape(ATT_G * DENSE_QBLK, ATT_DH)
        o_ref[0, :, kv * ATT_G * ATT_DH:(kv + 1) * ATT_G * ATT_DH] = _group_softmax_out(
            [_dot_nt(Q, k_ref[0, kv])], [v1_ref[0, kv]], DENSE_QBLK).astype(o_ref.dtype)


def _dense_attn(qd, kd, vd1, ctx_len):
    B, _, L, _ = qd.shape
    S = L - ctx_len
    nct = ctx_len // DENSE_QBLK
    full = lambda w: pl.BlockSpec((1, ATT_KV, L, w), lambda b, j: (b, 0, 0, 0))
    return pl.pallas_call(
        _dense_attn_kernel,
        grid=(B, S // DENSE_QBLK),
        in_specs=[pl.BlockSpec((1, ATT_HEADS, DENSE_QBLK, ATT_DH), lambda b, j: (b, 0, j + nct, 0)),
                  full(ATT_DH), full(128)],
        out_specs=pl.BlockSpec((1, DENSE_QBLK, ATT_HEADS * ATT_DH), lambda b, j: (b, j, 0)),
        out_shape=jax.ShapeDtypeStruct((B, S, ATT_HEADS * ATT_DH), BF16),
        compiler_params=_cparams("parallel", "parallel"),
        name="dense_attn",
    )(qd, kd, vd1)


def _sink_rows(sink_ref, kv, rows):
    return jnp.concatenate(
        [jnp.broadcast_to(sink_ref[kv * ATT_G + g:kv * ATT_G + g + 1, 0:1], (rows, 1)) for g in range(ATT_G)], axis=0)


def _window_attn_kernel(q_ref, k_ref, v1_ref, sink_ref, o_ref, *, ctx_len, seq):
    j = pl.program_id(1)
    span = WIN_SPAN
    start = jnp.clip(j * WIN_QBLK - WINDOW, 0, seq - span)
    kstart = pl.multiple_of(ctx_len + start, 128)
    rows = ATT_G * WIN_QBLK
    qpos = j * WIN_QBLK + lax.broadcasted_iota(jnp.int32, (rows, span), 0) % WIN_QBLK
    kpos = start + lax.broadcasted_iota(jnp.int32, (rows, span), 1)
    valid = jnp.abs(kpos - qpos) <= WINDOW
    for kv in range(ATT_KV):
        Q = q_ref[0, kv * ATT_G:(kv + 1) * ATT_G].reshape(rows, ATT_DH)
        Sc = _dot_nt(Q, k_ref[0, kv, 0:ctx_len])
        Sl = jnp.where(valid, _dot_nt(Q, k_ref[0, kv, pl.ds(kstart, span)]), NEG_INF)
        out = _group_softmax_out([Sc, Sl], [v1_ref[0, kv, 0:ctx_len], v1_ref[0, kv, pl.ds(kstart, span)]], WIN_QBLK,
                                 sink=_sink_rows(sink_ref, kv, WIN_QBLK) * LOG2E)
        o_ref[0, :, kv * ATT_G * ATT_DH:(kv + 1) * ATT_G * ATT_DH] = out.astype(o_ref.dtype)


def _window_attn(qs, ks, vs1, sink_b, ctx_len):
    B, _, L, _ = qs.shape
    S = L - ctx_len
    nct = ctx_len // WIN_QBLK
    full = lambda w: pl.BlockSpec((1, ATT_KV, L, w), lambda b, j: (b, 0, 0, 0))
    return pl.pallas_call(
        functools.partial(_window_attn_kernel, ctx_len=ctx_len, seq=S),
        grid=(B, S // WIN_QBLK),
        in_specs=[pl.BlockSpec((1, ATT_HEADS, WIN_QBLK, ATT_DH), lambda b, j: (b, 0, j + nct, 0)),
                  full(ATT_DH), full(128),
                  pl.BlockSpec((ATT_HEADS, 128), lambda b, j: (0, 0))],
        out_specs=pl.BlockSpec((1, WIN_QBLK, ATT_HEADS * ATT_DH), lambda b, j: (b, j, 0)),
        out_shape=jax.ShapeDtypeStruct((B, S, ATT_HEADS * ATT_DH), BF16),
        compiler_params=_cparams("parallel", "parallel"),
        name="window_attn",
    )(qs, ks, vs1, sink_b)


def _ctx_attn_kernel(qs_ref, ks_ref, vs1_ref, qd_ref, kd_ref, vd1_ref, sink_ref, os_ref, od_ref, *, ctx_len):
    T = ctx_len
    for kv in range(ATT_KV):
        cols = slice(kv * ATT_G * ATT_DH, (kv + 1) * ATT_G * ATT_DH)
        Q = qs_ref[0, kv * ATT_G:(kv + 1) * ATT_G].reshape(ATT_G * T, ATT_DH)
        os_ref[0, :, cols] = _group_softmax_out([_dot_nt(Q, ks_ref[0, kv])], [vs1_ref[0, kv]], T,
                                                sink=_sink_rows(sink_ref, kv, T) * LOG2E).astype(os_ref.dtype)
        Q = qd_ref[0, kv * ATT_G:(kv + 1) * ATT_G].reshape(ATT_G * T, ATT_DH)
        od_ref[0, :, cols] = _group_softmax_out([_dot_nt(Q, kd_ref[0, kv])], [vd1_ref[0, kv]], T).astype(od_ref.dtype)


def _ctx_attn(qs, ks, vs1, qd, kd, vd1, sink_b, ctx_len):
    B = qs.shape[0]
    T = ctx_len
    blk = lambda n, w: pl.BlockSpec((1, n, T, w), lambda b: (b, 0, 0, 0))
    out = pl.BlockSpec((1, T, ATT_HEADS * ATT_DH), lambda b: (b, 0, 0))
    return pl.pallas_call(
        functools.partial(_ctx_attn_kernel, ctx_len=T),
        grid=(B,),
        in_specs=[blk(ATT_HEADS, ATT_DH), blk(ATT_KV, ATT_DH), blk(ATT_KV, 128),
                  blk(ATT_HEADS, ATT_DH), blk(ATT_KV, ATT_DH), blk(ATT_KV, 128),
                  pl.BlockSpec((ATT_HEADS, 128), lambda b: (0, 0))],
        out_specs=[out, out],
        out_shape=[jax.ShapeDtypeStruct((B, T, ATT_HEADS * ATT_DH), BF16)] * 2,
        compiler_params=_cparams("parallel"),
        name="ctx_attn",
    )(qs, ks, vs1, qd, kd, vd1, sink_b)


def _odd_out_kernel(ysc_ref, ydc_ref, ysl_ref, ydl_ref, h_ref, mod_ref, w_ref, n2_ref, wr_ref,
                    o_ref, a_ref, c_ref, *, nct):
    is_ctx = pl.program_id(1) < nct
    ys = jnp.where(is_ctx, ysc_ref[0], ysl_ref[0])
    yd = jnp.where(is_ctx, ydc_ref[0], ydl_ref[0])
    mix = jnp.concatenate([ys, yd], axis=-1)
    mod = mod_ref[0, 0]
    x = h_ref[0] + mod[2:3] * _dot(mix, w_ref[...])
    o_ref[0] = x
    _route_rows(x, mod, n2_ref, wr_ref, a_ref, c_ref)


def _odd_out(ys_ctx, yd_ctx, ys_lat, yd_lat, h, mod, w_out, norm2_w, w_route, ctx_len):
    B, L, d = h.shape
    tm = ROW_TILE
    nct = ctx_len // tm
    half = ATT_HEADS * ATT_DH
    row = lambda n: pl.BlockSpec((1, tm, n), lambda b, i: (b, i, 0))
    ctx = pl.BlockSpec((1, tm, half), lambda b, i: (b, jnp.minimum(i, nct - 1), 0))
    lat = pl.BlockSpec((1, tm, half), lambda b, i: (b, jnp.maximum(i - nct, 0), 0))
    r_in, r_out, r_shape = _route_specs(B, L, d, tm)
    return pl.pallas_call(
        functools.partial(_odd_out_kernel, nct=nct),
        grid=(B, L // tm),
        in_specs=[ctx, ctx, lat, lat, row(d), _mod_spec(nct),
                  pl.BlockSpec((2 * half, d), lambda b, i: (0, 0))] + r_in,
        out_specs=[row(d)] + r_out,
        out_shape=[jax.ShapeDtypeStruct((B, L, d), F32)] + r_shape,
        compiler_params=_cparams("parallel", "parallel"),
        name="odd_out",
    )(ys_ctx, yd_ctx, ys_lat, yd_lat, h, mod, w_out, norm2_w.reshape(1, d), w_route)


def _first_lane_of(cond, lane):
    return jnp.min(jnp.where(cond, lane, ROUTE_W), axis=-1, keepdims=True)


def _route_rows(x, mod, nw_ref, wr_ref, a_ref, c_ref):
    a = _norm_mod(x, nw_ref[...], mod, 3, 4)
    a_hi = a.astype(BF16)
    a_ref[0, :, :D_MODEL] = a_hi
    a_lo = (a - a_hi.astype(F32)).astype(BF16)
    w = wr_ref[...]
    w_hi = w.astype(BF16)
    w_lo = (w - w_hi.astype(F32)).astype(BF16)
    logits = _dot(a_hi, w_hi) + (_dot(a_lo, w_hi) + _dot(a_hi, w_lo))
    lane = lax.broadcasted_iota(jnp.int32, logits.shape, 1)
    gl = jnp.where(lane < N_GROUPS, logits, NEG_INF)
    gmax = jnp.max(gl, axis=-1, keepdims=True)
    p_top = 1.0 / jnp.sum(jnp.exp(gl - gmax), axis=-1, keepdims=True)
    g_idx = _first_lane_of(gl == gmax, lane)
    in_group = jnp.logical_and(lane >= EXPERT_LANE0, (lane - EXPERT_LANE0) // EXPERTS_PER_GROUP == g_idx)
    in_group = jnp.logical_and(in_group, lane < EXPERT_LANE0 + N_EXPERTS)
    el = jnp.where(in_group, logits, NEG_INF)
    m1 = jnp.max(el, axis=-1, keepdims=True)
    i1 = _first_lane_of(el == m1, lane)
    el2 = jnp.where(lane == i1, NEG_INF, el)
    m2 = jnp.max(el2, axis=-1, keepdims=True)
    i2 = _first_lane_of(el2 == m2, lane)
    e2 = jnp.exp(m2 - m1)
    w1 = 1.0 / (1.0 + e2) * p_top
    w2 = e2 / (1.0 + e2) * p_top
    j1 = i1 - EXPERT_LANE0 - EXPERTS_PER_GROUP * g_idx
    j2 = i2 - EXPERT_LANE0 - EXPERTS_PER_GROUP * g_idx
    both = lambda j, w: jnp.where(jnp.logical_or(lane == j, lane == j + EXPERTS_PER_GROUP), w, 0.0)
    gate = both(j1, w1) + both(j2, w2)
    gate_hi = gate.astype(BF16).astype(F32)
    tail = jnp.where(lane < EXPERTS_PER_GROUP, gate_hi, gate - gate_hi)
    tail = jnp.where(lane == GIDX_LANE, g_idx.astype(F32), tail)
    a_ref[0, :, D_MODEL:] = tail.astype(BF16)
    counts =jnp.sum((lane == g_idx).astype(jnp.int32), axis=0, keepdims=True)
    c_ref[0, 0] = jnp.broadcast_to(counts, (8, ROUTE_W))


def _route_specs(B, L, d, tm):
    in_specs = [pl.BlockSpec((1, d), lambda b, i: (0, 0)), pl.BlockSpec((d, ROUTE_W), lambda b, i: (0, 0))]
    out_specs = [pl.BlockSpec((1, tm, d + ROUTE_W), lambda b, i: (b, i, 0)),
                 pl.BlockSpec((1, 1, 8, ROUTE_W), lambda b, i: (b, i, 0, 0))]
    out_shape = [jax.ShapeDtypeStruct((B, L, d + ROUTE_W), BF16),
                 jax.ShapeDtypeStruct((B, L // tm, 8, ROUTE_W), jnp.int32)]
    return in_specs, out_specs, out_shape


def _route_plan(counts, n_row_tiles):
    pc = (counts + RUN_ALIGN - 1) // RUN_ALIGN * RUN_ALIGN
    region = (jnp.sum(pc, axis=0) + EXPERT_TILE - 1) // EXPERT_TILE * EXPERT_TILE
    region_end = jnp.cumsum(region)
    off = (region_end - region)[None, :] + jnp.cumsum(pc, axis=0) - pc
    n_used = (region_end[-1] // EXPERT_TILE).reshape(1)
    tile_row0 = jnp.arange(n_row_tiles, dtype=jnp.int32) * EXPERT_TILE
    tile_group = jnp.minimum(jnp.sum(tile_row0[:, None] >= region_end[None, :], axis=1), N_GROUPS - 1)
    return (pc.reshape(-1).astype(jnp.int32), off.reshape(-1).astype(jnp.int32),
            tile_group.astype(jnp.int32), n_used.astype(jnp.int32))


def _run_bases(pc_ref, t):
    bases, base = [], 0
    for g in range(N_GROUPS):
        bases.append(base)
        base = base + pc_ref[t * N_GROUPS + g]
    return bases


def _tile_slots(G, bases):
    tm = G.shape[0]
    lane = lax.broadcasted_iota(jnp.int32, (tm, ROUTE_W), 1)
    g_idx = G[:, GIDX_LANE:GIDX_LANE + 1].astype(jnp.int32)
    onehot = (lane == g_idx).astype(BF16)
    ti = lax.broadcasted_iota(jnp.int32, (tm, tm), 0)
    tj = lax.broadcasted_iota(jnp.int32, (tm, tm), 1)
    earlier = _dot((tj < ti).astype(BF16), onehot)
    slot = jnp.sum(jnp.where(lane == g_idx, earlier, 0.0), axis=-1, keepdims=True).astype(jnp.int32)
    for g in range(N_GROUPS):
        slot = slot + jnp.where(g_idx == g, bases[g], 0)
    return slot


def _run_copies(pc_ref, off_ref, t, buf, pairs, sems, to_sorted, live):
    bases = _run_bases(pc_ref, t)
    out = []
    for g in range(N_GROUPS):
        n = pc_ref[t * N_GROUPS + g]
        dst0 = off_ref[t * N_GROUPS + g]
        for k, size in enumerate(RUN_PIECES):
            po = n & ~(2 * size - 1)
            present = jnp.logical_and(live, (n & size) != 0)
            v0 = pl.multiple_of(bases[g] + po, RUN_ALIGN)
            h0 = pl.multiple_of(dst0 + po, RUN_ALIGN)
            for a, (v_ref, h_ref) in enumerate(pairs):
                v, hb = v_ref.at[buf, pl.ds(v0, size)], h_ref.at[pl.ds(h0, size)]
                src, dst = (v, hb) if to_sorted else (hb, v)
                out.append((present, pltpu.make_async_copy(src, dst, sems.at[buf, a, g, k])))
    return out


def _start(copies):
    for present, cp in copies:
        pl.when(present)(cp.start)


def _wait(copies):
    for present, cp in copies:
        pl.when(present)(cp.wait)


def _dispatch_kernel(pc_ref, off_ref, a_ref, xs_in, xs_ref, sx_ref, sems):
    del xs_in
    n_tiles = pl.num_programs(0) * pl.num_programs(1)
    t = pl.program_id(0) * pl.num_programs(1) + pl.program_id(1)
    buf = t % 2
    G = a_ref[0, :, D_MODEL:].astype(F32)
    slot = _tile_slots(G, _run_bases(pc_ref, t))
    tm = G.shape[0]
    lane = lax.broadcasted_iota(jnp.int32, (tm, SORT_ROWS), 1)
    perm = (lane == slot).astype(F32).T.astype(BF16)
    sx_ref[buf] = _dot(perm, a_ref[0]).astype(BF16)
    pairs = [(sx_ref, xs_ref)]
    _start(_run_copies(pc_ref, off_ref, t, buf, pairs, sems, True, True))
    _wait(_run_copies(pc_ref, off_ref, jnp.maximum(t - 1, 0), 1 - buf, pairs, sems, True, t >= 1))
    _wait(_run_copies(pc_ref, off_ref, t, buf, pairs, sems, True, t == n_tiles - 1))


def _dispatch(a2, pc, off, n_rows):
    B, L, w = a2.shape
    tm = ROW_TILE
    anyspec = pl.BlockSpec(memory_space=pl.ANY)
    return pl.pallas_call(
        _dispatch_kernel,
        grid_spec=pltpu.PrefetchScalarGridSpec(
            num_scalar_prefetch=2, grid=(B, L // tm),
            in_specs=[pl.BlockSpec((1, tm, w), lambda b, i, *_: (b, i, 0)), anyspec],
            out_specs=anyspec,
            scratch_shapes=[pltpu.VMEM((2, SORT_ROWS, w), BF16),
                            pltpu.SemaphoreType.DMA((2, 1, N_GROUPS, len(RUN_PIECES)))]),
        out_shape=jax.ShapeDtypeStruct((n_rows, w), BF16),
        input_output_aliases={3: 0},
        compiler_params=_cparams("arbitrary", "arbitrary"),
        name="moe_dispatch",
    )(pc, off, a2, jnp.zeros((n_rows, w), BF16))


def _group_experts_kernel(tg_ref, nu_ref, x_ref, wg_ref, wu_ref, wd_ref, y_ref):
    del tg_ref
    used = pl.program_id(0) < nu_ref[0]

    @pl.when(jnp.logical_not(used))
    def _():
        y_ref[...] = jnp.zeros_like(y_ref)

    @pl.when(used)
    def _():
        x = x_ref[:, :D_MODEL]
        hid = _silu(_dot(x, wg_ref[0])) * _dot(x, wu_ref[0])
        tail = x_ref[:, D_MODEL:].astype(F32)
        gates = tail[:, 0:EXPERTS_PER_GROUP] + tail[:, EXPERTS_PER_GROUP:2 * EXPERTS_PER_GROUP]
        hid16 = jnp.concatenate(
            [(hid[:, e * EXPERT_FF:(e + 1) * EXPERT_FF] * gates[:, e:e + 1]).astype(BF16)
             for e in range(EXPERTS_PER_GROUP)], axis=-1)
        y_ref[...] = _dot(hid16, wd_ref[0]).astype(y_ref.dtype)


def _group_experts(xs, tile_group, n_used, wg, wu, wd):
    n_rows, w = xs.shape
    d = D_MODEL
    tm = EXPERT_TILE
    ff = EXPERTS_PER_GROUP * EXPERT_FF
    wspec = lambda r, c: pl.BlockSpec((1, r, c), lambda i, tg, nu: (tg[jnp.minimum(i, nu[0] - 1)], 0, 0))
    return pl.pallas_call(
        _group_experts_kernel,
        grid_spec=pltpu.PrefetchScalarGridSpec(
            num_scalar_prefetch=2, grid=(n_rows // tm,),
            in_specs=[pl.BlockSpec((tm, w), lambda i, tg, nu: (jnp.minimum(i, nu[0] - 1), 0)),
                      wspec(d, ff), wspec(d, ff), wspec(ff, d)],
            out_specs=pl.BlockSpec((tm, d), lambda i, tg, nu: (i, 0))),
        out_shape=jax.ShapeDtypeStruct((n_rows, d), BF16),
        compiler_params=_cparams("arbitrary"),
        name="moe_experts",
    )(tile_group, n_used, xs, wg, wu, wd)


def _window_copies(pc_ref, off_ref, t, buf, sy_ref, ys_ref, sems, live):
    out = []
    for g in range(N_GROUPS):
        present = jnp.logical_and(live, pc_ref[t * N_GROUPS + g] > 0)
        h0 = pl.multiple_of(off_ref[t * N_GROUPS + g], RUN_ALIGN)
        out.append((present, pltpu.make_async_copy(ys_ref.at[pl.ds(h0, ROW_TILE)],
                                                   sy_ref.at[buf, pl.ds(g * ROW_TILE, ROW_TILE)], sems.at[buf, g])))
    return out


def _combine_kernel(pc_ref, off_ref, g_ref, h_ref, mod_ref, fw_ref, ys_ref, o_ref, sy_ref, sems, *, final):
    n_tiles = pl.num_programs(0) * pl.num_programs(1)
    t = pl.program_id(0) * pl.num_programs(1) + pl.program_id(1)
    buf = t % 2

    @pl.when(t == 0)
    def _():
        sy_ref[...] = jnp.zeros_like(sy_ref)

    _start(_window_copies(pc_ref, off_ref, t, buf, sy_ref, ys_ref, sems, t == 0))
    _start(_window_copies(pc_ref, off_ref, jnp.minimum(t + 1, n_tiles - 1), 1 - buf, sy_ref, ys_ref, sems,
                          t + 1 < n_tiles))
    G = g_ref[0].astype(F32)
    tm = G.shape[0]
    slot = _tile_slots(G, [g * tm for g in range(N_GROUPS)])
    lane = lax.broadcasted_iota(jnp.int32, (tm, N_GROUPS * tm), 1)
    pick = (lane == slot).astype(BF16)
    _wait(_window_copies(pc_ref, off_ref, t, buf, sy_ref, ys_ref, sems, True))
    x = h_ref[0] + mod_ref[0, 0][5:6] * _dot(pick, sy_ref[buf])
    if final:
        x = x * lax.rsqrt(jnp.mean(x * x, axis=-1, keepdims=True) + EPS) * fw_ref[...]
    o_ref[0] = x


def _combine(ys, a2, h, mod, pc, off, final_w, ctx_len, final):
    B, L, d = h.shape
    tm = ROW_TILE
    nct = ctx_len // tm
    row = lambda n: pl.BlockSpec((1, tm, n), lambda b, i, *_: (b, i, 0))
    if final:
        out_spec = pl.BlockSpec((1, tm, d), lambda b, i, *_: (b, jnp.maximum(i - nct, 0), 0))
        out_shape = jax.ShapeDtypeStruct((B, L - ctx_len, d), F32)
    else:
        out_spec, out_shape = row(d), jax.ShapeDtypeStruct((B, L, d), F32)
    return pl.pallas_call(
        functools.partial(_combine_kernel, final=final),
        grid_spec=pltpu.PrefetchScalarGridSpec(
            num_scalar_prefetch=2, grid=(B, L // tm),
            in_specs=[pl.BlockSpec((1, tm, ROUTE_W), lambda b, i, *_: (b, i, d // ROUTE_W)),
                      row(d),
                      pl.BlockSpec((1, 1, 6, d), lambda b, i, *_: (b, (i >= nct).astype(jnp.int32), 0, 0)),
                      pl.BlockSpec((1, d), lambda b, i, *_: (0, 0)),
                      pl.BlockSpec(memory_space=pl.ANY)],
            out_specs=out_spec,
            scratch_shapes=[pltpu.VMEM((2, N_GROUPS * tm, d), BF16),
                            pltpu.SemaphoreType.DMA((2, N_GROUPS))]),
        out_shape=out_shape,
        compiler_params=_cparams("arbitrary", "arbitrary"),
        name="moe_combine",
    )(pc, off, a2, h, mod, final_w.reshape(1, d), ys)


def _rope_tables(seq):
    rows = seq // GRID_W
    row = jnp.repeat(jnp.arange(rows), GRID_W).astype(F32)
    col = (jnp.arange(rows * GRID_W) % GRID_W).astype(F32)
    axis_dim = ATT_DH // 2
    inv = ROPE_THETA ** (-jnp.arange(0, axis_dim, 2, dtype=F32) / axis_dim)
    ang = jnp.concatenate([row[:, None] * inv, col[:, None] * inv], axis=-1)
    cosf = jnp.repeat(jnp.cos(ang), 2, axis=-1)
    sinf = jnp.repeat(jnp.sin(ang), 2, axis=-1) * jnp.tile(jnp.array([-1.0, 1.0], F32), ATT_DH // 2)
    return jnp.tile(cosf, (1, 2)), jnp.tile(sinf, (1, 2))


def kernel(x, c, ctx, c_ctx, ada_w, ada_b, norm1_w, norm2_w, ev_w_in, ev_conv_w, ev_conv_b, ev_dt_bias, ev_a_log, ev_d_skip, ev_ssd_norm_w, ev_ig_bias, ev_fg_bias, ev_mlstm_norm_w, ev_w_out, od_w_in, od_sink, od_q_norm_w, od_k_norm_w, od_w_out, moe_w_group, moe_w_expert, moe_w_gate, moe_w_up, moe_w_down, final_norm_w):
    B, S, d = x.shape
    T = ctx.shape[1]
    depth = ada_w.shape[0]
    assert d == D_MODEL and T % ROW_TILE == 0 and S % ROW_TILE == 0 and S % GRID_W == 0 and S >= WIN_SPAN

    h = jnp.concatenate([ctx, x], axis=1)

    rows = -(-(B + 1) // 8) * 8
    cond = jnp.zeros((rows, d), F32).at[:B].set(c).at[B].set(c_ctx)
    ada = _adaln(cond, ada_w, ada_b).reshape(depth, rows, 6, d)
    mods = jnp.stack([jnp.broadcast_to(ada[:, B:B + 1], (depth, B, 6, d)), ada[:, :B]], axis=2)

    cosf, sinf = _rope_tables(S)

    for layer in range(depth):
        li = layer // 2
        mod = mods[layer]
        w_route = jnp.pad(jnp.concatenate([moe_w_group[layer], moe_w_expert[layer]], axis=1),
                          ((0, 0), (0, ROUTE_W - N_GROUPS - N_EXPERTS)))
        if layer % 2 == 0:
            w_in = ev_w_in[li]
            conv, z, v, o, gate_cols = (w_in[:, :CONV_CH], w_in[:, CONV_CH:CONV_CH + SSD_INNER],
                                        w_in[:, CONV_CH + SSD_INNER:CONV_CH + 2 * SSD_INNER],
                                        w_in[:, CONV_CH + 2 * SSD_INNER:CONV_CH + 3 * SSD_INNER],
                                        w_in[:, CONV_CH + 3 * SSD_INNER:])
            w_rec = jnp.concatenate([v, conv], axis=1).astype(BF16)
            w_zo = jnp.concatenate([z, o], axis=1).astype(BF16)
            w_g = jnp.pad(gate_cols, ((0, 0), (0, GATE_W - N_GATE))).astype(BF16)
            rec, zo, gates = _norm_mod_matmul(h, mod, norm1_w[layer], [w_rec, w_zo, w_g], [BF16, BF16, F32], 0, 1, T)
            gates_t = jnp.swapaxes(gates[:, :, :N_GATE], 1, 2)
            bias = jnp.concatenate([ev_dt_bias[li].reshape(-1), ev_ig_bias[li].reshape(-1), ev_fg_bias[li].reshape(-1)])
            alog = jnp.pad(ev_a_log[li].reshape(-1), (0, N_GATE - 2 * SSD_HEADS))
            prow = jnp.pad(jnp.stack([bias, alog]), ((0, 0), (0, GATE_W - N_GATE)))
            pcol = jnp.stack([bias, alog], axis=1)
            dskip = jnp.repeat(ev_d_skip[li], SSD_HEAD_DIM).reshape(1, SSD_INNER)
            cb = ev_conv_b[li].reshape(1, CONV_CH)
            yf = _mixer_scan(rec, gates, gates_t, ev_conv_w[li], cb, prow, pcol, dskip, T, False)
            y = _mixer_scan(rec, gates, gates_t, ev_conv_w[li], cb, prow, pcol, yf, T, True)
            h, a2, counts = _even_out(y, zo, h, mod, ev_ssd_norm_w[li], ev_mlstm_norm_w[li],
                                      ev_w_out[li].astype(BF16), norm2_w[layer], w_route, T)
        else:
            qnw = jnp.tile(od_q_norm_w[li], ATT_HEADS).reshape(1, -1)
            knw = jnp.tile(od_k_norm_w[li], ATT_KV).reshape(1, -1)
            qs, ks, vs1, qd, kd, vd1 = _qkv(h, mod, norm1_w[layer], od_w_in[li].astype(BF16), cosf, sinf, qnw, knw, T)
            sink_b = jnp.broadcast_to(od_sink[li].reshape(ATT_HEADS, 1), (ATT_HEADS, 128))
            ys_lat = _window_attn(qs, ks, vs1, sink_b, T)
            yd_lat = _dense_attn(qd, kd, vd1, T)
            ys_ctx, yd_ctx = _ctx_attn(qs, ks, vs1, qd, kd, vd1, sink_b, T)
            h, a2, counts = _odd_out(ys_ctx, yd_ctx, ys_lat, yd_lat, h, mod, od_w_out[li].astype(BF16),
                                     norm2_w[layer], w_route, T)

        n_tiles = B * ((T + S) // ROW_TILE)
        n_rows = -(-(n_tiles * (ROW_TILE + N_GROUPS * RUN_ALIGN) + N_GROUPS * EXPERT_TILE + ROW_TILE)
                   // EXPERT_TILE) * EXPERT_TILE
        pc, off, tile_group, n_used = _route_plan(counts[:, :, 0, :N_GROUPS].reshape(n_tiles, N_GROUPS),
                                                  n_rows // EXPERT_TILE)
        xs = _dispatch(a2, pc, off, n_rows)
        ff = EXPERTS_PER_GROUP * EXPERT_FF
        wg = moe_w_gate[layer].transpose(0, 2, 1, 3).reshape(N_GROUPS, d, ff).astype(BF16)
        wu = moe_w_up[layer].transpose(0, 2, 1, 3).reshape(N_GROUPS, d, ff).astype(BF16)
        wd = moe_w_down[layer].reshape(N_GROUPS, ff, d).astype(BF16)
        ys = _group_experts(xs, tile_group, n_used, wg, wu, wd)
        h = _combine(ys, a2, h, mod, pc, off, final_norm_w, T, final=layer == depth - 1)

    return h
```
